```python
import jax, jax.numpy as jnp
from jax import lax
import numpy as np

D_MODEL = 1024
BATCH = 4
SEQ = 8192
DEPTH = 1
DEC_BATCH = 128
DEC_SEQ = 4
PAST_LEN = 16384
PAGE_SIZE = 128

RW_HEADS = 8
RW_HEAD_DIM = 64
RW_WIDTH = RW_HEADS * RW_HEAD_DIM
DECAY_LORA = 64
A_LORA = 64
GATE_LORA = 128
RW_PROJ = 3 * RW_WIDTH + DECAY_LORA + A_LORA + GATE_LORA
RW_SPLITS = (RW_WIDTH, 2 * RW_WIDTH, 3 * RW_WIDTH, 3 * RW_WIDTH + DECAY_LORA, 3 * RW_WIDTH + DECAY_LORA + A_LORA)
LNX_EPS = 64e-5
MLA_HEADS = 8
QK_NOPE = 64
QK_ROPE = 32
V_DIM = 64
Q_LORA = 384
KV_LORA = 256
MLA_WIDTH = MLA_HEADS * V_DIM
MLA_PROJ = Q_LORA + KV_LORA + QK_ROPE
MLA_SCALE = (QK_NOPE + QK_ROPE) ** -0.5
ROPE_THETA = 10000.0
Q_BLOCK = 128
IN_PROJ = MLA_PROJ + RW_PROJ
MIX_WIDTH = RW_WIDTH + MLA_WIDTH
N_MEM = 256
MEM_HEADS = 4
MEM_HEAD_DIM = D_MODEL // MEM_HEADS
MEM_SCALE = MEM_HEAD_DIM ** -0.5
D_FF = 2816
CONV_W = 3
NORM_EPS = 1e-6

kernel_name = 'hymba_rwkv7_mla_memory_convffn_step'


def rmsnorm(x, g, eps=NORM_EPS):
    xf = x.astype(jnp.float32)
    y = xf * lax.rsqrt(jnp.mean(xf * xf, axis=-1, keepdims=True) + eps)
    return (y * g.astype(jnp.float32)).astype(x.dtype)


def rope(x, pos):
    half = QK_ROPE // 2
    inv = 1.0 / (ROPE_THETA ** (jnp.arange(half, dtype=jnp.float32) / half))
    ang = pos.astype(jnp.float32)[:, None] * inv[None, :]
    ang = ang.reshape(ang.shape[:1] + (1,) * (x.ndim - 3) + (half,))
    cos, sin = jnp.cos(ang), jnp.sin(ang)
    xf = x.astype(jnp.float32)
    x1, x2 = xf[..., :half], xf[..., half:]
    return jnp.concatenate([x1 * cos - x2 * sin, x1 * sin + x2 * cos], axis=-1).astype(x.dtype)


def rwkv7_group(p, shift_prev, state0, mu, w0, w2, a0, a2, g2, k_k, k_a, r_k, lnx_g, lnx_b):
    B, T, _ = p.shape
    p_prev = jnp.concatenate([shift_prev[:, None, :].astype(p.dtype), p[:, :-1]], axis=1)
    s = p + (p_prev - p) * mu
    r, k, v, wd, ad, gd = jnp.split(s, RW_SPLITS, axis=-1)
    w = -jax.nn.softplus(-(w0 + jnp.tanh(wd) @ w2).astype(jnp.float32)) - 0.5
    decay = jnp.exp(-jnp.exp(w))
    a = jax.nn.sigmoid(a0 + ad @ a2)
    g = jax.nn.sigmoid(gd) @ g2
    heads = lambda t: t.astype(jnp.float32).reshape(B, T, RW_HEADS, RW_HEAD_DIM)
    kk = heads(k * k_k)
    kk = kk / jnp.maximum(jnp.sqrt(jnp.sum(kk * kk, axis=-1, keepdims=True)), 1e-12)
    k = k * (1.0 + (a - 1.0) * k_a)
    r_h, k_h, v_h, a_h, d_h = heads(r), heads(k), heads(v), heads(a), heads(decay)

    def step(S, inp):
        r_t, d_t, k_t, v_t, kk_t, a_t = inp
        sa = -jnp.einsum('bhvk,bhk->bhv', S, kk_t)
        S = (S * d_t[:, :, None, :] + sa[..., None] * (kk_t * a_t)[:, :, None, :]
             + v_t[..., None] * k_t[:, :, None, :])
        return S, jnp.einsum('bhvk,bhk->bhv', S, r_t)

    xs = tuple(jnp.moveaxis(t, 1, 0) for t in (r_h, d_h, k_h, v_h, kk, a_h))
    state, out = lax.scan(step, state0.astype(jnp.float32), xs)
    out = jnp.moveaxis(out, 0, 1)
    mean = jnp.mean(out, axis=-1, keepdims=True)
    var = jnp.mean(jnp.square(out - mean), axis=-1, keepdims=True)
    y = ((out - mean) * lax.rsqrt(var + LNX_EPS)).reshape(B, T, RW_WIDTH)
    y = y * lnx_g.astype(jnp.float32) + lnx_b.astype(jnp.float32)
    bonus = jnp.sum(r_h * k_h * r_k.astype(jnp.float32), axis=-1, keepdims=True) * v_h
    y = (y + bonus.reshape(B, T, RW_WIDTH)) * g.astype(jnp.float32)
    return y.astype(p.dtype), state, p[:, -1]


def mla_full_causal(q_nope, q_rope, lat, k_rope, w_ukv):
    B, T = lat.shape[:2]
    kv = (lat @ w_ukv).reshape(B, T, MLA_HEADS, QK_NOPE + V_DIM)
    k_nope, v = kv[..., :QK_NOPE], kv[..., QK_NOPE:]
    kpos = jnp.arange(T)

    def block(i):
        s0 = i * Q_BLOCK
        qn = lax.dynamic_slice_in_dim(q_nope, s0, Q_BLOCK, axis=1)
        qr = lax.dynamic_slice_in_dim(q_rope, s0, Q_BLOCK, axis=1)
        sc = jnp.einsum('bqhd,bkhd->bhqk', qn, k_nope) + jnp.einsum('bqhr,bkr->bhqk', qr, k_rope)
        sc = sc.astype(jnp.float32) * MLA_SCALE
        qpos = s0 + jnp.arange(Q_BLOCK)
        sc = jnp.where(kpos[None, :] <= qpos[:, None], sc, -jnp.inf)
        pr = jax.nn.softmax(sc, axis=-1).astype(v.dtype)
        return jnp.einsum('bhqk,bkhd->bqhd', pr, v)

    out = lax.map(block, jnp.arange(T // Q_BLOCK))
    return jnp.moveaxis(out, 0, 1).reshape(B, T, MLA_WIDTH)


def mla_paged_step(q_nope, q_rope, lat_new, kr_new, lat_past, kr_past, w_ukv):
    B, T = lat_new.shape[:2]
    P = lat_past.shape[1]
    w = w_ukv.reshape(KV_LORA, MLA_HEADS, QK_NOPE + V_DIM)
    w_uk, w_uv = w[..., :QK_NOPE], w[..., QK_NOPE:]
    q_abs = jnp.einsum('bthd,chd->bthc', q_nope, w_uk)
    sc_past = jnp.einsum('bthc,bsc->bhts', q_abs, lat_past) + jnp.einsum('bthr,bsr->bhts', q_rope, kr_past)
    sc_new = jnp.einsum('bthc,bsc->bhts', q_abs, lat_new) + jnp.einsum('bthr,bsr->bhts', q_rope, kr_new)
    causal = jnp.arange(T)[None, :] <= jnp.arange(T)[:, None]
    sc_new = jnp.where(causal, sc_new.astype(jnp.float32) * MLA_SCALE, -jnp.inf)
    sc = jnp.concatenate([sc_past.astype(jnp.float32) * MLA_SCALE, sc_new], axis=-1)
    pr = jax.nn.softmax(sc, axis=-1).astype(lat_new.dtype)
    o_lat = (jnp.einsum('bhts,bsc->bthc', pr[..., :P], lat_past)
             + jnp.einsum('bhts,bsc->bthc', pr[..., P:], lat_new))
    o = jnp.einsum('bthc,chd->bthd', o_lat, w_uv)
    return o.reshape(B, T, MLA_WIDTH)


def memory_kv(mem, g, w_k, w_v):
    B = mem.shape[0]
    m = rmsnorm(mem, g)
    k = (m @ w_k).reshape(B, -1, MEM_HEADS, MEM_HEAD_DIM)
    v = (m @ w_v).reshape(B, -1, MEM_HEADS, MEM_HEAD_DIM)
    return k, v


def memory_attn(h, mem_k, mem_v, w_q, w_out):
    B, T, _ = h.shape
    q = (h @ w_q).reshape(B, T, MEM_HEADS, MEM_HEAD_DIM)
    sc = jnp.einsum('bthd,bmhd->bhtm', q, mem_k.astype(q.dtype)).astype(jnp.float32) * MEM_SCALE
    pr = jax.nn.softmax(sc, axis=-1).astype(q.dtype)
    o = jnp.einsum('bhtm,bmhd->bthd', pr, mem_v.astype(q.dtype)).reshape(B, T, D_MODEL)
    return o @ w_out


def conv_ffn(h, conv_prev, w_up, conv_w, conv_b, w_down):
    T = h.shape[1]
    u = h @ w_up
    up = jnp.concatenate([conv_prev.astype(u.dtype), u], axis=1)
    c = conv_b + sum(conv_w[j] * up[:, j:j + T] for j in range(CONV_W))
    gate, val = jnp.split(c, 2, axis=-1)
    return (jax.nn.silu(gate) * val) @ w_down, up[:, T:]


def _layer(x, pos, rw_shift, rw_state, conv_prev, mem_k, mem_v, lp, past):
    B, T, _ = x.shape
    h = rmsnorm(x, lp['g_mix'])
    proj = h @ lp['w_in']
    p_mla, p_rw = proj[..., :MLA_PROJ], proj[..., MLA_PROJ:]
    c_q = p_mla[..., :Q_LORA]
    c_kv = p_mla[..., Q_LORA:Q_LORA + KV_LORA]
    k_r = p_mla[..., Q_LORA + KV_LORA:]
    q = (rmsnorm(c_q, lp['q_norm_g']) @ lp['w_uq']).reshape(B, T, MLA_HEADS, QK_NOPE + QK_ROPE)
    q_nope, q_rope = q[..., :QK_NOPE], rope(q[..., QK_NOPE:], pos)
    lat = rmsnorm(c_kv, lp['kv_norm_g'])
    k_rope = rope(k_r, pos)
    if past is None:
        o_mla = mla_full_causal(q_nope, q_rope, lat, k_rope, lp['w_ukv'])
    else:
        o_mla = mla_paged_step(q_nope, q_rope, lat, k_rope, past[0], past[1], lp['w_ukv'])
    o_mla = rmsnorm(o_mla, lp['g_mla_out'])
    o_rw, rw_state_new, rw_shift_new = rwkv7_group(
        p_rw, rw_shift, rw_state, lp['rw_mu'], lp['rw_w0'], lp['rw_w2'], lp['rw_a0'], lp['rw_a2'],
        lp['rw_g2'], lp['rw_k_k'], lp['rw_k_a'], lp['rw_r_k'], lp['rw_lnx_g'], lp['rw_lnx_b'])
    x = x + jnp.concatenate([o_rw, o_mla], axis=-1) @ lp['w_o']
    x = x + memory_attn(rmsnorm(x, lp['g_mem_q']), mem_k, mem_v, lp['w_mq'], lp['w_mo'])
    f, conv_new = conv_ffn(rmsnorm(x, lp['g_ffn']), conv_prev, lp['w_up'], lp['conv_w'], lp['conv_b'], lp['w_down'])
    x = x + f
    return x, (lat, k_rope, rw_state_new, rw_shift_new, conv_new)


def setup_inputs(seed: int = 0) -> dict:
    key = jax.random.key(seed)
    keys = iter(jax.random.split(key, 64))
    nrm = lambda shape, scale: jax.random.normal(next(keys), shape, jnp.float32) * scale
    gain = lambda shape: 1.0 + nrm(shape, 0.05)
    L = DEPTH
    F2 = 2 * D_FF
    n_pages = PAST_LEN // PAGE_SIZE
    n_used = DEC_BATCH * n_pages
    n_phys = n_used + max(1, n_used // 4)
    perm = jax.random.permutation(next(keys), n_phys)
    page_table = perm[:n_used].reshape(DEC_BATCH, n_pages).astype(jnp.int32)
    return {
        'x_prompt': nrm((BATCH, SEQ, D_MODEL), 1.0),
        'x_sample': nrm((DEC_BATCH, DEC_SEQ, D_MODEL), 1.0),
        'cache_mla_latent': nrm((L, n_phys, PAGE_SIZE, KV_LORA), 1.0),
        'cache_mla_krope': nrm((L, n_phys, PAGE_SIZE, QK_ROPE), 1.0),
        'cache_mem_k': nrm((L, DEC_BATCH, N_MEM, MEM_HEADS, MEM_HEAD_DIM), 1.0),
        'cache_mem_v': nrm((L, DEC_BATCH, N_MEM, MEM_HEADS, MEM_HEAD_DIM), 1.0),
        'state_rwkv': nrm((L, DEC_BATCH, RW_HEADS, RW_HEAD_DIM, RW_HEAD_DIM), 0.5),
        'state_rwkv_shift': nrm((L, DEC_BATCH, RW_PROJ), 1.0),
        'state_ffn_conv': nrm((L, DEC_BATCH, CONV_W - 1, F2), 1.0),
        'page_table': page_table,
        'mem_prompt': nrm((BATCH, N_MEM, D_MODEL), 1.0),
        'g_mix': gain((L, D_MODEL)),
        'w_in': nrm((L, D_MODEL, IN_PROJ), D_MODEL ** -0.5),
        'q_norm_g': gain((L, Q_LORA)),
        'kv_norm_g': gain((L, KV_LORA)),
        'w_uq': nrm((L, Q_LORA, MLA_HEADS * (QK_NOPE + QK_ROPE)), Q_LORA ** -0.5),
        'w_ukv': nrm((L, KV_LORA, MLA_HEADS * (QK_NOPE + V_DIM)), KV_LORA ** -0.5),
        'g_mla_out': gain((L, MLA_WIDTH)),
        'rw_mu': jax.random.uniform(next(keys), (L, RW_PROJ), jnp.float32),
        'rw_w0': nrm((L, RW_WIDTH), 0.5),
        'rw_w2': nrm((L, DECAY_LORA, RW_WIDTH), DECAY_LORA ** -0.5),
        'rw_a0': nrm((L, RW_WIDTH), 0.1),
        'rw_a2': nrm((L, A_LORA, RW_WIDTH), A_LORA ** -0.5),
        'rw_g2': nrm((L, GATE_LORA, RW_WIDTH), GATE_LORA ** -0.5),
        'rw_k_k': 0.85 + nrm((L, RW_WIDTH), 0.05),
        'rw_k_a': 1.0 + nrm((L, RW_WIDTH), 0.05),
        'rw_r_k': nrm((L, RW_HEADS, RW_HEAD_DIM), 0.1),
        'rw_lnx_g': gain((L, RW_WIDTH)),
        'rw_lnx_b': nrm((L, RW_WIDTH), 0.01),
        'w_o': nrm((L, MIX_WIDTH, D_MODEL), MIX_WIDTH ** -0.5),
        'g_mem_q': gain((L, D_MODEL)),
        'g_mem_kv': gain((L, D_MODEL)),
        'w_mq': nrm((L, D_MODEL, D_MODEL), D_MODEL ** -0.5),
        'w_mk': nrm((L, D_MODEL, D_MODEL), D_MODEL ** -0.5),
        'w_mv': nrm((L, D_MODEL, D_MODEL), D_MODEL ** -0.5),
        'w_mo': nrm((L, D_MODEL, D_MODEL), D_MODEL ** -0.5),
        'g_ffn': gain((L, D_MODEL)),
        'w_up': nrm((L, D_MODEL, F2), D_MODEL ** -0.5),
        'conv_w': nrm((L, CONV_W, F2), CONV_W ** -0.5),
        'conv_b': nrm((L, F2), 0.01),
        'w_down': nrm((L, D_FF, D_MODEL), D_FF ** -0.5),
        'g_final': gain((D_MODEL,)),
    }


def _stack(outs, i):
    return jnp.stack([o[i] for o in outs])


def reference(x_prompt, x_sample, cache_mla_latent, cache_mla_krope, cache_mem_k, cache_mem_v,
              state_rwkv, state_rwkv_shift, state_ffn_conv, page_table, mem_prompt,
              g_mix, w_in, q_norm_g, kv_norm_g, w_uq, w_ukv, g_mla_out,
              rw_mu, rw_w0, rw_w2, rw_a0, rw_a2, rw_g2, rw_k_k, rw_k_a, rw_r_k, rw_lnx_g, rw_lnx_b,
              w_o, g_mem_q, g_mem_kv, w_mq, w_mk, w_mv, w_mo,
              g_ffn, w_up, conv_w, conv_b, w_down, g_final):
    B, T_p, _ = x_prompt.shape
    Bs, T_s, _ = x_sample.shape
    past_len = page_table.shape[1] * PAGE_SIZE
    pos_p = jnp.arange(T_p)
    pos_s = past_len + jnp.arange(T_s)
    xp, xs = x_prompt, x_sample
    outs_p, outs_s = [], []
    for l in range(DEPTH):
        lp = dict(g_mix=g_mix[l], w_in=w_in[l], q_norm_g=q_norm_g[l], kv_norm_g=kv_norm_g[l],
                  w_uq=w_uq[l], w_ukv=w_ukv[l], g_mla_out=g_mla_out[l], rw_mu=rw_mu[l],
                  rw_w0=rw_w0[l], rw_w2=rw_w2[l], rw_a0=rw_a0[l], rw_a2=rw_a2[l], rw_g2=rw_g2[l],
                  rw_k_k=rw_k_k[l], rw_k_a=rw_k_a[l], rw_r_k=rw_r_k[l], rw_lnx_g=rw_lnx_g[l],
                  rw_lnx_b=rw_lnx_b[l], w_o=w_o[l], g_mem_q=g_mem_q[l], w_mq=w_mq[l], w_mo=w_mo[l],
                  g_ffn=g_ffn[l], w_up=w_up[l], conv_w=conv_w[l], conv_b=conv_b[l], w_down=w_down[l])
        mk_p, mv_p = memory_kv(mem_prompt, g_mem_kv[l], w_mk[l], w_mv[l])
        xp, st_p = _layer(xp, pos_p,
                          jnp.zeros((B, RW_PROJ), xp.dtype),
                          jnp.zeros((B, RW_HEADS, RW_HEAD_DIM, RW_HEAD_DIM), jnp.float32),
                          jnp.zeros((B, CONV_W - 1, 2 * D_FF), xp.dtype),
                          mk_p, mv_p, lp, None)
        lat_past = cache_mla_latent[l][page_table].reshape(Bs, past_len, KV_LORA)
        kr_past = cache_mla_krope[l][page_table].reshape(Bs, past_len, QK_ROPE)
        xs, st_s = _layer(xs, pos_s, state_rwkv_shift[l], state_rwkv[l], state_ffn_conv[l],
                          cache_mem_k[l], cache_mem_v[l], lp, (lat_past, kr_past))
        outs_p.append(st_p + (mk_p, mv_p))
        outs_s.append(st_s)
    y_prompt = rmsnorm(xp, g_final)
    y_sample = rmsnorm(xs, g_final)
    return (y_prompt, y_sample,
            _stack(outs_p, 0), _stack(outs_p, 1), _stack(outs_p, 5), _stack(outs_p, 6),
            _stack(outs_p, 2), _stack(outs_p, 3), _stack(outs_p, 4),
            _stack(outs_s, 0), _stack(outs_s, 1), _stack(outs_s, 2), _stack(outs_s, 3), _stack(outs_s, 4))
```

```python
import functools

import numpy as np
import jax
import jax.numpy as jnp
from jax import lax
from jax.experimental import pallas as pl
from jax.experimental.pallas import tpu as pltpu

F32 = jnp.float32
BF16 = jnp.bfloat16

RW_HEADS = 8
RW_HEAD_DIM = 64
RW_WIDTH = 512
DECAY_LORA = 64
A_LORA = 64
GATE_LORA = 128
RW_PROJ = 3 * RW_WIDTH + DECAY_LORA + A_LORA + GATE_LORA
LNX_EPS = 64e-5
MLA_HEADS = 8
QK_NOPE = 64
QK_ROPE = 32
V_DIM = 64
Q_LORA = 384
KV_LORA = 256
MLA_PROJ = Q_LORA + KV_LORA + QK_ROPE
MLA_SCALE = (QK_NOPE + QK_ROPE) ** -0.5
ROPE_THETA = 10000.0
MEM_HEADS = 4
CONV_W = 3
NORM_EPS = 1e-6
PAGE_SIZE = 128

LANE = 128
SUBLANE = 8
VMEM_LIMIT = 56 * 1024 * 1024
HEAD_PAD = LANE
ROPE_LO = QK_NOPE
SROWS = 8
SLO = 2
NEG = -1e30
HI = lax.Precision.HIGHEST


def _cparams(*sem):
    return pltpu.CompilerParams(dimension_semantics=sem, vmem_limit_bytes=VMEM_LIMIT)


def _rms(x, g, eps=NORM_EPS):
    return x * lax.rsqrt(jnp.mean(x * x, axis=-1, keepdims=True) + eps) * g


def _dot(a, b):
    return jnp.dot(a, b, preferred_element_type=F32)


def _dot_nt(a, b, precision=None):
    return lax.dot_general(a, b, (((1,), (1,)), ((), ())), preferred_element_type=F32, precision=precision)


def _dot_tn(a, b, precision=None):
    return lax.dot_general(a, b, (((0,), (0,)), ((), ())), preferred_element_type=F32, precision=precision)


def _doth(a, b):
    return jnp.dot(a, b, preferred_element_type=F32, precision=HI)


C_Q0, C_KV0, C_KR0, C_KRS0, C_RW0 = 0, Q_LORA, Q_LORA + KV_LORA, Q_LORA + KV_LORA + LANE, Q_LORA + KV_LORA + 2 * LANE
W1_COLS = C_RW0 + RW_PROJ


def _inproj_kernel(x_ref, g_ref, w1_ref, qg_ref, kvg_ref, wq_ref, cq_ref, sq_ref, ck_ref, sk_ref, *rest, with_kv):
    if with_kv:
        wk_ref, wv_ref, q_out, lat_out, kr_out, prw_out, k_out, v_out = rest
    else:
        q_out, lat_out, kr_out, prw_out = rest
    h = _rms(x_ref[...], g_ref[...]).astype(BF16)
    prw_out[...] = _dot(h, w1_ref[:, C_RW0:W1_COLS])
    pm = _dot(h, w1_ref[:, 0:C_RW0])
    cqn = _rms(pm[:, C_Q0:C_KV0], qg_ref[...]).astype(BF16)
    lat = _rms(pm[:, C_KV0:C_KR0], kvg_ref[...])
    lat_out[...] = lat
    krope = pm[:, C_KR0:C_KRS0] * ck_ref[...] + pm[:, C_KRS0:C_RW0] * sk_ref[...]
    kr_out[...] = krope[:, ROPE_LO:ROPE_LO + QK_ROPE]
    q12 = _dot(cqn, wq_ref[...])
    nq = MLA_HEADS * HEAD_PAD
    cq, sq = cq_ref[...], sq_ref[...]
    for hd in range(MLA_HEADS):
        a, b = hd * HEAD_PAD, (hd + 1) * HEAD_PAD
        q_out[:, a:b] = (q12[:, a:b] * cq + q12[:, nq + a:nq + b] * sq).astype(BF16)
    if with_kv:
        latb = lat.astype(BF16)
        kn = _dot(latb, wk_ref[...])
        for hd in range(MLA_HEADS):
            a, b = hd * HEAD_PAD, (hd + 1) * HEAD_PAD
            k_out[:, a:b] = (kn[:, a:b] + krope).astype(BF16)
        v_out[...] = _dot(latb, wv_ref[...]).astype(BF16)


def _inproj(x, g, w1, qg, kvg, wq, tabs, wk, wv, *, tm, with_kv):
    n, d = x.shape
    ttab = tabs[0].shape[0]
    nt = ttab // tm
    row = lambda i: (i, 0)
    const = lambda i: (0, 0)
    tab = lambda i: (i % nt, 0)
    in_specs = [pl.BlockSpec((tm, d), row), pl.BlockSpec(g.shape, const), pl.BlockSpec(w1.shape, const),
                pl.BlockSpec(qg.shape, const), pl.BlockSpec(kvg.shape, const), pl.BlockSpec(wq.shape, const)]
    in_specs += [pl.BlockSpec((tm, LANE), tab)] * 4
    args = [x, g, w1, qg, kvg, wq, *tabs]
    nq = MLA_HEADS * HEAD_PAD
    out_shape = [jax.ShapeDtypeStruct((n, nq), BF16), jax.ShapeDtypeStruct((n, KV_LORA), F32),
                 jax.ShapeDtypeStruct((n, QK_ROPE), F32), jax.ShapeDtypeStruct((n, RW_PROJ), F32)]
    out_specs = [pl.BlockSpec((tm, nq), row), pl.BlockSpec((tm, KV_LORA), row),
                 pl.BlockSpec((tm, QK_ROPE), row), pl.BlockSpec((tm, RW_PROJ), row)]
    if with_kv:
        in_specs += [pl.BlockSpec(wk.shape, const), pl.BlockSpec(wv.shape, const)]
        args += [wk, wv]
        out_shape += [jax.ShapeDtypeStruct((n, nq), BF16), jax.ShapeDtypeStruct((n, nq), BF16)]
        out_specs += [pl.BlockSpec((tm, nq), row), pl.BlockSpec((tm, nq), row)]
    return pl.pallas_call(
        functools.partial(_inproj_kernel, with_kv=with_kv), grid=(n // tm,), in_specs=in_specs,
        out_specs=out_specs, out_shape=out_shape, compiler_params=_cparams("parallel"),
        name="inproj_kv" if with_kv else "inproj")(*args)


FLASH_HPB = 2


def _flash_kernel(q_ref, k_ref, v_ref, o_ref, *, tq):
    qi = pl.program_id(2)
    qs = [q_ref[:, h * HEAD_PAD:(h + 1) * HEAD_PAD] for h in range(FLASH_HPB)]
    rows = lax.broadcasted_iota(jnp.int32, (tq, tq), 0)
    cols = lax.broadcasted_iota(jnp.int32, (tq, tq), 1)

    def step(j, carry, diagonal):
        out = []
        start = pl.multiple_of(j * tq, tq)
        for h in range(FLASH_HPB):
            m, l, acc = carry[h]
            kb = k_ref[pl.ds(start, tq), h * HEAD_PAD:(h + 1) * HEAD_PAD]
            vb = v_ref[pl.ds(start, tq), h * HEAD_PAD:(h + 1) * HEAD_PAD]
            s = _dot_nt(qs[h], kb)
            if diagonal:
                s = jnp.where(cols <= rows, s, NEG)
            m_new = jnp.maximum(m, jnp.max(s, axis=-1, keepdims=True))
            alpha = jnp.exp(m - m_new)
            p = jnp.exp(s - m_new)
            l = alpha * l + jnp.sum(p, axis=-1, keepdims=True)
            acc = alpha * acc + _dot(p.astype(BF16), vb)
            out.append((m_new, l, acc))
        return tuple(out)

    init = tuple((jnp.full((tq, 1), NEG, F32), jnp.zeros((tq, 1), F32), jnp.zeros((tq, HEAD_PAD), F32))
                 for _ in range(FLASH_HPB))
    carry = lax.fori_loop(0, qi, lambda j, c: step(j, c, False), init)
    carry = step(qi, carry, True)
    for h in range(FLASH_HPB):
        _, l, acc = carry[h]
        o_ref[:, h * V_DIM:(h + 1) * V_DIM] = (acc / l)[:, :V_DIM]


def _flash(q, k, v, *, batch, seq, tq):
    nqb = seq // tq
    grid = (batch, MLA_HEADS // FLASH_HPB, nqb)
    w = FLASH_HPB * HEAD_PAD
    return pl.pallas_call(
        functools.partial(_flash_kernel, tq=tq), grid=grid,
        in_specs=[pl.BlockSpec((tq, w), lambda b, h, i: (b * nqb + i, h)),
                  pl.BlockSpec((seq, w), lambda b, h, i: (b, h)),
                  pl.BlockSpec((seq, w), lambda b, h, i: (b, h))],
        out_specs=pl.BlockSpec((tq, FLASH_HPB * V_DIM), lambda b, h, i: (b * nqb + i, h)),
        out_shape=jax.ShapeDtypeStruct((batch * seq, MLA_HEADS * V_DIM), F32),
        compiler_params=_cparams("parallel", "parallel", "arbitrary"), name="mla_flash")(q, k, v)


PAGES_PER_STEP = 16


def _paged_kernel(pt_ref, qa_ref, qr_ref, latn_ref, krn_ref, *rest, n_new):
    del pt_ref
    g = PAGES_PER_STEP
    lat_refs, kr_refs = rest[:g], rest[g:2 * g]
    o_ref, m_s, l_s, acc_s = rest[2 * g:]
    step = pl.program_id(1)

    @pl.when(step == 0)
    def _():
        m_s[...] = jnp.full(m_s.shape, NEG, F32)
        l_s[...] = jnp.zeros(l_s.shape, F32)
        acc_s[...] = jnp.zeros(acc_s.shape, F32)

    qa = qa_ref[0]
    qr = qr_ref[0]

    def update(ss, vals):
        m = m_s[...]
        m_new = m
        for s in ss:
            m_new = jnp.maximum(m_new, jnp.max(s, axis=-1, keepdims=True))
        alpha = jnp.exp(m - m_new)
        l = alpha * l_s[...]
        acc = alpha * acc_s[...]
        for s, vb in zip(ss, vals):
            p = jnp.exp(s - m_new)
            l = l + jnp.sum(p, axis=-1, keepdims=True)
            acc = acc + _dot(p.astype(BF16), vb)
        m_s[...] = m_new
        l_s[...] = l
        acc_s[...] = acc

    ss, vals = [], []
    for j in range(g):
        latb = lat_refs[j][0].astype(BF16)
        krb = kr_refs[j][0].astype(BF16)
        ss.append(_dot_nt(qa, latb) + _dot_nt(qr, krb))
        vals.append(latb)
    update(ss, vals)

    @pl.when(step == pl.num_programs(1) - 1)
    def _():
        latn = latn_ref[0].astype(BF16)
        krn = krn_ref[0].astype(BF16)
        s = _dot_nt(qa, latn) + _dot_nt(qr, krn)
        qt = lax.broadcasted_iota(jnp.int32, s.shape, 0) % n_new
        kt = lax.broadcasted_iota(jnp.int32, s.shape, 1) - SLO
        s = jnp.where((kt >= 0) & (kt <= qt), s, NEG)
        update([s], [latn])
        o_ref[0] = acc_s[...] / l_s[...]


def _paged(page_table, qa, qr, lat_new, kr_new, cache_lat, cache_kr, *, n_new):
    b, npages = page_table.shape
    g = PAGES_PER_STEP
    rows = qa.shape[1]
    lat_specs = [pl.BlockSpec((1, PAGE_SIZE, KV_LORA), functools.partial(lambda i, s, pt, j: (pt[i, s * g + j], 0, 0), j=j))
                 for j in range(g)]
    kr_specs = [pl.BlockSpec((1, PAGE_SIZE, QK_ROPE), functools.partial(lambda i, s, pt, j: (pt[i, s * g + j], 0, 0), j=j))
                for j in range(g)]
    seq3 = lambda i, s, pt: (i, 0, 0)
    grid_spec = pltpu.PrefetchScalarGridSpec(
        num_scalar_prefetch=1, grid=(b, npages // g),
        in_specs=[pl.BlockSpec((1, rows, KV_LORA), seq3), pl.BlockSpec((1, rows, QK_ROPE), seq3),
                  pl.BlockSpec((1, SROWS, KV_LORA), seq3), pl.BlockSpec((1, SROWS, QK_ROPE), seq3)] + lat_specs + kr_specs,
        out_specs=pl.BlockSpec((1, rows, KV_LORA), seq3),
        scratch_shapes=[pltpu.VMEM((rows, 1), F32), pltpu.VMEM((rows, 1), F32), pltpu.VMEM((rows, KV_LORA), F32)])
    return pl.pallas_call(
        functools.partial(_paged_kernel, n_new=n_new), grid_spec=grid_spec,
        out_shape=jax.ShapeDtypeStruct((b, rows, KV_LORA), F32),
        compiler_params=_cparams("parallel", "arbitrary"), name="mla_paged")(
            page_table, qa, qr, lat_new, kr_new, *([cache_lat] * g), *([cache_kr] * g))


def _bmm_kernel(a_ref, w_ref, o_ref):
    o_ref[0] = _dot(a_ref[...].astype(BF16), w_ref[0]).astype(o_ref.dtype)


def _bmm_cols(a, w, out_dtype):
    n = a.shape[0]
    hh, k, m = w.shape
    return pl.pallas_call(
        _bmm_kernel, grid=(hh,),
        in_specs=[pl.BlockSpec((n, k), lambda h: (0, h)), pl.BlockSpec((1, k, m), lambda h: (h, 0, 0))],
        out_specs=pl.BlockSpec((1, n, m), lambda h: (h, 0, 0)),
        out_shape=jax.ShapeDtypeStruct((hh, n, m), out_dtype), compiler_params=_cparams("parallel"),
        name="bmm_cols")(a, w)


def _bmm_kernel3(a_ref, w_ref, o_ref):
    o_ref[0] = _dot(a_ref[0].astype(BF16), w_ref[0]).astype(o_ref.dtype)


def _bmm(a, w, out_dtype):
    hh, n, k = a.shape
    m = w.shape[2]
    return pl.pallas_call(
        _bmm_kernel3, grid=(hh,),
        in_specs=[pl.BlockSpec((1, n, k), lambda h: (h, 0, 0)), pl.BlockSpec((1, k, m), lambda h: (h, 0, 0))],
        out_specs=pl.BlockSpec((1, n, m), lambda h: (h, 0, 0)),
        out_shape=jax.ShapeDtypeStruct((hh, n, m), out_dtype), compiler_params=_cparams("parallel"),
        name="bmm")(a, w)


PAIRS = RW_HEADS // 2
C_R, C_K, C_V, C_WA, C_G = 0, RW_WIDTH, 2 * RW_WIDTH, 3 * RW_WIDTH, 3 * RW_WIDTH + DECAY_LORA + A_LORA


def _seg_sum(x):
    lane = lax.broadcasted_iota(jnp.int32, (x.shape[0], LANE), 1)
    low = lane < RW_HEAD_DIM
    outs = []
    for t in range(x.shape[1] // LANE):
        xt = x[:, t * LANE:(t + 1) * LANE]
        s0 = jnp.sum(jnp.where(low, xt, 0.0), axis=-1, keepdims=True)
        s1 = jnp.sum(jnp.where(low, 0.0, xt), axis=-1, keepdims=True)
        outs.append(jnp.where(low, s0, s1))
    return outs[0] if len(outs) == 1 else jnp.concatenate(outs, axis=-1)


def _rwkv_kernel(p_ref, s0_ref, mu_ref, w0_ref, wa2_ref, a0_ref, g2_ref, kk_ref, ka_ref, rk_ref, lg_ref, lb_ref,
                 o_ref, so_ref, carry_s, r_s, k_s, v_s, kk_s, b_s, ld_s, out_s, *, tb, ch, row_lo, row_hi):
    i = pl.program_id(1)

    @pl.when(i == 0)
    def _():
        carry_s[...] = jnp.zeros(carry_s.shape, F32)
        so_ref[0] = s0_ref[0]

    p = p_ref[0]
    rows1 = lax.broadcasted_iota(jnp.int32, (tb, 1), 0)
    prev = jnp.where(rows1 == 0, carry_s[SUBLANE - 1:SUBLANE, :], pltpu.roll(p, 1, axis=0))
    carry_s[...] = p[tb - SUBLANE:tb, :]
    s = p + (prev - p) * mu_ref[...]
    r = s[:, C_R:C_K]
    k = s[:, C_K:C_V]
    v = s[:, C_V:C_WA]
    wa = s[:, C_WA:C_G]
    lane = lax.broadcasted_iota(jnp.int32, wa.shape, 1)
    z = jnp.where(lane < DECAY_LORA, jnp.tanh(wa), wa).astype(BF16)
    lin = _dot(z, wa2_ref[...])
    xw = -(w0_ref[...] + lin[:, :RW_WIDTH])
    w = -(jnp.maximum(xw, 0.0) + jnp.log(1.0 + jnp.exp(-jnp.abs(xw)))) - 0.5
    logd = -jnp.exp(w)
    a = jax.nn.sigmoid(a0_ref[...] + lin[:, RW_WIDTH:])
    gate = _dot(jax.nn.sigmoid(s[:, C_G:RW_PROJ]).astype(BF16), g2_ref[...])
    kk = k * kk_ref[...]
    kk = kk / jnp.maximum(jnp.sqrt(_seg_sum(kk * kk)), 1e-12)
    k = k * (1.0 + (a - 1.0) * ka_ref[...])
    bonus = _seg_sum(r * k * rk_ref[...]) * v
    if row_lo > 0 or row_hi < tb:
        live = (rows1 >= row_lo) & (rows1 < row_hi)
        logd = jnp.where(live, logd, 0.0)
        kk = jnp.where(live, kk, 0.0)
        k = jnp.where(live, k, 0.0)
        v = jnp.where(live, v, 0.0)
    r_s[...] = r
    k_s[...] = k
    v_s[...] = v
    kk_s[...] = kk
    b_s[...] = kk * a
    ld_s[...] = logd

    c2 = 2 * ch
    ri = lax.broadcasted_iota(jnp.int32, (c2, c2), 0)
    ci = lax.broadcasted_iota(jnp.int32, (c2, c2), 1)
    strict = (ci % ch) < (ri % ch)
    incl = (ci % ch) <= (ri % ch)
    eye = (ri == ci).astype(F32)
    tri = (lax.broadcasted_iota(jnp.int32, (ch, ch), 1) <= lax.broadcasted_iota(jnp.int32, (ch, ch), 0)).astype(F32)
    lane2 = lax.broadcasted_iota(jnp.int32, (c2, LANE), 1)
    row2 = lax.broadcasted_iota(jnp.int32, (c2, LANE), 0)
    own = (lane2 < RW_HEAD_DIM) == (row2 < ch)

    def stack(x):
        return jnp.where(own, jnp.concatenate([x, x], axis=0), 0.0)

    def chunk(c, _):
        t0 = pl.multiple_of(c * ch, ch)
        for pr in range(PAIRS):
            ls = slice(pr * LANE, (pr + 1) * LANE)
            ld = ld_s[pl.ds(t0, ch), ls]
            cs = _doth(tri, ld)
            tot = cs[ch - 1:ch, :]
            e_in, e_ex, e_inv, e_tail = jnp.exp(cs), jnp.exp(cs - ld), jnp.exp(-cs), jnp.exp(tot - cs)
            kc, bc = k_s[pl.ds(t0, ch), ls], b_s[pl.ds(t0, ch), ls]
            lhs = jnp.concatenate([stack(kk_s[pl.ds(t0, ch), ls] * e_ex), stack(r_s[pl.ds(t0, ch), ls] * e_in)], axis=0)
            rhs = jnp.concatenate([stack(kc * e_inv), stack(bc * e_inv)], axis=0)
            amat = _dot_nt(lhs, rhs, HI)
            a_kk = jnp.where(strict, amat[:c2, :c2], 0.0)
            a_kb = jnp.where(strict, amat[:c2, c2:], 0.0)
            a_rk = jnp.where(incl, amat[c2:, :c2], 0.0)
            a_rb = jnp.where(incl, amat[c2:, c2:], 0.0)
            inv = eye - jnp.where((ri // 2 == ci // 2), a_kb, 0.0)
            m = 2
            while m < ch:
                e = jnp.where((ri // (2 * m) == ci // (2 * m)) & ((ri // m) % 2 == 1) & ((ci // m) % 2 == 0), a_kb, 0.0)
                inv = inv - _doth(_doth(inv, e), inv)
                m *= 2
            st = so_ref[0, pr]
            gmat = _dot_nt(lhs, st, HI)
            vs = stack(v_s[pl.ds(t0, ch), ls])
            u = _doth(inv, gmat[:c2] + _doth(a_kk, vs))
            o2 = gmat[c2:] + _doth(a_rk, vs) - _doth(a_rb, u)
            out_s[pl.ds(t0, ch), ls] = o2[:ch] + o2[ch:]
            so_ref[0, pr] = st * jnp.exp(tot) + _dot_tn(vs, stack(kc * e_tail), HI) - _dot_tn(u, stack(bc * e_tail), HI)
        return 0

    lax.fori_loop(0, tb // ch, chunk, 0)

    out = out_s[...]
    mean = _seg_sum(out) * (1.0 / RW_HEAD_DIM)
    cen = out - mean
    var = _seg_sum(cen * cen) * (1.0 / RW_HEAD_DIM)
    y = cen * lax.rsqrt(var + LNX_EPS) * lg_ref[...] + lb_ref[...]
    o_ref[0] = ((y + bonus) * gate).astype(o_ref.dtype)


def _rwkv(p, s0, mu, w0, wa2, a0, g2, k_k, k_a, r_k, lnx_g, lnx_b, *, tb, ch, row_lo, row_hi):
    b, t, _ = p.shape
    const = lambda bi, i: (0, 0)
    vec = lambda a: pl.BlockSpec(a.shape, const)
    scr = lambda: pltpu.VMEM((tb, RW_WIDTH), F32)
    return pl.pallas_call(
        functools.partial(_rwkv_kernel, tb=tb, ch=ch, row_lo=row_lo, row_hi=row_hi), grid=(b, t // tb),
        in_specs=[pl.BlockSpec((1, tb, RW_PROJ), lambda bi, i: (bi, i, 0)),
                  pl.BlockSpec((1, PAIRS, LANE, LANE), lambda bi, i: (bi, 0, 0, 0)),
                  vec(mu), vec(w0), vec(wa2), vec(a0), vec(g2), vec(k_k), vec(k_a), vec(r_k), vec(lnx_g), vec(lnx_b)],
        out_specs=[pl.BlockSpec((1, tb, RW_WIDTH), lambda bi, i: (bi, i, 0)),
                   pl.BlockSpec((1, PAIRS, LANE, LANE), lambda bi, i: (bi, 0, 0, 0))],
        out_shape=[jax.ShapeDtypeStruct((b, t, RW_WIDTH), BF16), jax.ShapeDtypeStruct((b, PAIRS, LANE, LANE), F32)],
        scratch_shapes=[pltpu.VMEM((SUBLANE, RW_PROJ), F32)] + [scr() for _ in range(7)],
        compiler_params=_cparams("parallel", "arbitrary"), name="rwkv")(
            p, s0, mu, w0, wa2, a0, g2, k_k, k_a, r_k, lnx_g, lnx_b)


def _mix_kernel(x_ref, orw_ref, omla_ref, g_ref, wa_ref, wb_ref, o_ref):
    om = _rms(omla_ref[...], g_ref[...]).astype(BF16)
    o_ref[...] = x_ref[...] + _dot(orw_ref[...], wa_ref[...]) + _dot(om, wb_ref[...])


def _mix(x, orw, omla, g, wa, wb, *, tm):
    n, d = x.shape
    row = lambda i: (i, 0)
    const = lambda i: (0, 0)
    return pl.pallas_call(
        _mix_kernel, grid=(n // tm,),
        in_specs=[pl.BlockSpec((tm, d), row), pl.BlockSpec((tm, orw.shape[1]), row), pl.BlockSpec((tm, omla.shape[1]), row),
                  pl.BlockSpec(g.shape, const), pl.BlockSpec(wa.shape, const), pl.BlockSpec(wb.shape, const)],
        out_specs=pl.BlockSpec((tm, d), row), out_shape=jax.ShapeDtypeStruct((n, d), F32),
        compiler_params=_cparams("parallel"), name="mix_out")(x, orw, omla, g, wa, wb)


def _norm_mm_kernel(x_ref, g_ref, w_ref, o_ref):
    o_ref[...] = _dot(_rms(x_ref[...], g_ref[...]).astype(BF16), w_ref[...]).astype(o_ref.dtype)


def _norm_mm(x, g, w, out_dtype, *, tm):
    n, d = x.shape
    m = w.shape[1]
    return pl.pallas_call(
        _norm_mm_kernel, grid=(n // tm,),
        in_specs=[pl.BlockSpec((tm, d), lambda i: (i, 0)), pl.BlockSpec(g.shape, lambda i: (0, 0)),
                  pl.BlockSpec(w.shape, lambda i: (0, 0))],
        out_specs=pl.BlockSpec((tm, m), lambda i: (i, 0)), out_shape=jax.ShapeDtypeStruct((n, m), out_dtype),
        compiler_params=_cparams("parallel"), name="norm_mm")(x, g, w)


def _mm_res_kernel(a_ref, w_ref, x_ref, o_ref):
    o_ref[...] = x_ref[...] + _dot(a_ref[...], w_ref[...])


def _mm_res(a, w, x, *, tm):
    n, d = x.shape
    return pl.pallas_call(
        _mm_res_kernel, grid=(n // tm,),
        in_specs=[pl.BlockSpec((tm, a.shape[1]), lambda i: (i, 0)), pl.BlockSpec(w.shape, lambda i: (0, 0)),
                  pl.BlockSpec((tm, d), lambda i: (i, 0))],
        out_specs=pl.BlockSpec((tm, d), lambda i: (i, 0)), out_shape=jax.ShapeDtypeStruct((n, d), F32),
        compiler_params=_cparams("parallel"), name="mm_res")(a, w, x)


def _memattn_kernel(q_ref, k_ref, v_ref, o_ref):
    dh = q_ref.shape[2] // MEM_HEADS
    scale = dh ** -0.5
    for h in range(MEM_HEADS):
        cs = slice(h * dh, (h + 1) * dh)
        s = _dot_nt(q_ref[0, :, cs], k_ref[0, :, cs].astype(BF16)) * scale
        p = jnp.exp(s - jnp.max(s, axis=-1, keepdims=True))
        l = jnp.sum(p, axis=-1, keepdims=True)
        o_ref[0, :, cs] = (_dot(p.astype(BF16), v_ref[0, :, cs].astype(BF16)) / l).astype(o_ref.dtype)


def _memattn(q, mk, mv, *, tm):
    b, t, d = q.shape
    nm = mk.shape[1]
    return pl.pallas_call(
        _memattn_kernel, grid=(b, t // tm),
        in_specs=[pl.BlockSpec((1, tm, d), lambda bi, i: (bi, i, 0)), pl.BlockSpec((1, nm, d), lambda bi, i: (bi, 0, 0)),
                  pl.BlockSpec((1, nm, d), lambda bi, i: (bi, 0, 0))],
        out_specs=pl.BlockSpec((1, tm, d), lambda bi, i: (bi, i, 0)), out_shape=jax.ShapeDtypeStruct((b, t, d), BF16),
        compiler_params=_cparams("parallel", "parallel"), name="mem_attn")(q, mk, mv)


FFN_CHUNK = 256


def _ffn_kernel(x_ref, prev_ref, g_ref, wup_ref, cw_ref, cb_ref, wdn_ref, gf_ref, y_ref, u_ref, *, tm, nch, prev_rows):
    i = pl.program_id(1)
    x = x_ref[0]
    h = _rms(x, g_ref[...]).astype(BF16)
    rows = lax.broadcasted_iota(jnp.int32, (tm, 1), 0)
    if prev_rows is None:
        @pl.when(i == 0)
        def _():
            u_ref[0] = prev_ref[0]
    else:
        hist = (rows % SROWS) < prev_rows

    def conv(c):
        u = _dot(h, wup_ref[c])
        if prev_rows is None:
            tail = u_ref[0, c]
            u_ref[0, c] = u[tm - SUBLANE:tm, :]
            p1, p2 = tail[SUBLANE - 1:SUBLANE, :], tail[SUBLANE - 2:SUBLANE - 1, :]
            u1 = jnp.where(rows == 0, p1, pltpu.roll(u, 1, axis=0))
            u2 = jnp.where(rows == 0, p2, jnp.where(rows == 1, p1, pltpu.roll(u, 2, axis=0)))
        else:
            u = jnp.where(hist, prev_ref[c], u)
            u_ref[c] = u
            u1 = pltpu.roll(u, 1, axis=0)
            u2 = pltpu.roll(u, 2, axis=0)
        cw = cw_ref[c]
        return cb_ref[c] + cw[0:1, :] * u2 + cw[1:2, :] * u1 + cw[2:3, :] * u

    def body(c, acc):
        gate = conv(c)
        val = conv(c + nch)
        act = (gate * jax.nn.sigmoid(gate) * val).astype(BF16)
        return acc + _dot(act, wdn_ref[c])

    f = lax.fori_loop(0, nch, body, jnp.zeros((tm, x.shape[1]), F32))
    y_ref[0] = _rms(x + f, gf_ref[...])


def _ffn(x, prev, g, wup, cw, cb, wdn, gf, *, tm, prev_rows):
    b, t, d = x.shape
    nch = wdn.shape[0]
    c3 = lambda bi, i: (0, 0, 0)
    if prev_rows is None:
        prev_spec = pl.BlockSpec((1, 2 * nch, SUBLANE, FFN_CHUNK), lambda bi, i: (bi, 0, 0, 0))
        u_spec = pl.BlockSpec((1, 2 * nch, SUBLANE, FFN_CHUNK), lambda bi, i: (bi, 0, 0, 0))
        u_shape = jax.ShapeDtypeStruct((b, 2 * nch, SUBLANE, FFN_CHUNK), F32)
    else:
        prev_spec = pl.BlockSpec((2 * nch, tm, FFN_CHUNK), lambda bi, i: (0, bi * (t // tm) + i, 0))
        u_spec = pl.BlockSpec((2 * nch, tm, FFN_CHUNK), lambda bi, i: (0, bi * (t // tm) + i, 0))
        u_shape = jax.ShapeDtypeStruct((2 * nch, b * t, FFN_CHUNK), F32)
    return pl.pallas_call(
        functools.partial(_ffn_kernel, tm=tm, nch=nch, prev_rows=prev_rows), grid=(b, t // tm),
        in_specs=[pl.BlockSpec((1, tm, d), lambda bi, i: (bi, i, 0)), prev_spec, pl.BlockSpec(g.shape, lambda bi, i: (0, 0)),
                  pl.BlockSpec(wup.shape, c3), pl.BlockSpec(cw.shape, c3), pl.BlockSpec(cb.shape, c3),
                  pl.BlockSpec(wdn.shape, c3), pl.BlockSpec(gf.shape, lambda bi, i: (0, 0))],
        out_specs=[pl.BlockSpec((1, tm, d), lambda bi, i: (bi, i, 0)), u_spec],
        out_shape=[jax.ShapeDtypeStruct((b, t, d), F32), u_shape],
        compiler_params=_cparams("parallel", "arbitrary"), name="conv_ffn")(x, prev, g, wup, cw, cb, wdn, gf)


def _rope_tables(pos, scale):
    half = QK_ROPE // 2
    inv = 1.0 / (ROPE_THETA ** (np.arange(half, dtype=np.float64) / half))
    ang = np.asarray(pos, np.float64)[:, None] * inv[None, :]
    cos, sin = np.cos(ang), np.sin(ang)
    n = len(pos)
    c = np.zeros((n, HEAD_PAD))
    s = np.zeros((n, HEAD_PAD))
    c[:, :QK_NOPE] = 1.0
    c[:, ROPE_LO:ROPE_LO + half] = cos
    c[:, ROPE_LO + half:ROPE_LO + QK_ROPE] = cos
    s[:, ROPE_LO:ROPE_LO + half] = -sin
    s[:, ROPE_LO + half:ROPE_LO + QK_ROPE] = sin
    return (jnp.asarray(c * scale, F32), jnp.asarray(s * scale, F32), jnp.asarray(c, F32), jnp.asarray(s, F32))


def _swap_halves(w):
    half = w.shape[-1] // 2
    return jnp.concatenate([w[..., half:], w[..., :half]], axis=-1)


def _prep_weights(w_in, w_uq, w_ukv, rw_w2, rw_a2, w_up, conv_w, conv_b, w_down):
    d = w_in.shape[0]
    z = lambda *s: jnp.zeros(s, F32)
    w_kr = w_in[:, Q_LORA + KV_LORA:MLA_PROJ]
    pad_head = lambda w: jnp.concatenate([z(d, ROPE_LO), w, z(d, HEAD_PAD - ROPE_LO - QK_ROPE)], axis=1)
    w1 = jnp.concatenate([w_in[:, :Q_LORA + KV_LORA], pad_head(w_kr), pad_head(_swap_halves(w_kr)), w_in[:, MLA_PROJ:]],
                         axis=1).astype(BF16)
    wq3 = w_uq.reshape(Q_LORA, MLA_HEADS, QK_NOPE + QK_ROPE)
    zq = lambda n: z(Q_LORA, MLA_HEADS, n)
    q_plain = jnp.concatenate([wq3, zq(HEAD_PAD - QK_NOPE - QK_ROPE)], axis=2)
    q_swap = jnp.concatenate([zq(QK_NOPE), _swap_halves(wq3[..., QK_NOPE:]), zq(HEAD_PAD - QK_NOPE - QK_ROPE)], axis=2)
    wq = jnp.concatenate([q_plain.reshape(Q_LORA, -1), q_swap.reshape(Q_LORA, -1)], axis=1).astype(BF16)
    wkv3 = w_ukv.reshape(KV_LORA, MLA_HEADS, QK_NOPE + V_DIM)
    zk = z(KV_LORA, MLA_HEADS, HEAD_PAD - QK_NOPE)
    wk = jnp.concatenate([wkv3[..., :QK_NOPE], zk], axis=2).reshape(KV_LORA, -1).astype(BF16)
    wv = jnp.concatenate([wkv3[..., QK_NOPE:], zk], axis=2).reshape(KV_LORA, -1).astype(BF16)
    w_uk = jnp.transpose(wkv3[..., :QK_NOPE], (1, 2, 0))
    w_uk = jnp.concatenate([w_uk, z(MLA_HEADS, HEAD_PAD - QK_NOPE, KV_LORA)], axis=1).astype(BF16)
    w_uv = jnp.transpose(wkv3[..., QK_NOPE:], (1, 0, 2)).astype(BF16)
    wa2 = jnp.concatenate([jnp.concatenate([rw_w2, z(A_LORA, RW_WIDTH)], axis=0),
                           jnp.concatenate([z(DECAY_LORA, RW_WIDTH), rw_a2], axis=0)], axis=1).astype(BF16)
    f2 = w_up.shape[1]
    nch2 = f2 // FFN_CHUNK
    wup = jnp.transpose(w_up.reshape(d, nch2, FFN_CHUNK), (1, 0, 2)).astype(BF16)
    cw = jnp.transpose(conv_w.reshape(CONV_W, nch2, FFN_CHUNK), (1, 0, 2))
    cw = jnp.concatenate([cw, z(nch2, SUBLANE - CONV_W, FFN_CHUNK)], axis=1)
    cb = conv_b.reshape(nch2, 1, FFN_CHUNK)
    wdn = w_down.reshape(nch2 // 2, FFN_CHUNK, d).astype(BF16)
    return w1, wq, wk, wv, w_uk, w_uv, wa2, wup, cw, cb, wdn


def _state_to_pairs(s):
    b = s.shape[0]
    s = s.reshape(b, PAIRS, 2, RW_HEAD_DIM, RW_HEAD_DIM)
    zz = jnp.zeros_like(s[:, :, 0])
    top = jnp.concatenate([s[:, :, 0], zz], axis=-1)
    bot = jnp.concatenate([zz, s[:, :, 1]], axis=-1)
    return jnp.concatenate([top, bot], axis=-2)


def _pairs_to_state(s):
    b = s.shape[0]
    h0 = s[:, :, :RW_HEAD_DIM, :RW_HEAD_DIM]
    h1 = s[:, :, RW_HEAD_DIM:, RW_HEAD_DIM:]
    return jnp.stack([h0, h1], axis=2).reshape(b, RW_HEADS, RW_HEAD_DIM, RW_HEAD_DIM)


def _pick(n, pref):
    for t in pref:
        if n % t == 0:
            return t
    return n


def kernel(x_prompt, x_sample, cache_mla_latent, cache_mla_krope, cache_mem_k, cache_mem_v, state_rwkv, state_rwkv_shift, state_ffn_conv, page_table, mem_prompt, g_mix, w_in, q_norm_g, kv_norm_g, w_uq, w_ukv, g_mla_out, rw_mu, rw_w0, rw_w2, rw_a0, rw_a2, rw_g2, rw_k_k, rw_k_a, rw_r_k, rw_lnx_g, rw_lnx_b, w_o, g_mem_q, g_mem_kv, w_mq, w_mk, w_mv, w_mo, g_ffn, w_up, conv_w, conv_b, w_down, g_final):
    depth = w_in.shape[0]
    assert depth == 1, "single-layer step"
    bp, tp, d = x_prompt.shape
    bs, ts, _ = x_sample.shape
    npages = page_table.shape[1]
    past_len = npages * PAGE_SIZE
    row2 = lambda a: a.reshape(1, -1)
    l = 0
    w1, wq, wk, wv, w_uk, w_uv, wa2, wup, cw, cb, wdn = _prep_weights(
        w_in[l], w_uq[l], w_ukv[l], rw_w2[l], rw_a2[l], w_up[l], conv_w[l], conv_b[l], w_down[l])
    wo_a, wo_b = w_o[l, :RW_WIDTH].astype(BF16), w_o[l, RW_WIDTH:].astype(BF16)
    wmq, wmo = w_mq[l].astype(BF16), w_mo[l].astype(BF16)
    wmkv = jnp.concatenate([w_mk[l], w_mv[l]], axis=1).astype(BF16)
    g2 = rw_g2[l].astype(BF16)
    nch2 = wup.shape[0]
    rw_vecs = (row2(rw_mu[l]), row2(rw_w0[l]), wa2, row2(rw_a0[l]), g2, row2(rw_k_k[l]), row2(rw_k_a[l]),
               row2(rw_r_k[l]), row2(rw_lnx_g[l]), row2(rw_lnx_b[l]))

    def after_attention(x2, orw, omla, mem_k, mem_v, conv_in, b, t, tm, tm_mem, tm_ffn, prev_rows):
        x1 = _mix(x2, orw, omla, row2(g_mla_out[l]), wo_a, wo_b, tm=tm)
        qm = _norm_mm(x1, row2(g_mem_q[l]), wmq, BF16, tm=tm)
        om = _memattn(qm.reshape(b, t, d), mem_k, mem_v, tm=tm_mem)
        xm = _mm_res(om.reshape(b * t, d), wmo, x1, tm=tm)
        fb, ft = (b, t) if prev_rows is None else (1, b * t)
        return _ffn(xm.reshape(fb, ft, d), conv_in, row2(g_ffn[l]), wup, cw, cb, wdn, row2(g_final), tm=tm_ffn,
                    prev_rows=prev_rows)

    n_p = bp * tp
    tm_p = _pick(tp, (512, 256, 128, 64, 32, 16, 8))
    tabs_p = _rope_tables(np.arange(tp), MLA_SCALE)
    xp2 = x_prompt.reshape(n_p, d)
    q_p, lat_p, kr_p, prw_p, k_p, v_p = _inproj(
        xp2, row2(g_mix[l]), w1, row2(q_norm_g[l]), row2(kv_norm_g[l]), wq, tabs_p, wk, wv, tm=tm_p, with_kv=True)
    tq = _pick(tp, (256, 128))
    omla_p = _flash(q_p, k_p, v_p, batch=bp, seq=tp, tq=tq)
    ch_p = _pick(tp, (64, 32, 16, 8))
    tb_p = _pick(tp, (512, 256, 128, 64, 32, 16, 8))
    zeros_state = jnp.zeros((bp, PAIRS, LANE, LANE), F32)
    orw_p, st_p = _rwkv(prw_p.reshape(bp, tp, RW_PROJ), zeros_state, *rw_vecs, tb=tb_p, ch=ch_p, row_lo=0, row_hi=tb_p)
    mkv = _norm_mm(mem_prompt.reshape(-1, d), row2(g_mem_kv[l]), wmkv, F32, tm=_pick(mem_prompt.shape[0] * mem_prompt.shape[1], (512, 256, 128, 8)))
    n_mem = mem_prompt.shape[1]
    mk_p = mkv[:, :d].reshape(bp, n_mem, d)
    mv_p = mkv[:, d:].reshape(bp, n_mem, d)
    conv0_p = jnp.zeros((bp, nch2, SUBLANE, FFN_CHUNK), F32)
    y_p, u_p = after_attention(xp2, orw_p.reshape(n_p, RW_WIDTH), omla_p, mk_p, mv_p, conv0_p, bp, tp, tm_p, tm_p, tm_p, None)
    conv_p = jnp.transpose(u_p[:, :, SUBLANE - (CONV_W - 1):, :], (0, 2, 1, 3)).reshape(bp, CONV_W - 1, nch2 * FFN_CHUNK)

    n_s = bs * SROWS
    xs3 = jnp.pad(x_sample, ((0, 0), (SLO, SROWS - SLO - ts), (0, 0)))
    pos_s = np.tile(np.concatenate([np.zeros(SLO), past_len + np.arange(ts), np.zeros(SROWS - SLO - ts)]), bs)
    tm_s = _pick(n_s, (1024, 512, 256, 128, 64, 32, 16, 8))
    tabs_s = _rope_tables(pos_s[:tm_s], MLA_SCALE)
    q_s, lat_s, kr_s, prw_s = _inproj(
        xs3.reshape(n_s, d), row2(g_mix[l]), w1, row2(q_norm_g[l]), row2(kv_norm_g[l]), wq, tabs_s, None, None,
        tm=tm_s, with_kv=False)
    qabs = _bmm_cols(q_s, w_uk, BF16)
    qabs = qabs.reshape(MLA_HEADS, bs, SROWS, KV_LORA)[:, :, SLO:SLO + ts]
    qabs = jnp.transpose(qabs, (1, 0, 2, 3)).reshape(bs, MLA_HEADS * ts, KV_LORA)
    qrope = q_s.reshape(bs, SROWS, MLA_HEADS, HEAD_PAD)[:, SLO:SLO + ts, :, ROPE_LO:ROPE_LO + QK_ROPE]
    qrope = jnp.transpose(qrope, (0, 2, 1, 3)).reshape(bs, MLA_HEADS * ts, QK_ROPE)
    olat = _paged(page_table, qabs, qrope, lat_s.reshape(bs, SROWS, KV_LORA), kr_s.reshape(bs, SROWS, QK_ROPE),
                  cache_mla_latent[l], cache_mla_krope[l], n_new=ts)
    olat = jnp.transpose(olat.reshape(bs, MLA_HEADS, ts, KV_LORA), (1, 0, 2, 3))
    olat = jnp.pad(olat, ((0, 0), (0, 0), (SLO, SROWS - SLO - ts), (0, 0))).reshape(MLA_HEADS, n_s, KV_LORA)
    omla_s = _bmm(olat, w_uv, F32)
    omla_s = jnp.transpose(omla_s, (1, 0, 2)).reshape(n_s, MLA_HEADS * V_DIM)
    prw_s3 = prw_s.reshape(bs, SROWS, RW_PROJ).at[:, SLO - 1, :].set(state_rwkv_shift[l])
    orw_s, st_s = _rwkv(prw_s3, _state_to_pairs(state_rwkv[l]), *rw_vecs, tb=SROWS, ch=SROWS, row_lo=SLO, row_hi=SLO + ts)
    hist = jnp.transpose(state_ffn_conv[l].reshape(bs, CONV_W - 1, nch2, FFN_CHUNK), (2, 0, 1, 3))
    hist = jnp.pad(hist, ((0, 0), (0, 0), (SLO - (CONV_W - 1), SROWS - SLO), (0, 0))).reshape(nch2, n_s, FFN_CHUNK)
    mk_s = cache_mem_k[l].reshape(bs, cache_mem_k.shape[2], d)
    mv_s = cache_mem_v[l].reshape(bs, cache_mem_v.shape[2], d)
    y_s, u_s = after_attention(xs3.reshape(n_s, d), orw_s.reshape(n_s, RW_WIDTH), omla_s, mk_s, mv_s, hist,
                               bs, SROWS, tm_s, SROWS, _pick(n_s, (256, 128, 64, 32, 16, 8)), SLO)
    u_s = u_s.reshape(nch2, bs, SROWS, FFN_CHUNK)[:, :, SLO + ts - (CONV_W - 1):SLO + ts]
    conv_s = jnp.transpose(u_s, (1, 2, 0, 3)).reshape(bs, CONV_W - 1, nch2 * FFN_CHUNK)

    real = lambda a, w: a.reshape(bs, SROWS, w)[:, SLO:SLO + ts]
    mem5 = lambda a: a.reshape(1, bp, n_mem, MEM_HEADS, d // MEM_HEADS)
    return (y_p, real(y_s, d),
            lat_p.reshape(1, bp, tp, KV_LORA), kr_p.reshape(1, bp, tp, QK_ROPE), mem5(mk_p), mem5(mv_p),
            _pairs_to_state(st_p)[None], prw_p.reshape(bp, tp, RW_PROJ)[:, -1][None], conv_p[None],
            real(lat_s, KV_LORA)[None], real(kr_s, QK_ROPE)[None], _pairs_to_state(st_s)[None],
            prw_s.reshape(bs, SROWS, RW_PROJ)[:, SLO + ts - 1][None], conv_s[None])
```

```python
import functools

import numpy as np
import jax
import jax.numpy as jnp
from jax import lax
from jax.experimental import pallas as pl
from jax.experimental.pallas import tpu as pltpu

F32 = jnp.float32
BF16 = jnp.bfloat16

RW_HEADS = 8
RW_HEAD_DIM = 64
RW_WIDTH = 512
DECAY_LORA = 64
A_LORA = 64
GATE_LORA = 128
RW_PROJ = 3 * RW_WIDTH + DECAY_LORA + A_LORA + GATE_LORA
LNX_EPS = 64e-5
MLA_HEADS = 8
QK_NOPE = 64
QK_ROPE = 32
V_DIM = 64
Q_LORA = 384
KV_LORA = 256
MLA_PROJ = Q_LORA + KV_LORA + QK_ROPE
MLA_SCALE = (QK_NOPE + QK_ROPE) ** -0.5
ROPE_THETA = 10000.0
MEM_HEADS = 4
CONV_W = 3
NORM_EPS = 1e-6
PAGE_SIZE = 128

LANE = 128
SUBLANE = 8
VMEM_LIMIT = 56 * 1024 * 1024
HEAD_PAD = LANE
ROPE_LO = QK_NOPE
SROWS = 8
SLO = 2
NEG = -1e30
HI = lax.Precision.HIGHEST


def _cparams(*sem):
    return pltpu.CompilerParams(dimension_semantics=sem, vmem_limit_bytes=VMEM_LIMIT)


def _rms(x, g, eps=NORM_EPS):
    return x * lax.rsqrt(jnp.mean(x * x, axis=-1, keepdims=True) + eps) * g


def _dot(a, b):
    return jnp.dot(a, b, preferred_element_type=F32)


def _dot_nt(a, b, precision=None):
    return lax.dot_general(a, b, (((1,), (1,)), ((), ())), preferred_element_type=F32, precision=precision)


def _dot_tn(a, b, precision=None):
    return lax.dot_general(a, b, (((0,), (0,)), ((), ())), preferred_element_type=F32, precision=precision)


def _doth(a, b):
    return jnp.dot(a, b, preferred_element_type=F32, precision=HI)


C_Q0, C_KV0, C_KR0, C_KRS0, C_RW0 = 0, Q_LORA, Q_LORA + KV_LORA, Q_LORA + KV_LORA + LANE, Q_LORA + KV_LORA + 2 * LANE
W1_COLS = C_RW0 + RW_PROJ


def _inproj_kernel(x_ref, g_ref, w1_ref, qg_ref, kvg_ref, wq_ref, cq_ref, sq_ref, ck_ref, sk_ref, *rest, with_kv):
    if with_kv:
        wk_ref, wv_ref, q_out, lat_out, kr_out, prw_out, k_out, v_out = rest
    else:
        q_out, lat_out, kr_out, prw_out = rest
    h = _rms(x_ref[...], g_ref[...]).astype(BF16)
    prw_out[...] = _dot(h, w1_ref[:, C_RW0:W1_COLS])
    pm = _dot(h, w1_ref[:, 0:C_RW0])
    cqn = _rms(pm[:, C_Q0:C_KV0], qg_ref[...]).astype(BF16)
    lat = _rms(pm[:, C_KV0:C_KR0], kvg_ref[...])
    lat_out[...] = lat
    krope = pm[:, C_KR0:C_KRS0] * ck_ref[...] + pm[:, C_KRS0:C_RW0] * sk_ref[...]
    kr_out[...] = krope[:, ROPE_LO:ROPE_LO + QK_ROPE]
    q12 = _dot(cqn, wq_ref[...])
    nq = MLA_HEADS * HEAD_PAD
    cq, sq = cq_ref[...], sq_ref[...]
    for hd in range(MLA_HEADS):
        a, b = hd * HEAD_PAD, (hd + 1) * HEAD_PAD
        q_out[:, a:b] = (q12[:, a:b] * cq + q12[:, nq + a:nq + b] * sq).astype(BF16)
    if with_kv:
        latb = lat.astype(BF16)
        kn = _dot(latb, wk_ref[...])
        for hd in range(MLA_HEADS):
            a, b = hd * HEAD_PAD, (hd + 1) * HEAD_PAD
            k_out[:, a:b] = (kn[:, a:b] + krope).astype(BF16)
        vt = _dot_nt(wv_ref[...], latb).astype(BF16)
        for jb in range(v_out.shape[1]):
            v_out[0, jb] = vt[:, jb * FLASH_T:(jb + 1) * FLASH_T]


def _inproj(x, g, w1, qg, kvg, wq, tabs, wk, wv, *, tm, with_kv, seq=None):
    n, d = x.shape
    ttab = tabs[0].shape[0]
    nt = ttab // tm
    row = lambda i: (i, 0)
    const = lambda i: (0, 0)
    tab = lambda i: (i % nt, 0)
    in_specs = [pl.BlockSpec((tm, d), row), pl.BlockSpec(g.shape, const), pl.BlockSpec(w1.shape, const),
                pl.BlockSpec(qg.shape, const), pl.BlockSpec(kvg.shape, const), pl.BlockSpec(wq.shape, const)]
    in_specs += [pl.BlockSpec((tm, LANE), tab)] * 4
    args = [x, g, w1, qg, kvg, wq, *tabs]
    nq = MLA_HEADS * HEAD_PAD
    out_shape = [jax.ShapeDtypeStruct((n, nq), BF16), jax.ShapeDtypeStruct((n, KV_LORA), F32),
                 jax.ShapeDtypeStruct((n, QK_ROPE), F32), jax.ShapeDtypeStruct((n, RW_PROJ), F32)]
    out_specs = [pl.BlockSpec((tm, nq), row), pl.BlockSpec((tm, KV_LORA), row),
                 pl.BlockSpec((tm, QK_ROPE), row), pl.BlockSpec((tm, RW_PROJ), row)]
    if with_kv:
        in_specs += [pl.BlockSpec(wk.shape, const), pl.BlockSpec(wv.shape, const)]
        args += [wk, wv]
        nv = MLA_HEADS * V_DIM
        npb = seq // tm
        out_shape += [jax.ShapeDtypeStruct((n, nq), BF16), jax.ShapeDtypeStruct((n // seq, seq // FLASH_T, nv, FLASH_T), BF16)]
        out_specs += [pl.BlockSpec((tm, nq), row),
                      pl.BlockSpec((1, tm // FLASH_T, nv, FLASH_T), lambda i: (i // npb, i % npb, 0, 0))]
    return pl.pallas_call(
        functools.partial(_inproj_kernel, with_kv=with_kv), grid=(n // tm,), in_specs=in_specs,
        out_specs=out_specs, out_shape=out_shape, compiler_params=_cparams("parallel"),
        name="inproj_kv" if with_kv else "inproj")(*args)


FLASH_HPB = 4
FLASH_T = 256


def _flash_kernel(q_ref, k_ref, vt_ref, o_ref, *, tq):
    tk = FLASH_T
    qi = pl.program_id(2)
    nfull = qi * (tq // tk)
    qs = [q_ref[:, h * HEAD_PAD:(h + 1) * HEAD_PAD] for h in range(FLASH_HPB)]
    kidx = lax.broadcasted_iota(jnp.int32, (tk, tq), 0)
    qidx = lax.broadcasted_iota(jnp.int32, (tk, tq), 1)

    def scores(j):
        start = pl.multiple_of(j * tk, tk)
        return tuple(_dot_nt(k_ref[pl.ds(start, tk), h * HEAD_PAD:(h + 1) * HEAD_PAD], qs[h]) for h in range(FLASH_HPB))

    def step(j, sts, carry, diag_off):
        out = []
        for h in range(FLASH_HPB):
            m, l, acc = carry[h]
            st = sts[h]
            if diag_off is not None:
                st = jnp.where(kidx + diag_off <= qidx, st, NEG)
            m_new = jnp.maximum(m, jnp.max(st, axis=0, keepdims=True))
            alpha = jnp.exp(m - m_new)
            p = jnp.exp(st - m_new)
            l = alpha * l + jnp.sum(p, axis=0, keepdims=True)
            acc = alpha * acc + _dot(vt_ref[0, j, h * V_DIM:(h + 1) * V_DIM, :], p.astype(BF16))
            out.append((m_new, l, acc))
        return tuple(out)

    init = tuple((jnp.full((1, tq), NEG, F32), jnp.zeros((1, tq), F32), jnp.zeros((V_DIM, tq), F32))
                 for _ in range(FLASH_HPB))

    def body(j, c):
        nxt = scores(j + 1)
        return nxt, step(j, c[0], c[1], None)

    sts, carry = lax.fori_loop(0, nfull, body, (scores(0), init))
    ndiag = tq // tk
    for d in range(ndiag):
        nxt = scores(nfull + d + 1) if d + 1 < ndiag else None
        carry = step(nfull + d, sts, carry, d * tk)
        sts = nxt
    ot = jnp.concatenate([acc / l for _, l, acc in carry], axis=0)
    o_ref[...] = ot.T


def _flash(q, k, vt, *, batch, seq, tq):
    nqb = seq // tq
    grid = (batch, MLA_HEADS // FLASH_HPB, nqb)
    w = FLASH_HPB * HEAD_PAD
    return pl.pallas_call(
        functools.partial(_flash_kernel, tq=tq), grid=grid,
        in_specs=[pl.BlockSpec((tq, w), lambda b, h, i: (b * nqb + i, h)),
                  pl.BlockSpec((seq, w), lambda b, h, i: (b, h)),
                  pl.BlockSpec((1, seq // FLASH_T, FLASH_HPB * V_DIM, FLASH_T), lambda b, h, i: (b, 0, h, 0))],
        out_specs=pl.BlockSpec((tq, FLASH_HPB * V_DIM), lambda b, h, i: (b * nqb + i, h)),
        out_shape=jax.ShapeDtypeStruct((batch * seq, MLA_HEADS * V_DIM), F32),
        compiler_params=_cparams("parallel", "parallel", "arbitrary"), name="mla_flash")(q, k, vt)


PAGES_PER_STEP = 16
PAGE_GROUP = 4


def _paged_kernel(pt_ref, qa_ref, qr_ref, latn_ref, krn_ref, *rest, n_new):
    del pt_ref
    g = PAGES_PER_STEP
    lat_refs, kr_refs = rest[:g], rest[g:2 * g]
    o_ref, m_s, l_s, acc_s = rest[2 * g:]
    step = pl.program_id(1)

    @pl.when(step == 0)
    def _():
        m_s[...] = jnp.full(m_s.shape, NEG, F32)
        l_s[...] = jnp.zeros(l_s.shape, F32)
        acc_s[...] = jnp.zeros(acc_s.shape, F32)

    qa = qa_ref[0]
    qr = qr_ref[0]

    def update(carry, s, vb):
        m, l, acc = carry
        m_new = jnp.maximum(m, jnp.max(s, axis=-1, keepdims=True))
        alpha = jnp.exp(m - m_new)
        p = jnp.exp(s - m_new)
        l = alpha * l + jnp.sum(p, axis=-1, keepdims=True)
        acc = alpha * acc + _dot(p.astype(BF16), vb)
        return m_new, l, acc

    def scores(grp):
        pages = range(grp * PAGE_GROUP, (grp + 1) * PAGE_GROUP)
        latb = jnp.concatenate([lat_refs[j][0].astype(BF16) for j in pages], axis=0)
        krt = jnp.concatenate([kr_refs[j][0].astype(BF16) for j in pages], axis=1)
        return _dot_nt(qa, latb) + _dot(qr, krt), latb

    carry = (m_s[...], l_s[...], acc_s[...])
    ngrp = g // PAGE_GROUP
    cur = scores(0)
    for grp in range(ngrp):
        nxt = scores(grp + 1) if grp + 1 < ngrp else None
        carry = update(carry, *cur)
        cur = nxt
    m_s[...], l_s[...], acc_s[...] = carry

    @pl.when(step == pl.num_programs(1) - 1)
    def _():
        latn = latn_ref[0].astype(BF16)
        krn = krn_ref[0].astype(BF16)
        s = _dot_nt(qa, latn) + _dot_nt(qr, krn)
        qt = lax.broadcasted_iota(jnp.int32, s.shape, 0) % n_new
        kt = lax.broadcasted_iota(jnp.int32, s.shape, 1) - SLO
        s = jnp.where((kt >= 0) & (kt <= qt), s, NEG)
        _, l, acc = update((m_s[...], l_s[...], acc_s[...]), s, latn)
        o_ref[0] = acc / l


def _paged(page_table, qa, qr, lat_new, kr_new, cache_lat, cache_kr, *, n_new):
    b, npages = page_table.shape
    g = PAGES_PER_STEP
    rows = qa.shape[1]
    lat_specs = [pl.BlockSpec((1, PAGE_SIZE, KV_LORA), functools.partial(lambda i, s, pt, j: (pt[i, s * g + j], 0, 0), j=j))
                 for j in range(g)]
    kr_specs = [pl.BlockSpec((1, QK_ROPE, PAGE_SIZE), functools.partial(lambda i, s, pt, j: (pt[i, s * g + j], 0, 0), j=j))
                for j in range(g)]
    seq3 = lambda i, s, pt: (i, 0, 0)
    grid_spec = pltpu.PrefetchScalarGridSpec(
        num_scalar_prefetch=1, grid=(b, npages // g),
        in_specs=[pl.BlockSpec((1, rows, KV_LORA), seq3), pl.BlockSpec((1, rows, QK_ROPE), seq3),
                  pl.BlockSpec((1, SROWS, KV_LORA), seq3), pl.BlockSpec((1, SROWS, QK_ROPE), seq3)] + lat_specs + kr_specs,
        out_specs=pl.BlockSpec((1, rows, KV_LORA), seq3),
        scratch_shapes=[pltpu.VMEM((rows, 1), F32), pltpu.VMEM((rows, 1), F32), pltpu.VMEM((rows, KV_LORA), F32)])
    return pl.pallas_call(
        functools.partial(_paged_kernel, n_new=n_new), grid_spec=grid_spec,
        out_shape=jax.ShapeDtypeStruct((b, rows, KV_LORA), F32),
        compiler_params=_cparams("parallel", "arbitrary"), name="mla_paged")(
            page_table, qa, qr, lat_new, kr_new, *([cache_lat] * g), *([cache_kr] * g))


def _bmm_kernel(a_ref, w_ref, o_ref):
    o_ref[0] = _dot(a_ref[...].astype(BF16), w_ref[0]).astype(o_ref.dtype)


def _bmm_cols(a, w, out_dtype):
    n = a.shape[0]
    hh, k, m = w.shape
    return pl.pallas_call(
        _bmm_kernel, grid=(hh,),
        in_specs=[pl.BlockSpec((n, k), lambda h: (0, h)), pl.BlockSpec((1, k, m), lambda h: (h, 0, 0))],
        out_specs=pl.BlockSpec((1, n, m), lambda h: (h, 0, 0)),
        out_shape=jax.ShapeDtypeStruct((hh, n, m), out_dtype), compiler_params=_cparams("parallel"),
        name="bmm_cols")(a, w)


def _bmm_kernel3(a_ref, w_ref, o_ref):
    o_ref[0] = _dot(a_ref[0].astype(BF16), w_ref[0]).astype(o_ref.dtype)


def _bmm(a, w, out_dtype):
    hh, n, k = a.shape
    m = w.shape[2]
    return pl.pallas_call(
        _bmm_kernel3, grid=(hh,),
        in_specs=[pl.BlockSpec((1, n, k), lambda h: (h, 0, 0)), pl.BlockSpec((1, k, m), lambda h: (h, 0, 0))],
        out_specs=pl.BlockSpec((1, n, m), lambda h: (h, 0, 0)),
        out_shape=jax.ShapeDtypeStruct((hh, n, m), out_dtype), compiler_params=_cparams("parallel"),
        name="bmm")(a, w)


PAIRS = RW_HEADS // 2
C_R, C_K, C_V, C_WA, C_G = 0, RW_WIDTH, 2 * RW_WIDTH, 3 * RW_WIDTH, 3 * RW_WIDTH + DECAY_LORA + A_LORA


def _seg_sum(x):
    lane = lax.broadcasted_iota(jnp.int32, (x.shape[0], LANE), 1)
    low = lane < RW_HEAD_DIM
    outs = []
    for t in range(x.shape[1] // LANE):
        xt = x[:, t * LANE:(t + 1) * LANE]
        s0 = jnp.sum(jnp.where(low, xt, 0.0), axis=-1, keepdims=True)
        s1 = jnp.sum(jnp.where(low, 0.0, xt), axis=-1, keepdims=True)
        outs.append(jnp.where(low, s0, s1))
    return outs[0] if len(outs) == 1 else jnp.concatenate(outs, axis=-1)


def _rwkv_prologue(p_ref, s0_ref, so_ref, vec_refs, scratch, *, tb, row_lo, row_hi):
    mu_ref, w0_ref, wa2_ref, a0_ref, g2_ref, kk_ref, ka_ref, rk_ref = vec_refs
    carry_s, r_s, k_s, v_s, kk_s, b_s, ld_s = scratch
    i = pl.program_id(1)

    @pl.when(i == 0)
    def _():
        carry_s[...] = jnp.zeros(carry_s.shape, F32)
        so_ref[0] = s0_ref[0]

    p = p_ref[0]
    rows1 = lax.broadcasted_iota(jnp.int32, (tb, 1), 0)
    prev = jnp.where(rows1 == 0, carry_s[SUBLANE - 1:SUBLANE, :], pltpu.roll(p, 1, axis=0))
    carry_s[...] = p[tb - SUBLANE:tb, :]
    s = p + (prev - p) * mu_ref[...]
    r = s[:, C_R:C_K]
    k = s[:, C_K:C_V]
    v = s[:, C_V:C_WA]
    wa = s[:, C_WA:C_G]
    lane = lax.broadcasted_iota(jnp.int32, wa.shape, 1)
    z = jnp.where(lane < DECAY_LORA, jnp.tanh(wa), wa).astype(BF16)
    lin = _dot(z, wa2_ref[...])
    xw = -(w0_ref[...] + lin[:, :RW_WIDTH])
    w = -(jnp.maximum(xw, 0.0) + jnp.log(1.0 + jnp.exp(-jnp.abs(xw)))) - 0.5
    logd = -jnp.exp(w)
    a = jax.nn.sigmoid(a0_ref[...] + lin[:, RW_WIDTH:])
    gate = _dot(jax.nn.sigmoid(s[:, C_G:RW_PROJ]).astype(BF16), g2_ref[...])
    kk = k * kk_ref[...]
    kk = kk / jnp.maximum(jnp.sqrt(_seg_sum(kk * kk)), 1e-12)
    k = k * (1.0 + (a - 1.0) * ka_ref[...])
    bonus = _seg_sum(r * k * rk_ref[...]) * v
    if row_lo > 0 or row_hi < tb:
        live = (rows1 >= row_lo) & (rows1 < row_hi)
        logd = jnp.where(live, logd, 0.0)
        kk = jnp.where(live, kk, 0.0)
        k = jnp.where(live, k, 0.0)
        v = jnp.where(live, v, 0.0)
    r_s[...] = r
    k_s[...] = k
    v_s[...] = v
    kk_s[...] = kk
    b_s[...] = kk * a
    ld_s[...] = logd
    return gate, bonus


def _rwkv_epilogue(out, gate, bonus, lg_ref, lb_ref, o_ref):
    mean = _seg_sum(out) * (1.0 / RW_HEAD_DIM)
    cen = out - mean
    var = _seg_sum(cen * cen) * (1.0 / RW_HEAD_DIM)
    y = cen * lax.rsqrt(var + LNX_EPS) * lg_ref[...] + lb_ref[...]
    o_ref[0] = ((y + bonus) * gate).astype(o_ref.dtype)


def _rwkv_kernel(p_ref, s0_ref, mu_ref, w0_ref, wa2_ref, a0_ref, g2_ref, kk_ref, ka_ref, rk_ref, lg_ref, lb_ref,
                 o_ref, so_ref, carry_s, r_s, k_s, v_s, kk_s, b_s, ld_s, out_s, *, tb, ch, row_lo, row_hi):
    gate, bonus = _rwkv_prologue(p_ref, s0_ref, so_ref, (mu_ref, w0_ref, wa2_ref, a0_ref, g2_ref, kk_ref, ka_ref, rk_ref),
                                 (carry_s, r_s, k_s, v_s, kk_s, b_s, ld_s), tb=tb, row_lo=row_lo, row_hi=row_hi)
    c2 = 2 * ch
    ri = lax.broadcasted_iota(jnp.int32, (c2, c2), 0)
    ci = lax.broadcasted_iota(jnp.int32, (c2, c2), 1)
    strict = (ci % ch) < (ri % ch)
    incl = (ci % ch) <= (ri % ch)
    eye = (ri == ci).astype(F32)
    tri = (lax.broadcasted_iota(jnp.int32, (ch, ch), 1) <= lax.broadcasted_iota(jnp.int32, (ch, ch), 0)).astype(F32)
    lane2 = lax.broadcasted_iota(jnp.int32, (c2, LANE), 1)
    row2 = lax.broadcasted_iota(jnp.int32, (c2, LANE), 0)
    own = (lane2 < RW_HEAD_DIM) == (row2 < ch)

    def stack(x):
        return jnp.where(own, jnp.concatenate([x, x], axis=0), 0.0)

    bf = lambda x: x.astype(BF16)
    pairs = [slice(pr * LANE, (pr + 1) * LANE) for pr in range(PAIRS)]

    def chunk(c, _):
        ts = pl.ds(pl.multiple_of(c * ch, ch), ch)
        css = [_doth(tri, ld_s[ts, ls]) for ls in pairs]
        lhss, amats, tails = [], [], []
        for ls, cs in zip(pairs, css):
            ld = ld_s[ts, ls]
            e_inv = jnp.exp(-cs)
            lhs = bf(jnp.concatenate([stack(kk_s[ts, ls] * jnp.exp(cs - ld)), stack(r_s[ts, ls] * jnp.exp(cs))], axis=0))
            rhs = bf(jnp.concatenate([stack(k_s[ts, ls] * e_inv), stack(b_s[ts, ls] * e_inv)], axis=0))
            lhss.append(lhs)
            amats.append(_dot_nt(lhs, rhs))
        sts = [so_ref[0, pr] for pr in range(PAIRS)]
        gmats = [_dot_nt(lhs, bf(st)) for lhs, st in zip(lhss, sts)]
        vss = [bf(stack(v_s[ts, ls])) for ls in pairs]
        akbs = [jnp.where(strict, amat[:c2, c2:], 0.0) for amat in amats]
        ykk = [_dot(bf(jnp.where(strict, amat[:c2, :c2], 0.0)), vs) for amat, vs in zip(amats, vss)]
        yrk = [_dot(bf(jnp.where(incl, amat[c2:, :c2], 0.0)), vs) for amat, vs in zip(amats, vss)]
        invs = [eye - jnp.where((ri // 2 == ci // 2), a_kb, 0.0) for a_kb in akbs]
        m = 2
        while m < ch:
            lvl = (ri // (2 * m) == ci // (2 * m)) & ((ri // m) % 2 == 1) & ((ci // m) % 2 == 0)
            tmps = [_dot(bf(inv), bf(jnp.where(lvl, a_kb, 0.0))) for inv, a_kb in zip(invs, akbs)]
            invs = [inv - _dot(bf(tmp), bf(inv)) for inv, tmp in zip(invs, tmps)]
            m *= 2
        us = [_dot(bf(inv), bf(gmat[:c2] + y)) for inv, gmat, y in zip(invs, gmats, ykk)]
        o2s = [gmat[c2:] + y - _dot(bf(jnp.where(incl, amat[c2:, c2:], 0.0)), bf(u))
               for gmat, y, amat, u in zip(gmats, yrk, amats, us)]
        for pr, (ls, cs, st, vs, u, o2) in enumerate(zip(pairs, css, sts, vss, us, o2s)):
            tot = cs[ch - 1:ch, :]
            e_tail = jnp.exp(tot - cs)
            out_s[ts, ls] = o2[:ch] + o2[ch:]
            so_ref[0, pr] = (st * jnp.exp(tot) + _dot_tn(vs, bf(stack(k_s[ts, ls] * e_tail)))
                             - _dot_tn(bf(u), bf(stack(b_s[ts, ls] * e_tail))))
        return 0

    lax.fori_loop(0, tb // ch, chunk, 0)
    _rwkv_epilogue(out_s[...], gate, bonus, lg_ref, lb_ref, o_ref)


def _rwkv(p, s0, mu, w0, wa2, a0, g2, k_k, k_a, r_k, lnx_g, lnx_b, *, tb, ch, row_lo, row_hi):
    b, t, _ = p.shape
    const = lambda bi, i: (0, 0)
    vec = lambda a: pl.BlockSpec(a.shape, const)
    scr = lambda: pltpu.VMEM((tb, RW_WIDTH), F32)
    return pl.pallas_call(
        functools.partial(_rwkv_kernel, tb=tb, ch=ch, row_lo=row_lo, row_hi=row_hi), grid=(b, t // tb),
        in_specs=[pl.BlockSpec((1, tb, RW_PROJ), lambda bi, i: (bi, i, 0)),
                  pl.BlockSpec((1, PAIRS, LANE, LANE), lambda bi, i: (bi, 0, 0, 0)),
                  vec(mu), vec(w0), vec(wa2), vec(a0), vec(g2), vec(k_k), vec(k_a), vec(r_k), vec(lnx_g), vec(lnx_b)],
        out_specs=[pl.BlockSpec((1, tb, RW_WIDTH), lambda bi, i: (bi, i, 0)),
                   pl.BlockSpec((1, PAIRS, LANE, LANE), lambda bi, i: (bi, 0, 0, 0))],
        out_shape=[jax.ShapeDtypeStruct((b, t, RW_WIDTH), BF16), jax.ShapeDtypeStruct((b, PAIRS, LANE, LANE), F32)],
        scratch_shapes=[pltpu.VMEM((SUBLANE, RW_PROJ), F32)] + [scr() for _ in range(7)],
        compiler_params=_cparams("parallel", "arbitrary"), name="rwkv")(
            p, s0, mu, w0, wa2, a0, g2, k_k, k_a, r_k, lnx_g, lnx_b)


QUAD = 4
QCH = RW_HEAD_DIM
QW = QUAD * RW_HEAD_DIM
QLEVELS = (2, 4, 8, 16, 32)
PREP_CHUNKS = 4


def _tile4(x):
    return jnp.concatenate([x] * QUAD, axis=0)


def _rwkv_quad_kernel(p_ref, s0_ref, mu_ref, w0_ref, wa2_ref, a0_ref, g2_ref, kk_ref, ka_ref, rk_ref, lg_ref, lb_ref,
                      tri_ref, bdm_ref, own_ref, sbm_ref,
                      o_ref, so_ref, carry_s, r_s, k_s, v_s, kk_s, b_s, ld_s, out_s,
                      lhs_c, arb_c, y_c, kt_c, dec_c, *, tb):
    gate, bonus = _rwkv_prologue(p_ref, s0_ref, so_ref, (mu_ref, w0_ref, wa2_ref, a0_ref, g2_ref, kk_ref, ka_ref, rk_ref),
                                 (carry_s, r_s, k_s, v_s, kk_s, b_s, ld_s), tb=tb, row_lo=0, row_hi=tb)
    ch = QCH
    nquad = RW_HEADS // QUAD
    bf = lambda x: x.astype(BF16)

    def bd(x):
        return _tile4(bf(x)) * bdm_ref[0]

    quads = [slice(q * QW, (q + 1) * QW) for q in range(nquad)]

    def prepare(chunks):
        inst = [(c, q, pl.ds(pl.multiple_of(c * ch, ch), ch), quads[q]) for c in chunks for q in range(nquad)]
        strict, incl, eye, lvl1 = sbm_ref[0] > 0, sbm_ref[1] > 0, sbm_ref[2], sbm_ref[3]
        css = []
        for c, q, ts, ls in inst:
            ld = ld_s[ts, ls]
            l1 = bf(ld)
            r1 = ld - l1.astype(F32)
            l2 = bf(r1)
            l3 = bf(r1 - l2.astype(F32))
            css.append(_dot(tri_ref[...], jnp.concatenate([l1, l2, l3], axis=0)))
        a1s, a2s = [], []
        for (c, q, ts, ls), cs in zip(inst, css):
            ld = ld_s[ts, ls]
            tot = cs[ch - 1:ch, :]
            e_inv = jnp.exp(-cs)
            kc, bc = k_s[ts, ls], b_s[ts, ls]
            lhs = bf(jnp.concatenate([kk_s[ts, ls] * jnp.exp(cs - ld), r_s[ts, ls] * jnp.exp(cs)], axis=0))
            lhs_c[c, q] = lhs
            e_tail = jnp.exp(tot - cs)
            kt_c[c, q] = bf(jnp.concatenate([kc * e_tail, bc * e_tail], axis=0))
            dec_c[c, q] = jnp.broadcast_to(jnp.exp(tot), (SUBLANE, QW))
            a1s.append(_dot_nt(lhs, bd(kc * e_inv)))
            a2s.append(_dot_nt(lhs, bd(bc * e_inv)))
        ys, akb4s, invs = [], [], []
        for (c, q, ts, ls), a1, a2 in zip(inst, a1s, a2s):
            a_kk = jnp.where(strict, a1[:ch], 0.0)
            a_rk = jnp.where(incl, a1[ch:], 0.0)
            a_kb = jnp.where(strict, a2[:ch], 0.0)
            arb_c[c, q] = bf(jnp.where(incl, a2[ch:], 0.0))
            ys.append(_dot(bf(jnp.concatenate([a_kk, a_rk], axis=0)), bd(v_s[ts, ls])))
            akb4s.append(_tile4(bf(a_kb)))
            invs.append(eye - a_kb * lvl1)
        for li in range(len(QLEVELS)):
            tmps = [_dot(bf(inv), akb4 * bdm_ref[1 + li]) for inv, akb4 in zip(invs, akb4s)]
            invs = [inv - _dot(bf(tmp), bd(inv)) for inv, tmp in zip(invs, tmps)]
        ws = [_dot(bf(inv), _tile4(lhs_c[c, q][:ch]) * bdm_ref[0]) for (c, q, ts, ls), inv in zip(inst, invs)]
        yks = [_dot(bf(inv), bd(y[:ch])) for inv, y in zip(invs, ys)]
        for (c, q, ts, ls), w, yk, y in zip(inst, ws, yks, ys):
            lhs_c[c, q, 0:ch, :] = bf(w)
            y_c[c, q] = jnp.concatenate([yk, y[ch:]], axis=0)

    nc = tb // ch
    if nc % PREP_CHUNKS == 0:
        def prepare_loop(cc, _):
            prepare([PREP_CHUNKS * cc + d for d in range(PREP_CHUNKS)])
            return 0
        lax.fori_loop(0, nc // PREP_CHUNKS, prepare_loop, 0)
    else:
        lax.fori_loop(0, nc, lambda c, _: (prepare([c]), 0)[1], 0)

    def recur(c, _):
        ts = pl.ds(pl.multiple_of(c * ch, ch), ch)
        sts = [so_ref[0, q] for q in range(nquad)]
        gys = [_dot_nt(lhs_c[c, q], bf(sts[q])) + y_c[c, q] for q in range(nquad)]
        us = [gys[q][:ch] for q in range(nquad)]
        outs = [gys[q][ch:] - _dot(arb_c[c, q], bd(us[q])) for q in range(nquad)]
        upds = [_dot_tn(bf(jnp.concatenate([v_s[ts, quads[q]], -us[q]], axis=0)), kt_c[c, q]) for q in range(nquad)]
        for q in range(nquad):
            out_s[ts, quads[q]] = outs[q]
            so_ref[0, q] = sts[q] * dec_c[c, q][0:1, :] + upds[q] * own_ref[...]
        return 0

    lax.fori_loop(0, nc, recur, 0)
    _rwkv_epilogue(out_s[...], gate, bonus, lg_ref, lb_ref, o_ref)


def _quad_masks():
    ch = QCH
    r = np.arange(QUAD * ch)[:, None]
    c = np.arange(QW)[None, :]
    own = (r // ch) == (c // RW_HEAD_DIM)
    sp, s = r % ch, c % ch
    bdm = [own]
    for m in QLEVELS:
        bdm.append(own & (sp // (2 * m) == s // (2 * m)) & ((sp // m) % 2 == 1) & ((s // m) % 2 == 0))
    t = np.arange(ch)[:, None]
    sbm = [s < t, s <= t, s == t, (t // 2 == s // 2) & (t % 2 == 1) & (s % 2 == 0)]
    tri = np.tile(np.arange(ch)[None, :] <= np.arange(ch)[:, None], (1, 3))
    return (jnp.asarray(tri, BF16), jnp.asarray(np.stack(bdm), BF16), jnp.asarray(own, F32),
            jnp.asarray(np.stack([np.broadcast_to(m, (ch, QW)) for m in sbm]), F32))


def _rwkv_quad(p, s0, mu, w0, wa2, a0, g2, k_k, k_a, r_k, lnx_g, lnx_b, *, tb):
    b, t, _ = p.shape
    nq = RW_HEADS // QUAD
    nc = tb // QCH
    masks = _quad_masks()
    vec = lambda a: pl.BlockSpec(a.shape, lambda bi, i: (0,) * a.ndim)
    scr = lambda: pltpu.VMEM((tb, RW_WIDTH), F32)
    vecs = (mu, w0, wa2, a0, g2, k_k, k_a, r_k, lnx_g, lnx_b) + masks
    return pl.pallas_call(
        functools.partial(_rwkv_quad_kernel, tb=tb), grid=(b, t // tb),
        in_specs=[pl.BlockSpec((1, tb, RW_PROJ), lambda bi, i: (bi, i, 0)),
                  pl.BlockSpec((1, nq, QW, QW), lambda bi, i: (bi, 0, 0, 0))] + [vec(a) for a in vecs],
        out_specs=[pl.BlockSpec((1, tb, RW_WIDTH), lambda bi, i: (bi, i, 0)),
                   pl.BlockSpec((1, nq, QW, QW), lambda bi, i: (bi, 0, 0, 0))],
        out_shape=[jax.ShapeDtypeStruct((b, t, RW_WIDTH), BF16), jax.ShapeDtypeStruct((b, nq, QW, QW), F32)],
        scratch_shapes=[pltpu.VMEM((SUBLANE, RW_PROJ), F32)] + [scr() for _ in range(7)] + [
            pltpu.VMEM((nc, nq, 2 * QCH, QW), BF16),
            pltpu.VMEM((nc, nq, QCH, QW), BF16), pltpu.VMEM((nc, nq, 2 * QCH, QW), F32),
            pltpu.VMEM((nc, nq, 2 * QCH, QW), BF16), pltpu.VMEM((nc, nq, SUBLANE, QW), F32)],
        compiler_params=_cparams("parallel", "arbitrary"), name="rwkv_quad")(p, s0, *vecs)


def _quads_to_state(sq):
    b = sq.shape[0]
    s6 = sq.reshape(b, RW_HEADS // QUAD, QUAD, RW_HEAD_DIM, QUAD, RW_HEAD_DIM)
    return jnp.stack([s6[:, :, h, :, h, :] for h in range(QUAD)], axis=2).reshape(b, RW_HEADS, RW_HEAD_DIM, RW_HEAD_DIM)


def _mix_kernel(x_ref, orw_ref, omla_ref, g_ref, wa_ref, wb_ref, o_ref):
    om = _rms(omla_ref[...], g_ref[...]).astype(BF16)
    o_ref[...] = x_ref[...] + _dot(orw_ref[...], wa_ref[...]) + _dot(om, wb_ref[...])


def _mix(x, orw, omla, g, wa, wb, *, tm):
    n, d = x.shape
    row = lambda i: (i, 0)
    const = lambda i: (0, 0)
    return pl.pallas_call(
        _mix_kernel, grid=(n // tm,),
        in_specs=[pl.BlockSpec((tm, d), row), pl.BlockSpec((tm, orw.shape[1]), row), pl.BlockSpec((tm, omla.shape[1]), row),
                  pl.BlockSpec(g.shape, const), pl.BlockSpec(wa.shape, const), pl.BlockSpec(wb.shape, const)],
        out_specs=pl.BlockSpec((tm, d), row), out_shape=jax.ShapeDtypeStruct((n, d), F32),
        compiler_params=_cparams("parallel"), name="mix_out")(x, orw, omla, g, wa, wb)


def _norm_mm_kernel(x_ref, g_ref, w_ref, o_ref):
    o_ref[...] = _dot(_rms(x_ref[...], g_ref[...]).astype(BF16), w_ref[...]).astype(o_ref.dtype)


def _norm_mm(x, g, w, out_dtype, *, tm):
    n, d = x.shape
    m = w.shape[1]
    return pl.pallas_call(
        _norm_mm_kernel, grid=(n // tm,),
        in_specs=[pl.BlockSpec((tm, d), lambda i: (i, 0)), pl.BlockSpec(g.shape, lambda i: (0, 0)),
                  pl.BlockSpec(w.shape, lambda i: (0, 0))],
        out_specs=pl.BlockSpec((tm, m), lambda i: (i, 0)), out_shape=jax.ShapeDtypeStruct((n, m), out_dtype),
        compiler_params=_cparams("parallel"), name="norm_mm")(x, g, w)


def _mm_res_kernel(a_ref, w_ref, x_ref, o_ref):
    o_ref[...] = x_ref[...] + _dot(a_ref[...], w_ref[...])


def _mm_res(a, w, x, *, tm):
    n, d = x.shape
    return pl.pallas_call(
        _mm_res_kernel, grid=(n // tm,),
        in_specs=[pl.BlockSpec((tm, a.shape[1]), lambda i: (i, 0)), pl.BlockSpec(w.shape, lambda i: (0, 0)),
                  pl.BlockSpec((tm, d), lambda i: (i, 0))],
        out_specs=pl.BlockSpec((tm, d), lambda i: (i, 0)), out_shape=jax.ShapeDtypeStruct((n, d), F32),
        compiler_params=_cparams("parallel"), name="mm_res")(a, w, x)


def _memattn_kernel(q_ref, k_ref, v_ref, o_ref):
    dh = q_ref.shape[2] // MEM_HEADS
    scale = dh ** -0.5
    for h in range(MEM_HEADS):
        cs = slice(h * dh, (h + 1) * dh)
        s = _dot_nt(q_ref[0, :, cs], k_ref[0, :, cs].astype(BF16)) * scale
        p = jnp.exp(s - jnp.max(s, axis=-1, keepdims=True))
        l = jnp.sum(p, axis=-1, keepdims=True)
        o_ref[0, :, cs] = (_dot(p.astype(BF16), v_ref[0, :, cs].astype(BF16)) / l).astype(o_ref.dtype)


def _memattn(q, mk, mv, *, tm):
    b, t, d = q.shape
    nm = mk.shape[1]
    return pl.pallas_call(
        _memattn_kernel, grid=(b, t // tm),
        in_specs=[pl.BlockSpec((1, tm, d), lambda bi, i: (bi, i, 0)), pl.BlockSpec((1, nm, d), lambda bi, i: (bi, 0, 0)),
                  pl.BlockSpec((1, nm, d), lambda bi, i: (bi, 0, 0))],
        out_specs=pl.BlockSpec((1, tm, d), lambda bi, i: (bi, i, 0)), out_shape=jax.ShapeDtypeStruct((b, t, d), BF16),
        compiler_params=_cparams("parallel", "parallel"), name="mem_attn")(q, mk, mv)


FFN_CHUNK = 256


def _ffn_kernel(x_ref, prev_ref, g_ref, wup_ref, cw_ref, cb_ref, wdn_ref, gf_ref, y_ref, u_ref, *, tm, nch, prev_rows):
    i = pl.program_id(1)
    x = x_ref[0]
    h = _rms(x, g_ref[...]).astype(BF16)
    rows = lax.broadcasted_iota(jnp.int32, (tm, 1), 0)
    if prev_rows is None:
        @pl.when(i == 0)
        def _():
            u_ref[0] = prev_ref[0]
    else:
        hist = (rows % SROWS) < prev_rows

    def conv(c):
        u = _dot(h, wup_ref[c])
        if prev_rows is None:
            tail = u_ref[0, c]
            u_ref[0, c] = u[tm - SUBLANE:tm, :]
            p1, p2 = tail[SUBLANE - 1:SUBLANE, :], tail[SUBLANE - 2:SUBLANE - 1, :]
            u1 = jnp.where(rows == 0, p1, pltpu.roll(u, 1, axis=0))
            u2 = jnp.where(rows == 0, p2, jnp.where(rows == 1, p1, pltpu.roll(u, 2, axis=0)))
        else:
            u = jnp.where(hist, prev_ref[c], u)
            u_ref[c] = u
            u1 = pltpu.roll(u, 1, axis=0)
            u2 = pltpu.roll(u, 2, axis=0)
        cw = cw_ref[c]
        return cb_ref[c] + cw[0:1, :] * u2 + cw[1:2, :] * u1 + cw[2:3, :] * u

    def body(c, acc):
        gate = conv(c)
        val = conv(c + nch)
        act = (gate * jax.nn.sigmoid(gate) * val).astype(BF16)
        return acc + _dot(act, wdn_ref[c])

    f = lax.fori_loop(0, nch, body, jnp.zeros((tm, x.shape[1]), F32))
    y_ref[0] = _rms(x + f, gf_ref[...])


def _ffn(x, prev, g, wup, cw, cb, wdn, gf, *, tm, prev_rows):
    b, t, d = x.shape
    nch = wdn.shape[0]
    c3 = lambda bi, i: (0, 0, 0)
    if prev_rows is None:
        prev_spec = pl.BlockSpec((1, 2 * nch, SUBLANE, FFN_CHUNK), lambda bi, i: (bi, 0, 0, 0))
        u_spec = pl.BlockSpec((1, 2 * nch, SUBLANE, FFN_CHUNK), lambda bi, i: (bi, 0, 0, 0))
        u_shape = jax.ShapeDtypeStruct((b, 2 * nch, SUBLANE, FFN_CHUNK), F32)
    else:
        prev_spec = pl.BlockSpec((2 * nch, tm, FFN_CHUNK), lambda bi, i: (0, bi * (t // tm) + i, 0))
        u_spec = pl.BlockSpec((2 * nch, tm, FFN_CHUNK), lambda bi, i: (0, bi * (t // tm) + i, 0))
        u_shape = jax.ShapeDtypeStruct((2 * nch, b * t, FFN_CHUNK), F32)
    return pl.pallas_call(
        functools.partial(_ffn_kernel, tm=tm, nch=nch, prev_rows=prev_rows), grid=(b, t // tm),
        in_specs=[pl.BlockSpec((1, tm, d), lambda bi, i: (bi, i, 0)), prev_spec, pl.BlockSpec(g.shape, lambda bi, i: (0, 0)),
                  pl.BlockSpec(wup.shape, c3), pl.BlockSpec(cw.shape, c3), pl.BlockSpec(cb.shape, c3),
                  pl.BlockSpec(wdn.shape, c3), pl.BlockSpec(gf.shape, lambda bi, i: (0, 0))],
        out_specs=[pl.BlockSpec((1, tm, d), lambda bi, i: (bi, i, 0)), u_spec],
        out_shape=[jax.ShapeDtypeStruct((b, t, d), F32), u_shape],
        compiler_params=_cparams("parallel", "arbitrary"), name="conv_ffn")(x, prev, g, wup, cw, cb, wdn, gf)


def _rope_tables(pos, scale):
    half = QK_ROPE // 2
    inv = 1.0 / (ROPE_THETA ** (np.arange(half, dtype=np.float64) / half))
    ang = np.asarray(pos, np.float64)[:, None] * inv[None, :]
    cos, sin = np.cos(ang), np.sin(ang)
    n = len(pos)
    c = np.zeros((n, HEAD_PAD))
    s = np.zeros((n, HEAD_PAD))
    c[:, :QK_NOPE] = 1.0
    c[:, ROPE_LO:ROPE_LO + half] = cos
    c[:, ROPE_LO + half:ROPE_LO + QK_ROPE] = cos
    s[:, ROPE_LO:ROPE_LO + half] = -sin
    s[:, ROPE_LO + half:ROPE_LO + QK_ROPE] = sin
    return (jnp.asarray(c * scale, F32), jnp.asarray(s * scale, F32), jnp.asarray(c, F32), jnp.asarray(s, F32))


def _swap_halves(w):
    half = w.shape[-1] // 2
    return jnp.concatenate([w[..., half:], w[..., :half]], axis=-1)


def _prep_weights(w_in, w_uq, w_ukv, rw_w2, rw_a2, w_up, conv_w, conv_b, w_down):
    d = w_in.shape[0]
    z = lambda *s: jnp.zeros(s, F32)
    w_kr = w_in[:, Q_LORA + KV_LORA:MLA_PROJ]
    pad_head = lambda w: jnp.concatenate([z(d, ROPE_LO), w, z(d, HEAD_PAD - ROPE_LO - QK_ROPE)], axis=1)
    w1 = jnp.concatenate([w_in[:, :Q_LORA + KV_LORA], pad_head(w_kr), pad_head(_swap_halves(w_kr)), w_in[:, MLA_PROJ:]],
                         axis=1).astype(BF16)
    wq3 = w_uq.reshape(Q_LORA, MLA_HEADS, QK_NOPE + QK_ROPE)
    zq = lambda n: z(Q_LORA, MLA_HEADS, n)
    q_plain = jnp.concatenate([wq3, zq(HEAD_PAD - QK_NOPE - QK_ROPE)], axis=2)
    q_swap = jnp.concatenate([zq(QK_NOPE), _swap_halves(wq3[..., QK_NOPE:]), zq(HEAD_PAD - QK_NOPE - QK_ROPE)], axis=2)
    wq = jnp.concatenate([q_plain.reshape(Q_LORA, -1), q_swap.reshape(Q_LORA, -1)], axis=1).astype(BF16)
    wkv3 = w_ukv.reshape(KV_LORA, MLA_HEADS, QK_NOPE + V_DIM)
    zk = z(KV_LORA, MLA_HEADS, HEAD_PAD - QK_NOPE)
    wk = jnp.concatenate([wkv3[..., :QK_NOPE], zk], axis=2).reshape(KV_LORA, -1).astype(BF16)
    wv = wkv3[..., QK_NOPE:].reshape(KV_LORA, -1).T.astype(BF16)
    w_uk = jnp.transpose(wkv3[..., :QK_NOPE], (1, 2, 0))
    w_uk = jnp.concatenate([w_uk, z(MLA_HEADS, HEAD_PAD - QK_NOPE, KV_LORA)], axis=1).astype(BF16)
    w_uv = jnp.transpose(wkv3[..., QK_NOPE:], (1, 0, 2)).astype(BF16)
    wa2 = jnp.concatenate([jnp.concatenate([rw_w2, z(A_LORA, RW_WIDTH)], axis=0),
                           jnp.concatenate([z(DECAY_LORA, RW_WIDTH), rw_a2], axis=0)], axis=1).astype(BF16)
    f2 = w_up.shape[1]
    nch2 = f2 // FFN_CHUNK
    wup = jnp.transpose(w_up.reshape(d, nch2, FFN_CHUNK), (1, 0, 2)).astype(BF16)
    cw = jnp.transpose(conv_w.reshape(CONV_W, nch2, FFN_CHUNK), (1, 0, 2))
    cw = jnp.concatenate([cw, z(nch2, SUBLANE - CONV_W, FFN_CHUNK)], axis=1)
    cb = conv_b.reshape(nch2, 1, FFN_CHUNK)
    wdn = w_down.reshape(nch2 // 2, FFN_CHUNK, d).astype(BF16)
    return w1, wq, wk, wv, w_uk, w_uv, wa2, wup, cw, cb, wdn


def _state_to_pairs(s):
    b = s.shape[0]
    s = s.reshape(b, PAIRS, 2, RW_HEAD_DIM, RW_HEAD_DIM)
    zz = jnp.zeros_like(s[:, :, 0])
    top = jnp.concatenate([s[:, :, 0], zz], axis=-1)
    bot = jnp.concatenate([zz, s[:, :, 1]], axis=-1)
    return jnp.concatenate([top, bot], axis=-2)


def _pairs_to_state(s):
    b = s.shape[0]
    h0 = s[:, :, :RW_HEAD_DIM, :RW_HEAD_DIM]
    h1 = s[:, :, RW_HEAD_DIM:, RW_HEAD_DIM:]
    return jnp.stack([h0, h1], axis=2).reshape(b, RW_HEADS, RW_HEAD_DIM, RW_HEAD_DIM)


def _pick(n, pref):
    for t in pref:
        if n % t == 0:
            return t
    return n


def kernel(x_prompt, x_sample, cache_mla_latent, cache_mla_krope, cache_mem_k, cache_mem_v, state_rwkv, state_rwkv_shift, state_ffn_conv, page_table, mem_prompt, g_mix, w_in, q_norm_g, kv_norm_g, w_uq, w_ukv, g_mla_out, rw_mu, rw_w0, rw_w2, rw_a0, rw_a2, rw_g2, rw_k_k, rw_k_a, rw_r_k, rw_lnx_g, rw_lnx_b, w_o, g_mem_q, g_mem_kv, w_mq, w_mk, w_mv, w_mo, g_ffn, w_up, conv_w, conv_b, w_down, g_final):
    depth = w_in.shape[0]
    assert depth == 1, "single-layer step"
    bp, tp, d = x_prompt.shape
    bs, ts, _ = x_sample.shape
    npages = page_table.shape[1]
    past_len = npages * PAGE_SIZE
    row2 = lambda a: a.reshape(1, -1)
    l = 0
    w1, wq, wk, wv, w_uk, w_uv, wa2, wup, cw, cb, wdn = _prep_weights(
        w_in[l], w_uq[l], w_ukv[l], rw_w2[l], rw_a2[l], w_up[l], conv_w[l], conv_b[l], w_down[l])
    wo_a, wo_b = w_o[l, :RW_WIDTH].astype(BF16), w_o[l, RW_WIDTH:].astype(BF16)
    wmq, wmo = w_mq[l].astype(BF16), w_mo[l].astype(BF16)
    wmkv = jnp.concatenate([w_mk[l], w_mv[l]], axis=1).astype(BF16)
    g2 = rw_g2[l].astype(BF16)
    nch2 = wup.shape[0]
    rw_vecs = (row2(rw_mu[l]), row2(rw_w0[l]), wa2, row2(rw_a0[l]), g2, row2(rw_k_k[l]), row2(rw_k_a[l]),
               row2(rw_r_k[l]), row2(rw_lnx_g[l]), row2(rw_lnx_b[l]))

    def after_attention(x2, orw, omla, mem_k, mem_v, conv_in, b, t, tm, tm_mem, tm_ffn, prev_rows):
        x1 = _mix(x2, orw, omla, row2(g_mla_out[l]), wo_a, wo_b, tm=tm)
        qm = _norm_mm(x1, row2(g_mem_q[l]), wmq, BF16, tm=tm)
        om = _memattn(qm.reshape(b, t, d), mem_k, mem_v, tm=tm_mem)
        xm = _mm_res(om.reshape(b * t, d), wmo, x1, tm=tm)
        fb, ft = (b, t) if prev_rows is None else (1, b * t)
        return _ffn(xm.reshape(fb, ft, d), conv_in, row2(g_ffn[l]), wup, cw, cb, wdn, row2(g_final), tm=tm_ffn,
                    prev_rows=prev_rows)

    n_p = bp * tp
    tm_p = _pick(tp, (512, 256, 128, 64, 32, 16, 8))
    tabs_p = _rope_tables(np.arange(tp), MLA_SCALE)
    xp2 = x_prompt.reshape(n_p, d)
    q_p, lat_p, kr_p, prw_p, k_p, v_p = _inproj(
        xp2, row2(g_mix[l]), w1, row2(q_norm_g[l]), row2(kv_norm_g[l]), wq, tabs_p, wk, wv, tm=tm_p, with_kv=True, seq=tp)
    tq = FLASH_T
    omla_p = _flash(q_p, k_p, v_p, batch=bp, seq=tp, tq=tq)
    tb_p = _pick(tp, (512, 256, 128, 64, 32, 16, 8))
    if tb_p % QCH == 0:
        zeros_state = jnp.zeros((bp, RW_HEADS // QUAD, QW, QW), F32)
        orw_p, st_p = _rwkv_quad(prw_p.reshape(bp, tp, RW_PROJ), zeros_state, *rw_vecs, tb=tb_p)
        st_p = _quads_to_state(st_p)
    else:
        zeros_state = jnp.zeros((bp, PAIRS, LANE, LANE), F32)
        orw_p, st_p = _rwkv(prw_p.reshape(bp, tp, RW_PROJ), zeros_state, *rw_vecs, tb=tb_p,
                            ch=_pick(tb_p, (32, 16, 8)), row_lo=0, row_hi=tb_p)
        st_p = _pairs_to_state(st_p)
    mkv = _norm_mm(mem_prompt.reshape(-1, d), row2(g_mem_kv[l]), wmkv, F32, tm=_pick(mem_prompt.shape[0] * mem_prompt.shape[1], (512, 256, 128, 8)))
    n_mem = mem_prompt.shape[1]
    mk_p = mkv[:, :d].reshape(bp, n_mem, d)
    mv_p = mkv[:, d:].reshape(bp, n_mem, d)
    conv0_p = jnp.zeros((bp, nch2, SUBLANE, FFN_CHUNK), F32)
    y_p, u_p = after_attention(xp2, orw_p.reshape(n_p, RW_WIDTH), omla_p, mk_p, mv_p, conv0_p, bp, tp, tm_p, tm_p, tm_p, None)
    conv_p = jnp.transpose(u_p[:, :, SUBLANE - (CONV_W - 1):, :], (0, 2, 1, 3)).reshape(bp, CONV_W - 1, nch2 * FFN_CHUNK)

    n_s = bs * SROWS
    xs3 = jnp.pad(x_sample, ((0, 0), (SLO, SROWS - SLO - ts), (0, 0)))
    pos_s = np.tile(np.concatenate([np.zeros(SLO), past_len + np.arange(ts), np.zeros(SROWS - SLO - ts)]), bs)
    tm_s = _pick(n_s, (1024, 512, 256, 128, 64, 32, 16, 8))
    tabs_s = _rope_tables(pos_s[:tm_s], MLA_SCALE)
    q_s, lat_s, kr_s, prw_s = _inproj(
        xs3.reshape(n_s, d), row2(g_mix[l]), w1, row2(q_norm_g[l]), row2(kv_norm_g[l]), wq, tabs_s, None, None,
        tm=tm_s, with_kv=False)
    qabs = _bmm_cols(q_s, w_uk, BF16)
    qabs = qabs.reshape(MLA_HEADS, bs, SROWS, KV_LORA)[:, :, SLO:SLO + ts]
    qabs = jnp.transpose(qabs, (1, 0, 2, 3)).reshape(bs, MLA_HEADS * ts, KV_LORA)
    qrope = q_s.reshape(bs, SROWS, MLA_HEADS, HEAD_PAD)[:, SLO:SLO + ts, :, ROPE_LO:ROPE_LO + QK_ROPE]
    qrope = jnp.transpose(qrope, (0, 2, 1, 3)).reshape(bs, MLA_HEADS * ts, QK_ROPE)
    olat = _paged(page_table, qabs, qrope, lat_s.reshape(bs, SROWS, KV_LORA), kr_s.reshape(bs, SROWS, QK_ROPE),
                  cache_mla_latent[l], jnp.swapaxes(cache_mla_krope[l], 1, 2), n_new=ts)
    olat = jnp.transpose(olat.reshape(bs, MLA_HEADS, ts, KV_LORA), (1, 0, 2, 3))
    olat = jnp.pad(olat, ((0, 0), (0, 0), (SLO, SROWS - SLO - ts), (0, 0))).reshape(MLA_HEADS, n_s, KV_LORA)
    omla_s = _bmm(olat, w_uv, F32)
    omla_s = jnp.transpose(omla_s, (1, 0, 2)).reshape(n_s, MLA_HEADS * V_DIM)
    prw_s3 = prw_s.reshape(bs, SROWS, RW_PROJ).at[:, SLO - 1, :].set(state_rwkv_shift[l])
    orw_s, st_s = _rwkv(prw_s3, _state_to_pairs(state_rwkv[l]), *rw_vecs, tb=SROWS, ch=SROWS, row_lo=SLO, row_hi=SLO + ts)
    hist = jnp.transpose(state_ffn_conv[l].reshape(bs, CONV_W - 1, nch2, FFN_CHUNK), (2, 0, 1, 3))
    hist = jnp.pad(hist, ((0, 0), (0, 0), (SLO - (CONV_W - 1), SROWS - SLO), (0, 0))).reshape(nch2, n_s, FFN_CHUNK)
    mk_s = cache_mem_k[l].reshape(bs, cache_mem_k.shape[2], d)
    mv_s = cache_mem_v[l].reshape(bs, cache_mem_v.shape[2], d)
    y_s, u_s = after_attention(xs3.reshape(n_s, d), orw_s.reshape(n_s, RW_WIDTH), omla_s, mk_s, mv_s, hist,
                               bs, SROWS, tm_s, SROWS, _pick(n_s, (256, 128, 64, 32, 16, 8)), SLO)
    u_s = u_s.reshape(nch2, bs, SROWS, FFN_CHUNK)[:, :, SLO + ts - (CONV_W - 1):SLO + ts]
    conv_s = jnp.transpose(u_s, (1, 2, 0, 3)).reshape(bs, CONV_W - 1, nch2 * FFN_CHUNK)

    real = lambda a, w: a.reshape(bs, SROWS, w)[:, SLO:SLO + ts]
    mem5 = lambda a: a.reshape(1, bp, n_mem, MEM_HEADS, d // MEM_HEADS)
    return (y_p, real(y_s, d),
            lat_p.reshape(1, bp, tp, KV_LORA), kr_p.reshape(1, bp, tp, QK_ROPE), mem5(mk_p), mem5(mv_p),
            st_p[None], prw_p.reshape(bp, tp, RW_PROJ)[:, -1][None], conv_p[None],
            real(lat_s, KV_LORA)[None], real(kr_s, QK_ROPE)[None], _pairs_to_state(st_s)[None],
            prw_s.reshape(bs, SROWS, RW_PROJ)[:, SLO + ts - 1][None], conv_s[None])
```

```python
import functools

import numpy as np
import jax
import jax.numpy as jnp
from jax import lax
from jax.experimental import pallas as pl
from jax.experimental.pallas import tpu as pltpu

F32 = jnp.float32
BF16 = jnp.bfloat16

RW_HEADS = 8
RW_HEAD_DIM = 64
RW_WIDTH = 512
DECAY_LORA = 64
A_LORA = 64
GATE_LORA = 128
RW_PROJ = 3 * RW_WIDTH + DECAY_LORA + A_LORA + GATE_LORA
LNX_EPS = 64e-5
MLA_HEADS = 8
QK_NOPE = 64
QK_ROPE = 32
V_DIM = 64
Q_LORA = 384
KV_LORA = 256
MLA_PROJ = Q_LORA + KV_LORA + QK_ROPE
MLA_SCALE = (QK_NOPE + QK_ROPE) ** -0.5
ROPE_THETA = 10000.0
MEM_HEADS = 4
CONV_W = 3
NORM_EPS = 1e-6
PAGE_SIZE = 128

LANE = 128
SUBLANE = 8
VMEM_LIMIT = 56 * 1024 * 1024
HEAD_PAD = LANE
ROPE_LO = QK_NOPE
SROWS = 8
SLO = 2
NEG = -1e30
LOG2E = float(np.log2(np.e))
HI = lax.Precision.HIGHEST


def _cparams(*sem):
    return pltpu.CompilerParams(dimension_semantics=sem, vmem_limit_bytes=VMEM_LIMIT)


def _rms(x, g, eps=NORM_EPS):
    return x * lax.rsqrt(jnp.mean(x * x, axis=-1, keepdims=True) + eps) * g


def _dot(a, b):
    return jnp.dot(a, b, preferred_element_type=F32)


def _dot_nt(a, b, precision=None):
    return lax.dot_general(a, b, (((1,), (1,)), ((), ())), preferred_element_type=F32, precision=precision)


def _dot_tn(a, b, precision=None):
    return lax.dot_general(a, b, (((0,), (0,)), ((), ())), preferred_element_type=F32, precision=precision)


def _doth(a, b):
    return jnp.dot(a, b, preferred_element_type=F32, precision=HI)


C_Q0, C_KV0, C_KR0, C_KRS0, C_RW0 = 0, Q_LORA, Q_LORA + KV_LORA, Q_LORA + KV_LORA + LANE, Q_LORA + KV_LORA + 2 * LANE
W1_COLS = C_RW0 + RW_PROJ


def _inproj_kernel(x_ref, g_ref, w1_ref, qg_ref, kvg_ref, wq_ref, cq_ref, sq_ref, ck_ref, sk_ref, *rest, with_kv):
    if with_kv:
        wk_ref, wv_ref, q_out, lat_out, kr_out, prw_out, k_out, v_out = rest
    else:
        q_out, lat_out, kr_out, prw_out = rest
    h = _rms(x_ref[...], g_ref[...]).astype(BF16)
    prw_out[...] = _dot(h, w1_ref[:, C_RW0:W1_COLS])
    pm = _dot(h, w1_ref[:, 0:C_RW0])
    cqn = _rms(pm[:, C_Q0:C_KV0], qg_ref[...]).astype(BF16)
    lat = _rms(pm[:, C_KV0:C_KR0], kvg_ref[...])
    lat_out[...] = lat
    krope = pm[:, C_KR0:C_KRS0] * ck_ref[...] + pm[:, C_KRS0:C_RW0] * sk_ref[...]
    kr_out[...] = krope[:, ROPE_LO:ROPE_LO + QK_ROPE]
    q12 = _dot(cqn, wq_ref[...])
    nq = MLA_HEADS * HEAD_PAD
    cq, sq = cq_ref[...], sq_ref[...]
    for hd in range(MLA_HEADS):
        a, b = hd * HEAD_PAD, (hd + 1) * HEAD_PAD
        q_out[:, a:b] = (q12[:, a:b] * cq + q12[:, nq + a:nq + b] * sq).astype(BF16)
    if with_kv:
        latb = lat.astype(BF16)
        kn = _dot(latb, wk_ref[...])
        for hd in range(MLA_HEADS):
            a, b = hd * HEAD_PAD, (hd + 1) * HEAD_PAD
            k_out[:, a:b] = (kn[:, a:b] + krope).astype(BF16)
        vt = _dot_nt(wv_ref[...], latb).astype(BF16)
        for jb in range(v_out.shape[1]):
            v_out[0, jb] = vt[:, jb * FLASH_T:(jb + 1) * FLASH_T]


def _inproj(x, g, w1, qg, kvg, wq, tabs, wk, wv, *, tm, with_kv, seq=None):
    n, d = x.shape
    ttab = tabs[0].shape[0]
    nt = ttab // tm
    row = lambda i: (i, 0)
    const = lambda i: (0, 0)
    tab = lambda i: (i % nt, 0)
    in_specs = [pl.BlockSpec((tm, d), row), pl.BlockSpec(g.shape, const), pl.BlockSpec(w1.shape, const),
                pl.BlockSpec(qg.shape, const), pl.BlockSpec(kvg.shape, const), pl.BlockSpec(wq.shape, const)]
    in_specs += [pl.BlockSpec((tm, LANE), tab)] * 4
    args = [x, g, w1, qg, kvg, wq, *tabs]
    nq = MLA_HEADS * HEAD_PAD
    out_shape = [jax.ShapeDtypeStruct((n, nq), BF16), jax.ShapeDtypeStruct((n, KV_LORA), F32),
                 jax.ShapeDtypeStruct((n, QK_ROPE), F32), jax.ShapeDtypeStruct((n, RW_PROJ), F32)]
    out_specs = [pl.BlockSpec((tm, nq), row), pl.BlockSpec((tm, KV_LORA), row),
                 pl.BlockSpec((tm, QK_ROPE), row), pl.BlockSpec((tm, RW_PROJ), row)]
    if with_kv:
        in_specs += [pl.BlockSpec(wk.shape, const), pl.BlockSpec(wv.shape, const)]
        args += [wk, wv]
        nv = MLA_HEADS * V_DIM
        npb = seq // tm
        out_shape += [jax.ShapeDtypeStruct((n, nq), BF16), jax.ShapeDtypeStruct((n // seq, seq // FLASH_T, nv, FLASH_T), BF16)]
        out_specs += [pl.BlockSpec((tm, nq), row),
                      pl.BlockSpec((1, tm // FLASH_T, nv, FLASH_T), lambda i: (i // npb, i % npb, 0, 0))]
    return pl.pallas_call(
        functools.partial(_inproj_kernel, with_kv=with_kv), grid=(n // tm,), in_specs=in_specs,
        out_specs=out_specs, out_shape=out_shape, compiler_params=_cparams("parallel"),
        name="inproj_kv" if with_kv else "inproj")(*args)


FLASH_HPB = 4
FLASH_T = 256
FLASH_LROWS = 16


def _flash_kernel(q_ref, k_ref, vt_ref, o_ref, *, tq):
    tk = FLASH_T
    qi = pl.program_id(2)
    nfull = qi * (tq // tk)
    qs = [q_ref[:, h * HEAD_PAD:(h + 1) * HEAD_PAD] for h in range(FLASH_HPB)]
    kidx = lax.broadcasted_iota(jnp.int32, (tk, tq), 0)
    qidx = lax.broadcasted_iota(jnp.int32, (tk, tq), 1)

    def scores(j):
        start = pl.multiple_of(j * tk, tk)
        return tuple(_dot_nt(k_ref[pl.ds(start, tk), h * HEAD_PAD:(h + 1) * HEAD_PAD], qs[h]) for h in range(FLASH_HPB))

    def step(j, sts, carry, diag_off):
        out = []
        for h in range(FLASH_HPB):
            m, acc = carry[h]
            st = sts[h]
            if diag_off is not None:
                st = jnp.where(kidx + diag_off <= qidx, st, NEG)
            m_new = jnp.maximum(m, jnp.max(st, axis=0, keepdims=True))
            alpha = jnp.exp2(m - m_new)
            p = jnp.exp2(st - m_new).astype(BF16)
            vt1 = jnp.concatenate([vt_ref[0, j, h * V_DIM:(h + 1) * V_DIM, :], ones], axis=0)
            acc = alpha * acc + _dot(vt1, p)
            out.append((m_new, acc))
        return tuple(out)

    ones = jnp.ones((FLASH_LROWS, tk), BF16)
    init = tuple((jnp.full((1, tq), NEG, F32), jnp.zeros((V_DIM + FLASH_LROWS, tq), F32)) for _ in range(FLASH_HPB))

    def body(j, c):
        nxt = scores(j + 1)
        return nxt, step(j, c[0], c[1], None)

    sts, carry = lax.fori_loop(0, nfull, body, (scores(0), init))
    ndiag = tq // tk
    for d in range(ndiag):
        nxt = scores(nfull + d + 1) if d + 1 < ndiag else None
        carry = step(nfull + d, sts, carry, d * tk)
        sts = nxt
    ot = jnp.concatenate([acc[:V_DIM] / acc[V_DIM:V_DIM + 1] for _, acc in carry], axis=0)
    o_ref[...] = ot.T


def _flash(q, k, vt, *, batch, seq, tq):
    nqb = seq // tq
    grid = (batch, MLA_HEADS // FLASH_HPB, nqb)
    w = FLASH_HPB * HEAD_PAD
    return pl.pallas_call(
        functools.partial(_flash_kernel, tq=tq), grid=grid,
        in_specs=[pl.BlockSpec((tq, w), lambda b, h, i: (b * nqb + i, h)),
                  pl.BlockSpec((seq, w), lambda b, h, i: (b, h)),
                  pl.BlockSpec((1, seq // FLASH_T, FLASH_HPB * V_DIM, FLASH_T), lambda b, h, i: (b, 0, h, 0))],
        out_specs=pl.BlockSpec((tq, FLASH_HPB * V_DIM), lambda b, h, i: (b * nqb + i, h)),
        out_shape=jax.ShapeDtypeStruct((batch * seq, MLA_HEADS * V_DIM), F32),
        compiler_params=_cparams("parallel", "parallel", "arbitrary"), name="mla_flash")(q, k, vt)


PAGES_PER_STEP = 32
PAGE_GROUP = 4
PAGE_STREAMS = 4


def _paged_kernel(pt_ref, qa_ref, qr_ref, latn_ref, krn_ref, *rest, n_new, g, group, streams):
    del pt_ref
    lat_refs, kr_refs = rest[:g], rest[g:2 * g]
    o_ref, m_s, l_s, acc_s = rest[2 * g:]
    step = pl.program_id(1)

    @pl.when(step == 0)
    def _():
        m_s[...] = jnp.full(m_s.shape, NEG, F32)
        l_s[...] = jnp.zeros(l_s.shape, F32)
        acc_s[...] = jnp.zeros(acc_s.shape, F32)

    qa = qa_ref[0]
    qr = qr_ref[0]

    def update(carry, s, vb):
        m, l, acc = carry
        m_new = jnp.maximum(m, jnp.max(s, axis=-1, keepdims=True))
        alpha = jnp.exp(m - m_new)
        p = jnp.exp(s - m_new)
        l = alpha * l + jnp.sum(p, axis=-1, keepdims=True)
        acc = alpha * acc + _dot(p.astype(BF16), vb)
        return m_new, l, acc

    def scores(grp):
        pages = range(grp * group, (grp + 1) * group)
        latb = jnp.concatenate([lat_refs[j][0].astype(BF16) for j in pages], axis=0)
        krt = jnp.concatenate([kr_refs[j][0].astype(BF16) for j in pages], axis=1)
        return _dot_nt(qa, latb) + _dot(qr, krt), latb

    carries = [(m_s[t], l_s[t], acc_s[t]) for t in range(streams)]
    ngrp = g // group
    ahead = streams
    pend = [scores(i) for i in range(min(ahead, ngrp))]
    for grp in range(ngrp):
        if grp + ahead < ngrp:
            pend.append(scores(grp + ahead))
        t = grp % streams
        carries[t] = update(carries[t], *pend[grp])
        pend[grp] = None
    for t in range(streams):
        m_s[t], l_s[t], acc_s[t] = carries[t]

    @pl.when(step == pl.num_programs(1) - 1)
    def _():
        latn = latn_ref[0].astype(BF16)
        krn = krn_ref[0].astype(BF16)
        s = _dot_nt(qa, latn) + _dot_nt(qr, krn)
        qt = lax.broadcasted_iota(jnp.int32, s.shape, 0) % n_new
        kt = lax.broadcasted_iota(jnp.int32, s.shape, 1) - SLO
        s = jnp.where((kt >= 0) & (kt <= qt), s, NEG)
        m, l, acc = update((m_s[0], l_s[0], acc_s[0]), s, latn)
        for t in range(1, streams):
            mt = m_s[t]
            m_new = jnp.maximum(m, mt)
            a, bt = jnp.exp(m - m_new), jnp.exp(mt - m_new)
            l = a * l + bt * l_s[t]
            acc = a * acc + bt * acc_s[t]
            m = m_new
        o_ref[0] = acc / l


def _paged(page_table, qa, qr, lat_new, kr_new, cache_lat, cache_kr, *, n_new):
    b, npages = page_table.shape
    g = max(d for d in range(1, PAGES_PER_STEP + 1) if npages % d == 0)
    group = max(d for d in range(1, PAGE_GROUP + 1) if g % d == 0)
    streams = min(PAGE_STREAMS, g // group)
    rows = qa.shape[1]
    lat_specs = [pl.BlockSpec((1, PAGE_SIZE, KV_LORA), functools.partial(lambda i, s, pt, j: (pt[i, s * g + j], 0, 0), j=j))
                 for j in range(g)]
    kr_specs = [pl.BlockSpec((1, QK_ROPE, PAGE_SIZE), functools.partial(lambda i, s, pt, j: (pt[i, s * g + j], 0, 0), j=j))
                for j in range(g)]
    seq3 = lambda i, s, pt: (i, 0, 0)
    grid_spec = pltpu.PrefetchScalarGridSpec(
        num_scalar_prefetch=1, grid=(b, npages // g),
        in_specs=[pl.BlockSpec((1, rows, KV_LORA), seq3), pl.BlockSpec((1, rows, QK_ROPE), seq3),
                  pl.BlockSpec((1, SROWS, KV_LORA), seq3), pl.BlockSpec((1, SROWS, QK_ROPE), seq3)] + lat_specs + kr_specs,
        out_specs=pl.BlockSpec((1, rows, KV_LORA), seq3),
        scratch_shapes=[pltpu.VMEM((streams, rows, 1), F32), pltpu.VMEM((streams, rows, 1), F32),
                        pltpu.VMEM((streams, rows, KV_LORA), F32)])
    return pl.pallas_call(
        functools.partial(_paged_kernel, n_new=n_new, g=g, group=group, streams=streams), grid_spec=grid_spec,
        out_shape=jax.ShapeDtypeStruct((b, rows, KV_LORA), F32),
        compiler_params=_cparams("parallel", "arbitrary"), name="mla_paged")(
            page_table, qa, qr, lat_new, kr_new, *([cache_lat] * g), *([cache_kr] * g))


def _bmm_kernel(a_ref, w_ref, o_ref):
    o_ref[0] = _dot(a_ref[...].astype(BF16), w_ref[0]).astype(o_ref.dtype)


def _bmm_cols(a, w, out_dtype):
    n = a.shape[0]
    hh, k, m = w.shape
    return pl.pallas_call(
        _bmm_kernel, grid=(hh,),
        in_specs=[pl.BlockSpec((n, k), lambda h: (0, h)), pl.BlockSpec((1, k, m), lambda h: (h, 0, 0))],
        out_specs=pl.BlockSpec((1, n, m), lambda h: (h, 0, 0)),
        out_shape=jax.ShapeDtypeStruct((hh, n, m), out_dtype), compiler_params=_cparams("parallel"),
        name="bmm_cols")(a, w)


def _bmm_kernel3(a_ref, w_ref, o_ref):
    o_ref[0] = _dot(a_ref[0].astype(BF16), w_ref[0]).astype(o_ref.dtype)


def _bmm(a, w, out_dtype):
    hh, n, k = a.shape
    m = w.shape[2]
    return pl.pallas_call(
        _bmm_kernel3, grid=(hh,),
        in_specs=[pl.BlockSpec((1, n, k), lambda h: (h, 0, 0)), pl.BlockSpec((1, k, m), lambda h: (h, 0, 0))],
        out_specs=pl.BlockSpec((1, n, m), lambda h: (h, 0, 0)),
        out_shape=jax.ShapeDtypeStruct((hh, n, m), out_dtype), compiler_params=_cparams("parallel"),
        name="bmm")(a, w)


PAIRS = RW_HEADS // 2
C_R, C_K, C_V, C_WA, C_G = 0, RW_WIDTH, 2 * RW_WIDTH, 3 * RW_WIDTH, 3 * RW_WIDTH + DECAY_LORA + A_LORA


def _seg_sum(x):
    lane = lax.broadcasted_iota(jnp.int32, (x.shape[0], LANE), 1)
    low = lane < RW_HEAD_DIM
    outs = []
    for t in range(x.shape[1] // LANE):
        xt = x[:, t * LANE:(t + 1) * LANE]
        s0 = jnp.sum(jnp.where(low, xt, 0.0), axis=-1, keepdims=True)
        s1 = jnp.sum(jnp.where(low, 0.0, xt), axis=-1, keepdims=True)
        outs.append(jnp.where(low, s0, s1))
    return outs[0] if len(outs) == 1 else jnp.concatenate(outs, axis=-1)


def _rwkv_prologue(p_ref, s0_ref, so_ref, vec_refs, scratch, *, tb, row_lo, row_hi):
    mu_ref, w0_ref, wa2_ref, a0_ref, g2_ref, kk_ref, ka_ref, rk_ref = vec_refs
    carry_s, r_s, k_s, v_s, kk_s, b_s, ld_s = scratch
    i = pl.program_id(1)

    @pl.when(i == 0)
    def _():
        carry_s[...] = jnp.zeros(carry_s.shape, F32)
        so_ref[0] = s0_ref[0]

    p = p_ref[0]
    rows1 = lax.broadcasted_iota(jnp.int32, (tb, 1), 0)
    prev = jnp.where(rows1 == 0, carry_s[SUBLANE - 1:SUBLANE, :], pltpu.roll(p, 1, axis=0))
    carry_s[...] = p[tb - SUBLANE:tb, :]
    s = p + (prev - p) * mu_ref[...]
    r = s[:, C_R:C_K]
    k = s[:, C_K:C_V]
    v = s[:, C_V:C_WA]
    wa = s[:, C_WA:C_G]
    lane = lax.broadcasted_iota(jnp.int32, wa.shape, 1)
    z = jnp.where(lane < DECAY_LORA, jnp.tanh(wa), wa).astype(BF16)
    lin = _dot(z, wa2_ref[...])
    xw = -(w0_ref[...] + lin[:, :RW_WIDTH])
    w = -(jnp.maximum(xw, 0.0) + jnp.log(1.0 + jnp.exp(-jnp.abs(xw)))) - 0.5
    logd = -jnp.exp(w)
    a = jax.nn.sigmoid(a0_ref[...] + lin[:, RW_WIDTH:])
    gate = _dot(jax.nn.sigmoid(s[:, C_G:RW_PROJ]).astype(BF16), g2_ref[...])
    kk = k * kk_ref[...]
    kk = kk / jnp.maximum(jnp.sqrt(_seg_sum(kk * kk)), 1e-12)
    k = k * (1.0 + (a - 1.0) * ka_ref[...])
    bonus = _seg_sum(r * k * rk_ref[...]) * v
    if row_lo > 0 or row_hi < tb:
        live = (rows1 >= row_lo) & (rows1 < row_hi)
        logd = jnp.where(live, logd, 0.0)
        kk = jnp.where(live, kk, 0.0)
        k = jnp.where(live, k, 0.0)
        v = jnp.where(live, v, 0.0)
    r_s[...] = r
    k_s[...] = k
    v_s[...] = v
    kk_s[...] = kk
    b_s[...] = kk * a
    ld_s[...] = logd
    return gate, bonus


def _rwkv_epilogue(out, gate, bonus, lg_ref, lb_ref, o_ref):
    mean = _seg_sum(out) * (1.0 / RW_HEAD_DIM)
    cen = out - mean
    var = _seg_sum(cen * cen) * (1.0 / RW_HEAD_DIM)
    y = cen * lax.rsqrt(var + LNX_EPS) * lg_ref[...] + lb_ref[...]
    o_ref[0] = ((y + bonus) * gate).astype(o_ref.dtype)


def _rwkv_kernel(p_ref, s0_ref, mu_ref, w0_ref, wa2_ref, a0_ref, g2_ref, kk_ref, ka_ref, rk_ref, lg_ref, lb_ref,
                 o_ref, so_ref, carry_s, r_s, k_s, v_s, kk_s, b_s, ld_s, out_s, *, tb, ch, row_lo, row_hi):
    gate, bonus = _rwkv_prologue(p_ref, s0_ref, so_ref, (mu_ref, w0_ref, wa2_ref, a0_ref, g2_ref, kk_ref, ka_ref, rk_ref),
                                 (carry_s, r_s, k_s, v_s, kk_s, b_s, ld_s), tb=tb, row_lo=row_lo, row_hi=row_hi)
    c2 = 2 * ch
    ri = lax.broadcasted_iota(jnp.int32, (c2, c2), 0)
    ci = lax.broadcasted_iota(jnp.int32, (c2, c2), 1)
    strict = (ci % ch) < (ri % ch)
    incl = (ci % ch) <= (ri % ch)
    eye = (ri == ci).astype(F32)
    tri = (lax.broadcasted_iota(jnp.int32, (ch, ch), 1) <= lax.broadcasted_iota(jnp.int32, (ch, ch), 0)).astype(F32)
    lane2 = lax.broadcasted_iota(jnp.int32, (c2, LANE), 1)
    row2 = lax.broadcasted_iota(jnp.int32, (c2, LANE), 0)
    own = (lane2 < RW_HEAD_DIM) == (row2 < ch)

    def stack(x):
        return jnp.where(own, jnp.concatenate([x, x], axis=0), 0.0)

    bf = lambda x: x.astype(BF16)
    pairs = [slice(pr * LANE, (pr + 1) * LANE) for pr in range(PAIRS)]

    def chunk(c, _):
        ts = pl.ds(pl.multiple_of(c * ch, ch), ch)
        css = [_doth(tri, ld_s[ts, ls]) for ls in pairs]
        lhss, amats, tails = [], [], []
        for ls, cs in zip(pairs, css):
            ld = ld_s[ts, ls]
            e_inv = jnp.exp(-cs)
            lhs = bf(jnp.concatenate([stack(kk_s[ts, ls] * jnp.exp(cs - ld)), stack(r_s[ts, ls] * jnp.exp(cs))], axis=0))
            rhs = bf(jnp.concatenate([stack(k_s[ts, ls] * e_inv), stack(b_s[ts, ls] * e_inv)], axis=0))
            lhss.append(lhs)
            amats.append(_dot_nt(lhs, rhs))
        sts = [so_ref[0, pr] for pr in range(PAIRS)]
        gmats = [_dot_nt(lhs, bf(st)) for lhs, st in zip(lhss, sts)]
        vss = [bf(stack(v_s[ts, ls])) for ls in pairs]
        akbs = [jnp.where(strict, amat[:c2, c2:], 0.0) for amat in amats]
        ykk = [_dot(bf(jnp.where(strict, amat[:c2, :c2], 0.0)), vs) for amat, vs in zip(amats, vss)]
        yrk = [_dot(bf(jnp.where(incl, amat[c2:, :c2], 0.0)), vs) for amat, vs in zip(amats, vss)]
        invs = [eye - jnp.where((ri // 2 == ci // 2), a_kb, 0.0) for a_kb in akbs]
        m = 2
        while m < ch:
            lvl = (ri // (2 * m) == ci // (2 * m)) & ((ri // m) % 2 == 1) & ((ci // m) % 2 == 0)
            tmps = [_dot(bf(inv), bf(jnp.where(lvl, a_kb, 0.0))) for inv, a_kb in zip(invs, akbs)]
            invs = [inv - _dot(bf(tmp), bf(inv)) for inv, tmp in zip(invs, tmps)]
            m *= 2
        us = [_dot(bf(inv), bf(gmat[:c2] + y)) for inv, gmat, y in zip(invs, gmats, ykk)]
        o2s = [gmat[c2:] + y - _dot(bf(jnp.where(incl, amat[c2:, c2:], 0.0)), bf(u))
               for gmat, y, amat, u in zip(gmats, yrk, amats, us)]
        for pr, (ls, cs, st, vs, u, o2) in enumerate(zip(pairs, css, sts, vss, us, o2s)):
            tot = cs[ch - 1:ch, :]
            e_tail = jnp.exp(tot - cs)
            out_s[ts, ls] = o2[:ch] + o2[ch:]
            so_ref[0, pr] = (st * jnp.exp(tot) + _dot_tn(vs, bf(stack(k_s[ts, ls] * e_tail)))
                             - _dot_tn(bf(u), bf(stack(b_s[ts, ls] * e_tail))))
        return 0

    lax.fori_loop(0, tb // ch, chunk, 0)
    _rwkv_epilogue(out_s[...], gate, bonus, lg_ref, lb_ref, o_ref)


def _rwkv(p, s0, mu, w0, wa2, a0, g2, k_k, k_a, r_k, lnx_g, lnx_b, *, tb, ch, row_lo, row_hi):
    b, t, _ = p.shape
    const = lambda bi, i: (0, 0)
    vec = lambda a: pl.BlockSpec(a.shape, const)
    scr = lambda: pltpu.VMEM((tb, RW_WIDTH), F32)
    return pl.pallas_call(
        functools.partial(_rwkv_kernel, tb=tb, ch=ch, row_lo=row_lo, row_hi=row_hi), grid=(b, t // tb),
        in_specs=[pl.BlockSpec((1, tb, RW_PROJ), lambda bi, i: (bi, i, 0)),
                  pl.BlockSpec((1, PAIRS, LANE, LANE), lambda bi, i: (bi, 0, 0, 0)),
                  vec(mu), vec(w0), vec(wa2), vec(a0), vec(g2), vec(k_k), vec(k_a), vec(r_k), vec(lnx_g), vec(lnx_b)],
        out_specs=[pl.BlockSpec((1, tb, RW_WIDTH), lambda bi, i: (bi, i, 0)),
                   pl.BlockSpec((1, PAIRS, LANE, LANE), lambda bi, i: (bi, 0, 0, 0))],
        out_shape=[jax.ShapeDtypeStruct((b, t, RW_WIDTH), BF16), jax.ShapeDtypeStruct((b, PAIRS, LANE, LANE), F32)],
        scratch_shapes=[pltpu.VMEM((SUBLANE, RW_PROJ), F32)] + [scr() for _ in range(7)],
        compiler_params=_cparams("parallel", "arbitrary"), name="rwkv")(
            p, s0, mu, w0, wa2, a0, g2, k_k, k_a, r_k, lnx_g, lnx_b)


QUAD = 4
QCH = RW_HEAD_DIM
QW = QUAD * RW_HEAD_DIM
QLEVELS = (2, 4, 8, 16, 32)
PREP_CHUNKS = 4


def _tile4(x):
    return jnp.concatenate([x] * QUAD, axis=0)


def _rwkv_quad_kernel(p_ref, s0_ref, mu_ref, w0_ref, wa2_ref, a0_ref, g2_ref, kk_ref, ka_ref, rk_ref, lg_ref, lb_ref,
                      tri_ref, bdm_ref, own_ref, sbm_ref,
                      o_ref, so_ref, carry_s, r_s, k_s, v_s, kk_s, b_s, ld_s, out_s,
                      lhs_c, arb_c, y_c, kt_c, dec_c, *, tb):
    gate, bonus = _rwkv_prologue(p_ref, s0_ref, so_ref, (mu_ref, w0_ref, wa2_ref, a0_ref, g2_ref, kk_ref, ka_ref, rk_ref),
                                 (carry_s, r_s, k_s, v_s, kk_s, b_s, ld_s), tb=tb, row_lo=0, row_hi=tb)
    ch = QCH
    nquad = RW_HEADS // QUAD
    bf = lambda x: x.astype(BF16)

    def bd(x):
        return _tile4(bf(x)) * bdm_ref[0]

    quads = [slice(q * QW, (q + 1) * QW) for q in range(nquad)]

    def prepare(chunks):
        inst = [(c, q, pl.ds(pl.multiple_of(c * ch, ch), ch), quads[q]) for c in chunks for q in range(nquad)]
        strict, incl, eye, lvl1 = sbm_ref[0] > 0, sbm_ref[1] > 0, sbm_ref[2], sbm_ref[3]
        css = []
        for c, q, ts, ls in inst:
            ld = ld_s[ts, ls]
            l1 = bf(ld)
            r1 = ld - l1.astype(F32)
            l2 = bf(r1)
            l3 = bf(r1 - l2.astype(F32))
            css.append(_dot(tri_ref[...], jnp.concatenate([l1, l2, l3], axis=0)))
        a1s, a2s = [], []
        for (c, q, ts, ls), cs in zip(inst, css):
            ld = ld_s[ts, ls]
            tot = cs[ch - 1:ch, :]
            e_inv = jnp.exp(-cs)
            kc, bc = k_s[ts, ls], b_s[ts, ls]
            lhs = bf(jnp.concatenate([kk_s[ts, ls] * jnp.exp(cs - ld), r_s[ts, ls] * jnp.exp(cs)], axis=0))
            lhs_c[c, q] = lhs
            e_tail = jnp.exp(tot - cs)
            kt_c[c, q] = bf(jnp.concatenate([kc * e_tail, bc * e_tail], axis=0))
            dec_c[c, q] = jnp.broadcast_to(jnp.exp(tot), (SUBLANE, QW))
            a1s.append(_dot_nt(lhs, bd(kc * e_inv)))
            a2s.append(_dot_nt(lhs, bd(bc * e_inv)))
        ys, akb4s, invs = [], [], []
        for (c, q, ts, ls), a1, a2 in zip(inst, a1s, a2s):
            a_kk = jnp.where(strict, a1[:ch], 0.0)
            a_rk = jnp.where(incl, a1[ch:], 0.0)
            a_kb = jnp.where(strict, a2[:ch], 0.0)
            arb_c[c, q] = bf(jnp.where(incl, a2[ch:], 0.0))
            ys.append(_dot(bf(jnp.concatenate([a_kk, a_rk], axis=0)), bd(v_s[ts, ls])))
            akb4s.append(_tile4(bf(a_kb)))
            invs.append(eye - a_kb * lvl1)
        for li in range(len(QLEVELS)):
            tmps = [_dot(bf(inv), akb4 * bdm_ref[1 + li]) for inv, akb4 in zip(invs, akb4s)]
            invs = [inv - _dot(bf(tmp), bd(inv)) for inv, tmp in zip(invs, tmps)]
        ws = [_dot(bf(inv), _tile4(lhs_c[c, q][:ch]) * bdm_ref[0]) for (c, q, ts, ls), inv in zip(inst, invs)]
        yks = [_dot(bf(inv), bd(y[:ch])) for inv, y in zip(invs, ys)]
        for (c, q, ts, ls), w, yk, y in zip(inst, ws, yks, ys):
            lhs_c[c, q, 0:ch, :] = bf(w)
            y_c[c, q] = jnp.concatenate([yk, y[ch:]], axis=0)

    nc = tb // ch
    if nc % PREP_CHUNKS == 0:
        def prepare_loop(cc, _):
            prepare([PREP_CHUNKS * cc + d for d in range(PREP_CHUNKS)])
            return 0
        lax.fori_loop(0, nc // PREP_CHUNKS, prepare_loop, 0)
    else:
        lax.fori_loop(0, nc, lambda c, _: (prepare([c]), 0)[1], 0)

    def recur(c, _):
        ts = pl.ds(pl.multiple_of(c * ch, ch), ch)
        sts = [so_ref[0, q] for q in range(nquad)]
        gys = [_dot_nt(lhs_c[c, q], bf(sts[q])) + y_c[c, q] for q in range(nquad)]
        us = [gys[q][:ch] for q in range(nquad)]
        outs = [gys[q][ch:] - _dot(arb_c[c, q], bd(us[q])) for q in range(nquad)]
        upds = [_dot_tn(bf(jnp.concatenate([v_s[ts, quads[q]], -us[q]], axis=0)), kt_c[c, q]) for q in range(nquad)]
        for q in range(nquad):
            out_s[ts, quads[q]] = outs[q]
            so_ref[0, q] = sts[q] * dec_c[c, q][0:1, :] + upds[q] * own_ref[...]
        return 0

    lax.fori_loop(0, nc, recur, 0)
    _rwkv_epilogue(out_s[...], gate, bonus, lg_ref, lb_ref, o_ref)


def _quad_masks():
    ch = QCH
    r = np.arange(QUAD * ch)[:, None]
    c = np.arange(QW)[None, :]
    own = (r // ch) == (c // RW_HEAD_DIM)
    sp, s = r % ch, c % ch
    bdm = [own]
    for m in QLEVELS:
        bdm.append(own & (sp // (2 * m) == s // (2 * m)) & ((sp // m) % 2 == 1) & ((s // m) % 2 == 0))
    t = np.arange(ch)[:, None]
    sbm = [s < t, s <= t, s == t, (t // 2 == s // 2) & (t % 2 == 1) & (s % 2 == 0)]
    tri = np.tile(np.arange(ch)[None, :] <= np.arange(ch)[:, None], (1, 3))
    return (jnp.asarray(tri, BF16), jnp.asarray(np.stack(bdm), BF16), jnp.asarray(own, F32),
            jnp.asarray(np.stack([np.broadcast_to(m, (ch, QW)) for m in sbm]), F32))


def _rwkv_quad(p, s0, mu, w0, wa2, a0, g2, k_k, k_a, r_k, lnx_g, lnx_b, *, tb):
    b, t, _ = p.shape
    nq = RW_HEADS // QUAD
    nc = tb // QCH
    masks = _quad_masks()
    vec = lambda a: pl.BlockSpec(a.shape, lambda bi, i: (0,) * a.ndim)
    scr = lambda: pltpu.VMEM((tb, RW_WIDTH), F32)
    vecs = (mu, w0, wa2, a0, g2, k_k, k_a, r_k, lnx_g, lnx_b) + masks
    return pl.pallas_call(
        functools.partial(_rwkv_quad_kernel, tb=tb), grid=(b, t // tb),
        in_specs=[pl.BlockSpec((1, tb, RW_PROJ), lambda bi, i: (bi, i, 0)),
                  pl.BlockSpec((1, nq, QW, QW), lambda bi, i: (bi, 0, 0, 0))] + [vec(a) for a in vecs],
        out_specs=[pl.BlockSpec((1, tb, RW_WIDTH), lambda bi, i: (bi, i, 0)),
                   pl.BlockSpec((1, nq, QW, QW), lambda bi, i: (bi, 0, 0, 0))],
        out_shape=[jax.ShapeDtypeStruct((b, t, RW_WIDTH), BF16), jax.ShapeDtypeStruct((b, nq, QW, QW), F32)],
        scratch_shapes=[pltpu.VMEM((SUBLANE, RW_PROJ), F32)] + [scr() for _ in range(7)] + [
            pltpu.VMEM((nc, nq, 2 * QCH, QW), BF16),
            pltpu.VMEM((nc, nq, QCH, QW), BF16), pltpu.VMEM((nc, nq, 2 * QCH, QW), F32),
            pltpu.VMEM((nc, nq, 2 * QCH, QW), BF16), pltpu.VMEM((nc, nq, SUBLANE, QW), F32)],
        compiler_params=_cparams("parallel", "arbitrary"), name="rwkv_quad")(p, s0, *vecs)


def _quads_to_state(sq):
    b = sq.shape[0]
    s6 = sq.reshape(b, RW_HEADS // QUAD, QUAD, RW_HEAD_DIM, QUAD, RW_HEAD_DIM)
    return jnp.stack([s6[:, :, h, :, h, :] for h in range(QUAD)], axis=2).reshape(b, RW_HEADS, RW_HEAD_DIM, RW_HEAD_DIM)


def _mix_kernel(x_ref, orw_ref, omla_ref, g_ref, wa_ref, wb_ref, o_ref):
    om = _rms(omla_ref[...], g_ref[...]).astype(BF16)
    o_ref[...] = x_ref[...] + _dot(orw_ref[...], wa_ref[...]) + _dot(om, wb_ref[...])


def _mix(x, orw, omla, g, wa, wb, *, tm):
    n, d = x.shape
    row = lambda i: (i, 0)
    const = lambda i: (0, 0)
    return pl.pallas_call(
        _mix_kernel, grid=(n // tm,),
        in_specs=[pl.BlockSpec((tm, d), row), pl.BlockSpec((tm, orw.shape[1]), row), pl.BlockSpec((tm, omla.shape[1]), row),
                  pl.BlockSpec(g.shape, const), pl.BlockSpec(wa.shape, const), pl.BlockSpec(wb.shape, const)],
        out_specs=pl.BlockSpec((tm, d), row), out_shape=jax.ShapeDtypeStruct((n, d), F32),
        compiler_params=_cparams("parallel"), name="mix_out")(x, orw, omla, g, wa, wb)


def _norm_mm_kernel(x_ref, g_ref, w_ref, o_ref):
    o_ref[...] = _dot(_rms(x_ref[...], g_ref[...]).astype(BF16), w_ref[...]).astype(o_ref.dtype)


def _norm_mm(x, g, w, out_dtype, *, tm):
    n, d = x.shape
    m = w.shape[1]
    return pl.pallas_call(
        _norm_mm_kernel, grid=(n // tm,),
        in_specs=[pl.BlockSpec((tm, d), lambda i: (i, 0)), pl.BlockSpec(g.shape, lambda i: (0, 0)),
                  pl.BlockSpec(w.shape, lambda i: (0, 0))],
        out_specs=pl.BlockSpec((tm, m), lambda i: (i, 0)), out_shape=jax.ShapeDtypeStruct((n, m), out_dtype),
        compiler_params=_cparams("parallel"), name="norm_mm")(x, g, w)


def _mm_res_kernel(a_ref, w_ref, x_ref, o_ref):
    o_ref[...] = x_ref[...] + _dot(a_ref[...], w_ref[...])


def _mm_res(a, w, x, *, tm):
    n, d = x.shape
    return pl.pallas_call(
        _mm_res_kernel, grid=(n // tm,),
        in_specs=[pl.BlockSpec((tm, a.shape[1]), lambda i: (i, 0)), pl.BlockSpec(w.shape, lambda i: (0, 0)),
                  pl.BlockSpec((tm, d), lambda i: (i, 0))],
        out_specs=pl.BlockSpec((tm, d), lambda i: (i, 0)), out_shape=jax.ShapeDtypeStruct((n, d), F32),
        compiler_params=_cparams("parallel"), name="mm_res")(a, w, x)


def _memattn_kernel(q_ref, k_ref, v_ref, o_ref):
    dh = q_ref.shape[2] // MEM_HEADS
    scale = dh ** -0.5
    for h in range(MEM_HEADS):
        cs = slice(h * dh, (h + 1) * dh)
        s = _dot_nt(q_ref[0, :, cs], k_ref[0, :, cs].astype(BF16)) * scale
        p = jnp.exp(s - jnp.max(s, axis=-1, keepdims=True))
        l = jnp.sum(p, axis=-1, keepdims=True)
        o_ref[0, :, cs] = (_dot(p.astype(BF16), v_ref[0, :, cs].astype(BF16)) / l).astype(o_ref.dtype)


def _memattn(q, mk, mv, *, tm):
    b, t, d = q.shape
    mem_spec = pl.BlockSpec((1,) + mk.shape[1:], lambda bi, i: (bi,) + (0,) * (mk.ndim - 1))
    return pl.pallas_call(
        _memattn_kernel, grid=(b, t // tm),
        in_specs=[pl.BlockSpec((1, tm, d), lambda bi, i: (bi, i, 0)), mem_spec, mem_spec],
        out_specs=pl.BlockSpec((1, tm, d), lambda bi, i: (bi, i, 0)), out_shape=jax.ShapeDtypeStruct((b, t, d), BF16),
        compiler_params=_cparams("parallel", "parallel"), name="mem_attn")(q, mk, mv)


def _memblock_kernel(x_ref, orw_ref, omla_ref, gmo_ref, wa_ref, wb_ref, gq_ref, wq_ref, k_ref, v_ref, wo_ref, o_ref):
    om = _rms(omla_ref[0], gmo_ref[...]).astype(BF16)
    x1 = x_ref[0] + _dot(orw_ref[0], wa_ref[...]) + _dot(om, wb_ref[...])
    q = _dot(_rms(x1, gq_ref[...]).astype(BF16), wq_ref[...]).astype(BF16)
    dh = q.shape[1] // MEM_HEADS
    scale = dh ** -0.5
    outs = []
    for h in range(MEM_HEADS):
        cs = slice(h * dh, (h + 1) * dh)
        s = _dot_nt(q[:, cs], k_ref[0, :, cs].astype(BF16)) * scale
        p = jnp.exp(s - jnp.max(s, axis=-1, keepdims=True))
        l = jnp.sum(p, axis=-1, keepdims=True)
        outs.append((_dot(p.astype(BF16), v_ref[0, :, cs].astype(BF16)) / l).astype(BF16))
    o_ref[0] = x1 + _dot(jnp.concatenate(outs, axis=1), wo_ref[...])


def _memblock(x, orw, omla, gmo, wa, wb, gq, wq, mk, mv, wo, *, tm):
    b, t, d = x.shape
    row = lambda bi, i: (bi, i, 0)
    seq = lambda bi, i: (bi, 0, 0)
    const = lambda bi, i: (0, 0)
    cs = lambda a: pl.BlockSpec(a.shape, const)
    return pl.pallas_call(
        _memblock_kernel, grid=(b, t // tm),
        in_specs=[pl.BlockSpec((1, tm, d), row), pl.BlockSpec((1, tm, orw.shape[2]), row), pl.BlockSpec((1, tm, omla.shape[2]), row),
                  cs(gmo), cs(wa), cs(wb), cs(gq), cs(wq), pl.BlockSpec((1,) + mk.shape[1:], seq),
                  pl.BlockSpec((1,) + mv.shape[1:], seq), cs(wo)],
        out_specs=pl.BlockSpec((1, tm, d), row), out_shape=jax.ShapeDtypeStruct((b, t, d), F32),
        compiler_params=_cparams("parallel", "parallel"), name="mem_block")(x, orw, omla, gmo, wa, wb, gq, wq, mk, mv, wo)


FFN_CHUNK = 256
FFN_GROUP = 4


def _ffn_kernel(x_ref, prev_ref, g_ref, wup_ref, cw_ref, cb_ref, wdn_ref, gf_ref, y_ref, u_ref, act_s, *, tm, nch, prev_rows):
    i = pl.program_id(1)
    x = x_ref[0]
    h = _rms(x, g_ref[...]).astype(BF16)
    rows = lax.broadcasted_iota(jnp.int32, (tm, 1), 0)
    if prev_rows is None:
        @pl.when(i == 0)
        def _():
            u_ref[0] = prev_ref[0]
    else:
        hist = (rows % SROWS) < prev_rows

    def up(c):
        return _dot(h, wup_ref[c]), _dot(h, wup_ref[c + nch])

    def conv(c, u):
        if prev_rows is None:
            tail = u_ref[0, c]
            u_ref[0, c] = u[tm - SUBLANE:tm, :]
            p1, p2 = tail[SUBLANE - 1:SUBLANE, :], tail[SUBLANE - 2:SUBLANE - 1, :]
            u1, u2 = pltpu.roll(u, 1, axis=0), pltpu.roll(u, 2, axis=0)
            r8 = rows[:SUBLANE]
            u1 = jnp.concatenate([jnp.where(r8 == 0, p1, u1[:SUBLANE]), u1[SUBLANE:]], axis=0)
            u2 = jnp.concatenate([jnp.where(r8 == 0, p2, jnp.where(r8 == 1, p1, u2[:SUBLANE])), u2[SUBLANE:]], axis=0)
        else:
            u = jnp.where(hist, prev_ref[c], u)
            u_ref[c] = u
            u1 = pltpu.roll(u, 1, axis=0)
            u2 = pltpu.roll(u, 2, axis=0)
        cw = cw_ref[c]
        return cb_ref[c] + cw[0:1, :] * u2 + cw[1:2, :] * u1 + cw[2:3, :] * u

    def gated(c, ug, uv):
        gate = conv(c, ug)
        val = conv(c + nch, uv)
        act_s[c] = (gate * jax.nn.sigmoid(gate) * val).astype(BF16)

    def group(cs):
        ups = [up(c) for c in cs]
        for c, (ug, uv) in zip(cs, ups):
            gated(c, ug, uv)

    ngrp = nch // FFN_GROUP

    def body(gi, _):
        group([gi * FFN_GROUP + d for d in range(FFN_GROUP)])
        return 0

    lax.fori_loop(0, ngrp, body, 0)
    if nch % FFN_GROUP:
        group(list(range(ngrp * FFN_GROUP, nch)))
    f = _dot(jnp.concatenate([act_s[c] for c in range(nch)], axis=1), wdn_ref[...])
    y_ref[0] = _rms(x + f, gf_ref[...])


def _ffn(x, prev, g, wup, cw, cb, wdn, gf, *, tm, prev_rows):
    b, t, d = x.shape
    nch = wup.shape[0] // 2
    c3 = lambda bi, i: (0, 0, 0)
    if prev_rows is None:
        prev_spec = pl.BlockSpec((1, 2 * nch, SUBLANE, FFN_CHUNK), lambda bi, i: (bi, 0, 0, 0))
        u_spec = pl.BlockSpec((1, 2 * nch, SUBLANE, FFN_CHUNK), lambda bi, i: (bi, 0, 0, 0))
        u_shape = jax.ShapeDtypeStruct((b, 2 * nch, SUBLANE, FFN_CHUNK), F32)
    else:
        prev_spec = pl.BlockSpec((2 * nch, tm, FFN_CHUNK), lambda bi, i: (0, bi * (t // tm) + i, 0))
        u_spec = pl.BlockSpec((2 * nch, tm, FFN_CHUNK), lambda bi, i: (0, bi * (t // tm) + i, 0))
        u_shape = jax.ShapeDtypeStruct((2 * nch, b * t, FFN_CHUNK), F32)
    return pl.pallas_call(
        functools.partial(_ffn_kernel, tm=tm, nch=nch, prev_rows=prev_rows), grid=(b, t // tm),
        in_specs=[pl.BlockSpec((1, tm, d), lambda bi, i: (bi, i, 0)), prev_spec, pl.BlockSpec(g.shape, lambda bi, i: (0, 0)),
                  pl.BlockSpec(wup.shape, c3), pl.BlockSpec(cw.shape, c3), pl.BlockSpec(cb.shape, c3),
                  pl.BlockSpec(wdn.shape, lambda bi, i: (0, 0)), pl.BlockSpec(gf.shape, lambda bi, i: (0, 0))],
        out_specs=[pl.BlockSpec((1, tm, d), lambda bi, i: (bi, i, 0)), u_spec],
        out_shape=[jax.ShapeDtypeStruct((b, t, d), F32), u_shape],
        scratch_shapes=[pltpu.VMEM((nch, tm, FFN_CHUNK), BF16)],
        compiler_params=_cparams("parallel", "arbitrary"), name="conv_ffn")(x, prev, g, wup, cw, cb, wdn, gf)


def _rope_tables(pos, scale):
    half = QK_ROPE // 2
    inv = 1.0 / (ROPE_THETA ** (np.arange(half, dtype=np.float64) / half))
    ang = np.asarray(pos, np.float64)[:, None] * inv[None, :]
    cos, sin = np.cos(ang), np.sin(ang)
    n = len(pos)
    c = np.zeros((n, HEAD_PAD))
    s = np.zeros((n, HEAD_PAD))
    c[:, :QK_NOPE] = 1.0
    c[:, ROPE_LO:ROPE_LO + half] = cos
    c[:, ROPE_LO + half:ROPE_LO + QK_ROPE] = cos
    s[:, ROPE_LO:ROPE_LO + half] = -sin
    s[:, ROPE_LO + half:ROPE_LO + QK_ROPE] = sin
    return (jnp.asarray(c * scale, F32), jnp.asarray(s * scale, F32), jnp.asarray(c, F32), jnp.asarray(s, F32))


def _swap_halves(w):
    half = w.shape[-1] // 2
    return jnp.concatenate([w[..., half:], w[..., :half]], axis=-1)


def _prep_weights(w_in, w_uq, w_ukv, rw_w2, rw_a2, w_up, conv_w, conv_b, w_down):
    d = w_in.shape[0]
    z = lambda *s: jnp.zeros(s, F32)
    w_kr = w_in[:, Q_LORA + KV_LORA:MLA_PROJ]
    pad_head = lambda w: jnp.concatenate([z(d, ROPE_LO), w, z(d, HEAD_PAD - ROPE_LO - QK_ROPE)], axis=1)
    w1 = jnp.concatenate([w_in[:, :Q_LORA + KV_LORA], pad_head(w_kr), pad_head(_swap_halves(w_kr)), w_in[:, MLA_PROJ:]],
                         axis=1).astype(BF16)
    wq3 = w_uq.reshape(Q_LORA, MLA_HEADS, QK_NOPE + QK_ROPE)
    zq = lambda n: z(Q_LORA, MLA_HEADS, n)
    q_plain = jnp.concatenate([wq3, zq(HEAD_PAD - QK_NOPE - QK_ROPE)], axis=2)
    q_swap = jnp.concatenate([zq(QK_NOPE), _swap_halves(wq3[..., QK_NOPE:]), zq(HEAD_PAD - QK_NOPE - QK_ROPE)], axis=2)
    wq = jnp.concatenate([q_plain.reshape(Q_LORA, -1), q_swap.reshape(Q_LORA, -1)], axis=1).astype(BF16)
    wkv3 = w_ukv.reshape(KV_LORA, MLA_HEADS, QK_NOPE + V_DIM)
    zk = z(KV_LORA, MLA_HEADS, HEAD_PAD - QK_NOPE)
    wk = jnp.concatenate([wkv3[..., :QK_NOPE], zk], axis=2).reshape(KV_LORA, -1).astype(BF16)
    wv = wkv3[..., QK_NOPE:].reshape(KV_LORA, -1).T.astype(BF16)
    w_uk = jnp.transpose(wkv3[..., :QK_NOPE], (1, 2, 0))
    w_uk = jnp.concatenate([w_uk, z(MLA_HEADS, HEAD_PAD - QK_NOPE, KV_LORA)], axis=1).astype(BF16)
    w_uv = jnp.transpose(wkv3[..., QK_NOPE:], (1, 0, 2)).astype(BF16)
    wa2 = jnp.concatenate([jnp.concatenate([rw_w2, z(A_LORA, RW_WIDTH)], axis=0),
                           jnp.concatenate([z(DECAY_LORA, RW_WIDTH), rw_a2], axis=0)], axis=1).astype(BF16)
    f2 = w_up.shape[1]
    nch2 = f2 // FFN_CHUNK
    wup = jnp.transpose(w_up.reshape(d, nch2, FFN_CHUNK), (1, 0, 2)).astype(BF16)
    cw = jnp.transpose(conv_w.reshape(CONV_W, nch2, FFN_CHUNK), (1, 0, 2))
    cw = jnp.concatenate([cw, z(nch2, SUBLANE - CONV_W, FFN_CHUNK)], axis=1)
    cb = conv_b.reshape(nch2, 1, FFN_CHUNK)
    wdn = w_down.astype(BF16)
    return w1, wq, wk, wv, w_uk, w_uv, wa2, wup, cw, cb, wdn


def _state_to_pairs(s):
    b = s.shape[0]
    s = s.reshape(b, PAIRS, 2, RW_HEAD_DIM, RW_HEAD_DIM)
    zz = jnp.zeros_like(s[:, :, 0])
    top = jnp.concatenate([s[:, :, 0], zz], axis=-1)
    bot = jnp.concatenate([zz, s[:, :, 1]], axis=-1)
    return jnp.concatenate([top, bot], axis=-2)


def _pairs_to_state(s):
    b = s.shape[0]
    h0 = s[:, :, :RW_HEAD_DIM, :RW_HEAD_DIM]
    h1 = s[:, :, RW_HEAD_DIM:, RW_HEAD_DIM:]
    return jnp.stack([h0, h1], axis=2).reshape(b, RW_HEADS, RW_HEAD_DIM, RW_HEAD_DIM)


def _pick(n, pref):
    for t in pref:
        if n % t == 0:
            return t
    return n


def kernel(x_prompt, x_sample, cache_mla_latent, cache_mla_krope, cache_mem_k, cache_mem_v, state_rwkv, state_rwkv_shift, state_ffn_conv, page_table, mem_prompt, g_mix, w_in, q_norm_g, kv_norm_g, w_uq, w_ukv, g_mla_out, rw_mu, rw_w0, rw_w2, rw_a0, rw_a2, rw_g2, rw_k_k, rw_k_a, rw_r_k, rw_lnx_g, rw_lnx_b, w_o, g_mem_q, g_mem_kv, w_mq, w_mk, w_mv, w_mo, g_ffn, w_up, conv_w, conv_b, w_down, g_final):
    depth = w_in.shape[0]
    assert depth == 1, "single-layer step"
    bp, tp, d = x_prompt.shape
    bs, ts, _ = x_sample.shape
    npages = page_table.shape[1]
    past_len = npages * PAGE_SIZE
    row2 = lambda a: a.reshape(1, -1)
    l = 0
    w1, wq, wk, wv, w_uk, w_uv, wa2, wup, cw, cb, wdn = _prep_weights(
        w_in[l], w_uq[l], w_ukv[l], rw_w2[l], rw_a2[l], w_up[l], conv_w[l], conv_b[l], w_down[l])
    wo_a, wo_b = w_o[l, :RW_WIDTH].astype(BF16), w_o[l, RW_WIDTH:].astype(BF16)
    wmq, wmo = w_mq[l].astype(BF16), w_mo[l].astype(BF16)
    wmkv = jnp.concatenate([w_mk[l], w_mv[l]], axis=1).astype(BF16)
    g2 = rw_g2[l].astype(BF16)
    nch2 = wup.shape[0]
    rw_vecs = (row2(rw_mu[l]), row2(rw_w0[l]), wa2, row2(rw_a0[l]), g2, row2(rw_k_k[l]), row2(rw_k_a[l]),
               row2(rw_r_k[l]), row2(rw_lnx_g[l]), row2(rw_lnx_b[l]))

    def after_attention(x2, orw, omla, mem_k, mem_v, conv_in, b, t, tm, tm_mem, tm_ffn, prev_rows):
        if prev_rows is None:
            xm = _memblock(x2.reshape(b, t, d), orw.reshape(b, t, -1), omla.reshape(b, t, -1), row2(g_mla_out[l]),
                           wo_a, wo_b, row2(g_mem_q[l]), wmq, mem_k, mem_v, wmo, tm=tm)
        else:
            x1 = _mix(x2, orw, omla, row2(g_mla_out[l]), wo_a, wo_b, tm=tm)
            qm = _norm_mm(x1, row2(g_mem_q[l]), wmq, BF16, tm=tm)
            om = _memattn(qm.reshape(b, t, d), mem_k, mem_v, tm=tm_mem)
            xm = _mm_res(om.reshape(b * t, d), wmo, x1, tm=tm)
        fb, ft = (b, t) if prev_rows is None else (1, b * t)
        return _ffn(xm.reshape(fb, ft, d), conv_in, row2(g_ffn[l]), wup, cw, cb, wdn, row2(g_final), tm=tm_ffn,
                    prev_rows=prev_rows)

    n_p = bp * tp
    tm_p = _pick(tp, (512, 256, 128, 64, 32, 16, 8))
    tabs_p = _rope_tables(np.arange(tp), MLA_SCALE * LOG2E)
    xp2 = x_prompt.reshape(n_p, d)
    q_p, lat_p, kr_p, prw_p, k_p, v_p = _inproj(
        xp2, row2(g_mix[l]), w1, row2(q_norm_g[l]), row2(kv_norm_g[l]), wq, tabs_p, wk, wv, tm=tm_p, with_kv=True, seq=tp)
    tq = FLASH_T
    omla_p = _flash(q_p, k_p, v_p, batch=bp, seq=tp, tq=tq)
    tb_p = _pick(tp, (512, 256, 128, 64, 32, 16, 8))
    if tb_p % QCH == 0:
        zeros_state = jnp.zeros((bp, RW_HEADS // QUAD, QW, QW), F32)
        orw_p, st_p = _rwkv_quad(prw_p.reshape(bp, tp, RW_PROJ), zeros_state, *rw_vecs, tb=tb_p)
        st_p = _quads_to_state(st_p)
    else:
        zeros_state = jnp.zeros((bp, PAIRS, LANE, LANE), F32)
        orw_p, st_p = _rwkv(prw_p.reshape(bp, tp, RW_PROJ), zeros_state, *rw_vecs, tb=tb_p,
                            ch=_pick(tb_p, (32, 16, 8)), row_lo=0, row_hi=tb_p)
        st_p = _pairs_to_state(st_p)
    mkv = _norm_mm(mem_prompt.reshape(-1, d), row2(g_mem_kv[l]), wmkv, F32, tm=_pick(mem_prompt.shape[0] * mem_prompt.shape[1], (512, 256, 128, 8)))
    n_mem = mem_prompt.shape[1]
    mk_p = mkv[:, :d].reshape(bp, n_mem, d)
    mv_p = mkv[:, d:].reshape(bp, n_mem, d)
    conv0_p = jnp.zeros((bp, nch2, SUBLANE, FFN_CHUNK), F32)
    y_p, u_p = after_attention(xp2, orw_p.reshape(n_p, RW_WIDTH), omla_p, mk_p, mv_p, conv0_p, bp, tp, tm_p, tm_p, tm_p, None)
    conv_p = jnp.transpose(u_p[:, :, SUBLANE - (CONV_W - 1):, :], (0, 2, 1, 3)).reshape(bp, CONV_W - 1, nch2 * FFN_CHUNK)

    n_s = bs * SROWS
    xs3 = jnp.pad(x_sample, ((0, 0), (SLO, SROWS - SLO - ts), (0, 0)))
    pos_s = np.tile(np.concatenate([np.zeros(SLO), past_len + np.arange(ts), np.zeros(SROWS - SLO - ts)]), bs)
    tm_s = _pick(n_s, (1024, 512, 256, 128, 64, 32, 16, 8))
    tabs_s = _rope_tables(pos_s[:tm_s], MLA_SCALE)
    q_s, lat_s, kr_s, prw_s = _inproj(
        xs3.reshape(n_s, d), row2(g_mix[l]), w1, row2(q_norm_g[l]), row2(kv_norm_g[l]), wq, tabs_s, None, None,
        tm=tm_s, with_kv=False)
    qabs = _bmm_cols(q_s, w_uk, BF16)
    qabs = qabs.reshape(MLA_HEADS, bs, SROWS, KV_LORA)[:, :, SLO:SLO + ts]
    qabs = jnp.transpose(qabs, (1, 0, 2, 3)).reshape(bs, MLA_HEADS * ts, KV_LORA)
    qrope = q_s.reshape(bs, SROWS, MLA_HEADS, HEAD_PAD)[:, SLO:SLO + ts, :, ROPE_LO:ROPE_LO + QK_ROPE]
    qrope = jnp.transpose(qrope, (0, 2, 1, 3)).reshape(bs, MLA_HEADS * ts, QK_ROPE)
    olat = _paged(page_table, qabs, qrope, lat_s.reshape(bs, SROWS, KV_LORA), kr_s.reshape(bs, SROWS, QK_ROPE),
                  cache_mla_latent[l], jnp.swapaxes(cache_mla_krope[l], 1, 2), n_new=ts)
    olat = jnp.transpose(olat.reshape(bs, MLA_HEADS, ts, KV_LORA), (1, 0, 2, 3))
    olat = jnp.pad(olat, ((0, 0), (0, 0), (SLO, SROWS - SLO - ts), (0, 0))).reshape(MLA_HEADS, n_s, KV_LORA)
    omla_s = _bmm(olat, w_uv, F32)
    omla_s = jnp.transpose(omla_s, (1, 0, 2)).reshape(n_s, MLA_HEADS * V_DIM)
    prw_s3 = prw_s.reshape(bs, SROWS, RW_PROJ).at[:, SLO - 1, :].set(state_rwkv_shift[l])
    orw_s, st_s = _rwkv(prw_s3, _state_to_pairs(state_rwkv[l]), *rw_vecs, tb=SROWS, ch=SROWS, row_lo=SLO, row_hi=SLO + ts)
    hist = jnp.transpose(state_ffn_conv[l].reshape(bs, CONV_W - 1, nch2, FFN_CHUNK), (2, 0, 1, 3))
    hist = jnp.pad(hist, ((0, 0), (0, 0), (SLO - (CONV_W - 1), SROWS - SLO), (0, 0))).reshape(nch2, n_s, FFN_CHUNK)
    mk_s = cache_mem_k[l].reshape(bs, cache_mem_k.shape[2], d)
    mv_s = cache_mem_v[l].reshape(bs, cache_mem_v.shape[2], d)
    y_s, u_s = after_attention(xs3.reshape(n_s, d), orw_s.reshape(n_s, RW_WIDTH), omla_s, mk_s, mv_s, hist,
                               bs, SROWS, tm_s, SROWS, _pick(n_s, (256, 128, 64, 32, 16, 8)), SLO)
    u_s = u_s.reshape(nch2, bs, SROWS, FFN_CHUNK)[:, :, SLO + ts - (CONV_W - 1):SLO + ts]
    conv_s = jnp.transpose(u_s, (1, 2, 0, 3)).reshape(bs, CONV_W - 1, nch2 * FFN_CHUNK)

    real = lambda a, w: a.reshape(bs, SROWS, w)[:, SLO:SLO + ts]
    mem5 = lambda a: a.reshape(1, bp, n_mem, MEM_HEADS, d // MEM_HEADS)
    return (y_p, real(y_s, d),
            lat_p.reshape(1, bp, tp, KV_LORA), kr_p.reshape(1, bp, tp, QK_ROPE), mem5(mk_p), mem5(mv_p),
            st_p[None], prw_p.reshape(bp, tp, RW_PROJ)[:, -1][None], conv_p[None],
            real(lat_s, KV_LORA)[None], real(kr_s, QK_ROPE)[None], _pairs_to_state(st_s)[None],
            prw_s.reshape(bs, SROWS, RW_PROJ)[:, SLO + ts - 1][None], conv_s[None])
```

```python
import functools

import numpy as np
import jax
import jax.numpy as jnp
from jax import lax
from jax.experimental import pallas as pl
from jax.experimental.pallas import tpu as pltpu

F32 = jnp.float32
BF16 = jnp.bfloat16

RW_HEADS = 8
RW_HEAD_DIM = 64
RW_WIDTH = 512
DECAY_LORA = 64
A_LORA = 64
GATE_LORA = 128
RW_PROJ = 3 * RW_WIDTH + DECAY_LORA + A_LORA + GATE_LORA
LNX_EPS = 64e-5
MLA_HEADS = 8
QK_NOPE = 64
QK_ROPE = 32
V_DIM = 64
Q_LORA = 384
KV_LORA = 256
MLA_PROJ = Q_LORA + KV_LORA + QK_ROPE
MLA_SCALE = (QK_NOPE + QK_ROPE) ** -0.5
ROPE_THETA = 10000.0
MEM_HEADS = 4
CONV_W = 3
NORM_EPS = 1e-6
PAGE_SIZE = 128

LANE = 128
SUBLANE = 8
VMEM_LIMIT = 56 * 1024 * 1024
HEAD_PAD = LANE
ROPE_LO = QK_NOPE
SROWS = 8
SLO = 2
NEG = -1e30
LOG2E = float(np.log2(np.e))
HI = lax.Precision.HIGHEST


def _cparams(*sem):
    return pltpu.CompilerParams(dimension_semantics=sem, vmem_limit_bytes=VMEM_LIMIT)


def _rms(x, g, eps=NORM_EPS):
    return x * lax.rsqrt(jnp.mean(x * x, axis=-1, keepdims=True) + eps) * g


def _dot(a, b):
    return jnp.dot(a, b, preferred_element_type=F32)


def _dot_nt(a, b, precision=None):
    return lax.dot_general(a, b, (((1,), (1,)), ((), ())), preferred_element_type=F32, precision=precision)


def _dot_tn(a, b, precision=None):
    return lax.dot_general(a, b, (((0,), (0,)), ((), ())), preferred_element_type=F32, precision=precision)


def _doth(a, b):
    return jnp.dot(a, b, preferred_element_type=F32, precision=HI)


C_Q0, C_KV0, C_KR0, C_KRS0, C_RW0 = 0, Q_LORA, Q_LORA + KV_LORA, Q_LORA + KV_LORA + LANE, Q_LORA + KV_LORA + 2 * LANE
W1_COLS = C_RW0 + RW_PROJ


def _inproj_kernel(x_ref, g_ref, w1_ref, qg_ref, kvg_ref, wq_ref, cq_ref, sq_ref, ck_ref, sk_ref, *rest, with_kv):
    if with_kv:
        wk_ref, wv_ref, q_out, lat_out, kr_out, prw_out, k_out, v_out = rest
    else:
        q_out, lat_out, kr_out, prw_out = rest
    h = _rms(x_ref[...], g_ref[...]).astype(BF16)
    prw_out[...] = _dot(h, w1_ref[:, C_RW0:W1_COLS])
    pm = _dot(h, w1_ref[:, 0:C_RW0])
    cqn = _rms(pm[:, C_Q0:C_KV0], qg_ref[...]).astype(BF16)
    lat = _rms(pm[:, C_KV0:C_KR0], kvg_ref[...])
    lat_out[...] = lat
    krope = pm[:, C_KR0:C_KRS0] * ck_ref[...] + pm[:, C_KRS0:C_RW0] * sk_ref[...]
    kr_out[...] = krope[:, ROPE_LO:ROPE_LO + QK_ROPE]
    q12 = _dot(cqn, wq_ref[...])
    nq = MLA_HEADS * HEAD_PAD
    cq, sq = cq_ref[...], sq_ref[...]
    for hd in range(MLA_HEADS):
        a, b = hd * HEAD_PAD, (hd + 1) * HEAD_PAD
        q_out[:, a:b] = (q12[:, a:b] * cq + q12[:, nq + a:nq + b] * sq).astype(BF16)
    if with_kv:
        latb = lat.astype(BF16)
        kn = _dot(latb, wk_ref[...])
        for hd in range(MLA_HEADS):
            a, b = hd * HEAD_PAD, (hd + 1) * HEAD_PAD
            k_out[:, a:b] = (kn[:, a:b] + krope).astype(BF16)
        vt = _dot_nt(wv_ref[...], latb).astype(BF16)
        for jb in range(v_out.shape[1]):
            v_out[0, jb] = vt[:, jb * FLASH_T:(jb + 1) * FLASH_T]


def _inproj(x, g, w1, qg, kvg, wq, tabs, wk, wv, *, tm, with_kv, seq=None):
    n, d = x.shape
    ttab = tabs[0].shape[0]
    nt = ttab // tm
    row = lambda i: (i, 0)
    const = lambda i: (0, 0)
    tab = lambda i: (i % nt, 0)
    in_specs = [pl.BlockSpec((tm, d), row), pl.BlockSpec(g.shape, const), pl.BlockSpec(w1.shape, const),
                pl.BlockSpec(qg.shape, const), pl.BlockSpec(kvg.shape, const), pl.BlockSpec(wq.shape, const)]
    in_specs += [pl.BlockSpec((tm, LANE), tab)] * 4
    args = [x, g, w1, qg, kvg, wq, *tabs]
    nq = MLA_HEADS * HEAD_PAD
    out_shape = [jax.ShapeDtypeStruct((n, nq), BF16), jax.ShapeDtypeStruct((n, KV_LORA), F32),
                 jax.ShapeDtypeStruct((n, QK_ROPE), F32), jax.ShapeDtypeStruct((n, RW_PROJ), F32)]
    out_specs = [pl.BlockSpec((tm, nq), row), pl.BlockSpec((tm, KV_LORA), row),
                 pl.BlockSpec((tm, QK_ROPE), row), pl.BlockSpec((tm, RW_PROJ), row)]
    if with_kv:
        in_specs += [pl.BlockSpec(wk.shape, const), pl.BlockSpec(wv.shape, const)]
        args += [wk, wv]
        nv = MLA_HEADS * V_DIM
        npb = seq // tm
        out_shape += [jax.ShapeDtypeStruct((n, nq), BF16), jax.ShapeDtypeStruct((n // seq, seq // FLASH_T, nv, FLASH_T), BF16)]
        out_specs += [pl.BlockSpec((tm, nq), row),
                      pl.BlockSpec((1, tm // FLASH_T, nv, FLASH_T), lambda i: (i // npb, i % npb, 0, 0))]
    return pl.pallas_call(
        functools.partial(_inproj_kernel, with_kv=with_kv), grid=(n // tm,), in_specs=in_specs,
        out_specs=out_specs, out_shape=out_shape, compiler_params=_cparams("parallel"),
        name="inproj_kv" if with_kv else "inproj")(*args)


FLASH_HPB = 4
FLASH_T = 256
FLASH_LROWS = 16


def _flash_kernel(q_ref, k_ref, vt_ref, o_ref, st_a, st_b, m_s, acc_s, *, tq):
    tk = FLASH_T
    qi = pl.program_id(2)
    assert tq == tk
    nfull = qi
    qs = [q_ref[:, h * HEAD_PAD:(h + 1) * HEAD_PAD] for h in range(FLASH_HPB)]
    kidx = lax.broadcasted_iota(jnp.int32, (tk, tq), 0)
    qidx = lax.broadcasted_iota(jnp.int32, (tk, tq), 1)

    ones = jnp.ones((FLASH_LROWS, tk), BF16)

    def scores(slot, j):
        start = pl.multiple_of(j * tk, tk)
        for h in range(FLASH_HPB):
            slot[h] = _dot_nt(k_ref[pl.ds(start, tk), h * HEAD_PAD:(h + 1) * HEAD_PAD], qs[h])

    def step(slot, j, diagonal):
        for h in range(FLASH_HPB):
            m = m_s[h]
            st = slot[h]
            if diagonal:
                st = jnp.where(kidx <= qidx, st, NEG)
            m_new = jnp.maximum(m, jnp.max(st, axis=0, keepdims=True))
            alpha = jnp.exp2(m - m_new)
            p = jnp.exp2(st - m_new).astype(BF16)
            vt1 = jnp.concatenate([vt_ref[0, j, h * V_DIM:(h + 1) * V_DIM, :], ones], axis=0)
            acc_s[h] = alpha * acc_s[h] + _dot(vt1, p)
            m_s[h] = m_new

    m_s[...] = jnp.full(m_s.shape, NEG, F32)
    acc_s[...] = jnp.zeros(acc_s.shape, F32)

    scores(st_a, 0)

    def pair(jj, _):
        scores(st_b, 2 * jj + 1)
        step(st_a, 2 * jj, False)
        scores(st_a, 2 * jj + 2)
        step(st_b, 2 * jj + 1, False)
        return 0

    npair = nfull // 2
    lax.fori_loop(0, npair, pair, 0)

    @pl.when(nfull % 2 == 1)
    def _():
        scores(st_b, nfull)
        step(st_a, nfull - 1, False)
        step(st_b, nfull, True)

    @pl.when(nfull % 2 == 0)
    def _():
        step(st_a, nfull, True)

    ot = jnp.concatenate([acc_s[h, :V_DIM] / acc_s[h, V_DIM:V_DIM + 1] for h in range(FLASH_HPB)], axis=0)
    o_ref[...] = ot.T


def _flash(q, k, vt, *, batch, seq, tq):
    nqb = seq // tq
    grid = (batch, MLA_HEADS // FLASH_HPB, nqb)
    w = FLASH_HPB * HEAD_PAD
    return pl.pallas_call(
        functools.partial(_flash_kernel, tq=tq), grid=grid,
        in_specs=[pl.BlockSpec((tq, w), lambda b, h, i: (b * nqb + i, h)),
                  pl.BlockSpec((seq, w), lambda b, h, i: (b, h)),
                  pl.BlockSpec((1, seq // FLASH_T, FLASH_HPB * V_DIM, FLASH_T), lambda b, h, i: (b, 0, h, 0))],
        out_specs=pl.BlockSpec((tq, FLASH_HPB * V_DIM), lambda b, h, i: (b * nqb + i, h)),
        out_shape=jax.ShapeDtypeStruct((batch * seq, MLA_HEADS * V_DIM), F32),
        scratch_shapes=[pltpu.VMEM((FLASH_HPB, FLASH_T, tq), F32), pltpu.VMEM((FLASH_HPB, FLASH_T, tq), F32),
                        pltpu.VMEM((FLASH_HPB, 1, tq), F32), pltpu.VMEM((FLASH_HPB, V_DIM + FLASH_LROWS, tq), F32)],
        compiler_params=_cparams("parallel", "parallel", "arbitrary"), name="mla_flash")(q, k, vt)


PAGES_PER_STEP = 32
PAGE_GROUP = 4
PAGE_STREAMS = 4


def _paged_kernel(pt_ref, qa_ref, qr_ref, latn_ref, krn_ref, lat_hbm, kr_hbm, o_ref,
                  lat_buf, kr_buf, sem, m_s, l_s, acc_s, *, n_new, g, group, streams):
    seq, step = pl.program_id(0), pl.program_id(1)
    nsteps = pl.num_programs(1)
    lin = seq * nsteps + step
    slot = lin % 2

    def page_copy(page, sl, j):
        return (pltpu.make_async_copy(lat_hbm.at[page], lat_buf.at[sl, j], sem.at[sl, 0]),
                pltpu.make_async_copy(kr_hbm.at[page], kr_buf.at[sl, j], sem.at[sl, 1]))

    def start_step(sq, st, sl):
        for j in range(g):
            for c in page_copy(pt_ref[sq, st * g + j], sl, j):
                c.start()

    @pl.when(lin == 0)
    def _():
        start_step(0, 0, 0)

    @pl.when(lin + 1 < pl.num_programs(0) * nsteps)
    def _():
        wrap = step + 1 == nsteps
        start_step(jnp.where(wrap, seq + 1, seq), jnp.where(wrap, 0, step + 1), 1 - slot)

    for j in range(g):
        for c in page_copy(0, slot, j):
            c.wait()

    lat_refs = [lat_buf.at[slot, j] for j in range(g)]
    kr_refs = [kr_buf.at[slot, j] for j in range(g)]

    @pl.when(step == 0)
    def _():
        m_s[...] = jnp.full(m_s.shape, NEG, F32)
        l_s[...] = jnp.zeros(l_s.shape, F32)
        acc_s[...] = jnp.zeros(acc_s.shape, F32)

    qa = qa_ref[0]
    qr = qr_ref[0]

    def update(carry, s, vb):
        m, l, acc = carry
        m_new = jnp.maximum(m, jnp.max(s, axis=-1, keepdims=True))
        alpha = jnp.exp(m - m_new)
        p = jnp.exp(s - m_new)
        l = alpha * l + jnp.sum(p, axis=-1, keepdims=True)
        acc = alpha * acc + _dot(p.astype(BF16), vb)
        return m_new, l, acc

    def scores(grp):
        pages = range(grp * group, (grp + 1) * group)
        latb = jnp.concatenate([lat_refs[j][...].astype(BF16) for j in pages], axis=0)
        krt = jnp.concatenate([kr_refs[j][...].astype(BF16) for j in pages], axis=1)
        return _dot_nt(qa, latb) + _dot(qr, krt), latb

    carries = [(m_s[t], l_s[t], acc_s[t]) for t in range(streams)]
    ngrp = g // group
    ahead = streams
    pend = [scores(i) for i in range(min(ahead, ngrp))]
    for grp in range(ngrp):
        if grp + ahead < ngrp:
            pend.append(scores(grp + ahead))
        t = grp % streams
        carries[t] = update(carries[t], *pend[grp])
        pend[grp] = None
    for t in range(streams):
        m_s[t], l_s[t], acc_s[t] = carries[t]

    @pl.when(step == pl.num_programs(1) - 1)
    def _():
        latn = latn_ref[0].astype(BF16)
        krn = krn_ref[0].astype(BF16)
        s = _dot_nt(qa, latn) + _dot_nt(qr, krn)
        qt = lax.broadcasted_iota(jnp.int32, s.shape, 0) % n_new
        kt = lax.broadcasted_iota(jnp.int32, s.shape, 1) - SLO
        s = jnp.where((kt >= 0) & (kt <= qt), s, NEG)
        m, l, acc = update((m_s[0], l_s[0], acc_s[0]), s, latn)
        for t in range(1, streams):
            mt = m_s[t]
            m_new = jnp.maximum(m, mt)
            a, bt = jnp.exp(m - m_new), jnp.exp(mt - m_new)
            l = a * l + bt * l_s[t]
            acc = a * acc + bt * acc_s[t]
            m = m_new
        o_ref[0] = acc / l


def _paged(page_table, qa, qr, lat_new, kr_new, cache_lat, cache_kr, *, n_new):
    b, npages = page_table.shape
    g = max(d for d in range(1, PAGES_PER_STEP + 1) if npages % d == 0)
    group = max(d for d in range(1, PAGE_GROUP + 1) if g % d == 0)
    streams = min(PAGE_STREAMS, g // group)
    rows = qa.shape[1]
    seq3 = lambda i, s, pt: (i, 0, 0)
    grid_spec = pltpu.PrefetchScalarGridSpec(
        num_scalar_prefetch=1, grid=(b, npages // g),
        in_specs=[pl.BlockSpec((1, rows, KV_LORA), seq3), pl.BlockSpec((1, rows, QK_ROPE), seq3),
                  pl.BlockSpec((1, SROWS, KV_LORA), seq3), pl.BlockSpec((1, SROWS, QK_ROPE), seq3),
                  pl.BlockSpec(memory_space=pl.ANY), pl.BlockSpec(memory_space=pl.ANY)],
        out_specs=pl.BlockSpec((1, rows, KV_LORA), seq3),
        scratch_shapes=[pltpu.VMEM((2, g, PAGE_SIZE, KV_LORA), F32), pltpu.VMEM((2, g, QK_ROPE, PAGE_SIZE), F32),
                        pltpu.SemaphoreType.DMA((2, 2)),
                        pltpu.VMEM((streams, rows, 1), F32), pltpu.VMEM((streams, rows, 1), F32),
                        pltpu.VMEM((streams, rows, KV_LORA), F32)])
    return pl.pallas_call(
        functools.partial(_paged_kernel, n_new=n_new, g=g, group=group, streams=streams), grid_spec=grid_spec,
        out_shape=jax.ShapeDtypeStruct((b, rows, KV_LORA), F32),
        compiler_params=_cparams("arbitrary", "arbitrary"), name="mla_paged")(
            page_table, qa, qr, lat_new, kr_new, cache_lat, cache_kr)


def _bmm_kernel(a_ref, w_ref, o_ref):
    o_ref[0] = _dot(a_ref[...].astype(BF16), w_ref[0]).astype(o_ref.dtype)


def _bmm_cols(a, w, out_dtype):
    n = a.shape[0]
    hh, k, m = w.shape
    return pl.pallas_call(
        _bmm_kernel, grid=(hh,),
        in_specs=[pl.BlockSpec((n, k), lambda h: (0, h)), pl.BlockSpec((1, k, m), lambda h: (h, 0, 0))],
        out_specs=pl.BlockSpec((1, n, m), lambda h: (h, 0, 0)),
        out_shape=jax.ShapeDtypeStruct((hh, n, m), out_dtype), compiler_params=_cparams("parallel"),
        name="bmm_cols")(a, w)


def _bmm_kernel3(a_ref, w_ref, o_ref):
    o_ref[0] = _dot(a_ref[0].astype(BF16), w_ref[0]).astype(o_ref.dtype)


def _bmm(a, w, out_dtype):
    hh, n, k = a.shape
    m = w.shape[2]
    return pl.pallas_call(
        _bmm_kernel3, grid=(hh,),
        in_specs=[pl.BlockSpec((1, n, k), lambda h: (h, 0, 0)), pl.BlockSpec((1, k, m), lambda h: (h, 0, 0))],
        out_specs=pl.BlockSpec((1, n, m), lambda h: (h, 0, 0)),
        out_shape=jax.ShapeDtypeStruct((hh, n, m), out_dtype), compiler_params=_cparams("parallel"),
        name="bmm")(a, w)


PAIRS = RW_HEADS // 2
C_R, C_K, C_V, C_WA, C_G = 0, RW_WIDTH, 2 * RW_WIDTH, 3 * RW_WIDTH, 3 * RW_WIDTH + DECAY_LORA + A_LORA


def _seg_sum(x):
    lane = lax.broadcasted_iota(jnp.int32, (x.shape[0], LANE), 1)
    low = lane < RW_HEAD_DIM
    outs = []
    for t in range(x.shape[1] // LANE):
        xt = x[:, t * LANE:(t + 1) * LANE]
        s0 = jnp.sum(jnp.where(low, xt, 0.0), axis=-1, keepdims=True)
        s1 = jnp.sum(jnp.where(low, 0.0, xt), axis=-1, keepdims=True)
        outs.append(jnp.where(low, s0, s1))
    return outs[0] if len(outs) == 1 else jnp.concatenate(outs, axis=-1)


def _rwkv_prologue(p_ref, s0_ref, so_ref, vec_refs, scratch, *, tb, row_lo, row_hi):
    mu_ref, w0_ref, wa2_ref, a0_ref, g2_ref, kk_ref, ka_ref, rk_ref = vec_refs
    carry_s, r_s, k_s, v_s, kk_s, b_s, ld_s = scratch
    i = pl.program_id(1)

    @pl.when(i == 0)
    def _():
        carry_s[...] = jnp.zeros(carry_s.shape, F32)
        so_ref[0] = s0_ref[0]

    p = p_ref[0]
    rows1 = lax.broadcasted_iota(jnp.int32, (tb, 1), 0)
    prev = jnp.where(rows1 == 0, carry_s[SUBLANE - 1:SUBLANE, :], pltpu.roll(p, 1, axis=0))
    carry_s[...] = p[tb - SUBLANE:tb, :]
    s = p + (prev - p) * mu_ref[...]
    r = s[:, C_R:C_K]
    k = s[:, C_K:C_V]
    v = s[:, C_V:C_WA]
    wa = s[:, C_WA:C_G]
    lane = lax.broadcasted_iota(jnp.int32, wa.shape, 1)
    z = jnp.where(lane < DECAY_LORA, jnp.tanh(wa), wa).astype(BF16)
    lin = _dot(z, wa2_ref[...])
    xw = -(w0_ref[...] + lin[:, :RW_WIDTH])
    w = -(jnp.maximum(xw, 0.0) + jnp.log(1.0 + jnp.exp(-jnp.abs(xw)))) - 0.5
    logd = -jnp.exp(w)
    a = jax.nn.sigmoid(a0_ref[...] + lin[:, RW_WIDTH:])
    gate = _dot(jax.nn.sigmoid(s[:, C_G:RW_PROJ]).astype(BF16), g2_ref[...])
    kk = k * kk_ref[...]
    kk = kk / jnp.maximum(jnp.sqrt(_seg_sum(kk * kk)), 1e-12)
    k = k * (1.0 + (a - 1.0) * ka_ref[...])
    bonus = _seg_sum(r * k * rk_ref[...]) * v
    if row_lo > 0 or row_hi < tb:
        live = (rows1 >= row_lo) & (rows1 < row_hi)
        logd = jnp.where(live, logd, 0.0)
        kk = jnp.where(live, kk, 0.0)
        k = jnp.where(live, k, 0.0)
        v = jnp.where(live, v, 0.0)
    r_s[...] = r
    k_s[...] = k
    v_s[...] = v
    kk_s[...] = kk
    b_s[...] = kk * a
    ld_s[...] = logd
    return gate, bonus


def _rwkv_epilogue(out, gate, bonus, lg_ref, lb_ref, o_ref):
    mean = _seg_sum(out) * (1.0 / RW_HEAD_DIM)
    cen = out - mean
    var = _seg_sum(cen * cen) * (1.0 / RW_HEAD_DIM)
    y = cen * lax.rsqrt(var + LNX_EPS) * lg_ref[...] + lb_ref[...]
    o_ref[0] = ((y + bonus) * gate).astype(o_ref.dtype)


def _rwkv_kernel(p_ref, s0_ref, mu_ref, w0_ref, wa2_ref, a0_ref, g2_ref, kk_ref, ka_ref, rk_ref, lg_ref, lb_ref,
                 o_ref, so_ref, carry_s, r_s, k_s, v_s, kk_s, b_s, ld_s, out_s, *, tb, ch, row_lo, row_hi):
    gate, bonus = _rwkv_prologue(p_ref, s0_ref, so_ref, (mu_ref, w0_ref, wa2_ref, a0_ref, g2_ref, kk_ref, ka_ref, rk_ref),
                                 (carry_s, r_s, k_s, v_s, kk_s, b_s, ld_s), tb=tb, row_lo=row_lo, row_hi=row_hi)
    c2 = 2 * ch
    ri = lax.broadcasted_iota(jnp.int32, (c2, c2), 0)
    ci = lax.broadcasted_iota(jnp.int32, (c2, c2), 1)
    strict = (ci % ch) < (ri % ch)
    incl = (ci % ch) <= (ri % ch)
    eye = (ri == ci).astype(F32)
    tri = (lax.broadcasted_iota(jnp.int32, (ch, ch), 1) <= lax.broadcasted_iota(jnp.int32, (ch, ch), 0)).astype(F32)
    lane2 = lax.broadcasted_iota(jnp.int32, (c2, LANE), 1)
    row2 = lax.broadcasted_iota(jnp.int32, (c2, LANE), 0)
    own = (lane2 < RW_HEAD_DIM) == (row2 < ch)

    def stack(x):
        return jnp.where(own, jnp.concatenate([x, x], axis=0), 0.0)

    bf = lambda x: x.astype(BF16)
    pairs = [slice(pr * LANE, (pr + 1) * LANE) for pr in range(PAIRS)]

    def chunk(c, _):
        ts = pl.ds(pl.multiple_of(c * ch, ch), ch)
        css = [_doth(tri, ld_s[ts, ls]) for ls in pairs]
        lhss, amats, tails = [], [], []
        for ls, cs in zip(pairs, css):
            ld = ld_s[ts, ls]
            e_inv = jnp.exp(-cs)
            lhs = bf(jnp.concatenate([stack(kk_s[ts, ls] * jnp.exp(cs - ld)), stack(r_s[ts, ls] * jnp.exp(cs))], axis=0))
            rhs = bf(jnp.concatenate([stack(k_s[ts, ls] * e_inv), stack(b_s[ts, ls] * e_inv)], axis=0))
            lhss.append(lhs)
            amats.append(_dot_nt(lhs, rhs))
        sts = [so_ref[0, pr] for pr in range(PAIRS)]
        gmats = [_dot_nt(lhs, bf(st)) for lhs, st in zip(lhss, sts)]
        vss = [bf(stack(v_s[ts, ls])) for ls in pairs]
        akbs = [jnp.where(strict, amat[:c2, c2:], 0.0) for amat in amats]
        ykk = [_dot(bf(jnp.where(strict, amat[:c2, :c2], 0.0)), vs) for amat, vs in zip(amats, vss)]
        yrk = [_dot(bf(jnp.where(incl, amat[c2:, :c2], 0.0)), vs) for amat, vs in zip(amats, vss)]
        invs = [eye - jnp.where((ri // 2 == ci // 2), a_kb, 0.0) for a_kb in akbs]
        m = 2
        while m < ch:
            lvl = (ri // (2 * m) == ci // (2 * m)) & ((ri // m) % 2 == 1) & ((ci // m) % 2 == 0)
            tmps = [_dot(bf(inv), bf(jnp.where(lvl, a_kb, 0.0))) for inv, a_kb in zip(invs, akbs)]
            invs = [inv - _dot(bf(tmp), bf(inv)) for inv, tmp in zip(invs, tmps)]
            m *= 2
        us = [_dot(bf(inv), bf(gmat[:c2] + y)) for inv, gmat, y in zip(invs, gmats, ykk)]
        o2s = [gmat[c2:] + y - _dot(bf(jnp.where(incl, amat[c2:, c2:], 0.0)), bf(u))
               for gmat, y, amat, u in zip(gmats, yrk, amats, us)]
        for pr, (ls, cs, st, vs, u, o2) in enumerate(zip(pairs, css, sts, vss, us, o2s)):
            tot = cs[ch - 1:ch, :]
            e_tail = jnp.exp(tot - cs)
            out_s[ts, ls] = o2[:ch] + o2[ch:]
            so_ref[0, pr] = (st * jnp.exp(tot) + _dot_tn(vs, bf(stack(k_s[ts, ls] * e_tail)))
                             - _dot_tn(bf(u), bf(stack(b_s[ts, ls] * e_tail))))
        return 0

    lax.fori_loop(0, tb // ch, chunk, 0)
    _rwkv_epilogue(out_s[...], gate, bonus, lg_ref, lb_ref, o_ref)


def _rwkv(p, s0, mu, w0, wa2, a0, g2, k_k, k_a, r_k, lnx_g, lnx_b, *, tb, ch, row_lo, row_hi):
    b, t, _ = p.shape
    const = lambda bi, i: (0, 0)
    vec = lambda a: pl.BlockSpec(a.shape, const)
    scr = lambda: pltpu.VMEM((tb, RW_WIDTH), F32)
    return pl.pallas_call(
        functools.partial(_rwkv_kernel, tb=tb, ch=ch, row_lo=row_lo, row_hi=row_hi), grid=(b, t // tb),
        in_specs=[pl.BlockSpec((1, tb, RW_PROJ), lambda bi, i: (bi, i, 0)),
                  pl.BlockSpec((1, PAIRS, LANE, LANE), lambda bi, i: (bi, 0, 0, 0)),
                  vec(mu), vec(w0), vec(wa2), vec(a0), vec(g2), vec(k_k), vec(k_a), vec(r_k), vec(lnx_g), vec(lnx_b)],
        out_specs=[pl.BlockSpec((1, tb, RW_WIDTH), lambda bi, i: (bi, i, 0)),
                   pl.BlockSpec((1, PAIRS, LANE, LANE), lambda bi, i: (bi, 0, 0, 0))],
        out_shape=[jax.ShapeDtypeStruct((b, t, RW_WIDTH), BF16), jax.ShapeDtypeStruct((b, PAIRS, LANE, LANE), F32)],
        scratch_shapes=[pltpu.VMEM((SUBLANE, RW_PROJ), F32)] + [scr() for _ in range(7)],
        compiler_params=_cparams("parallel", "arbitrary"), name="rwkv")(
            p, s0, mu, w0, wa2, a0, g2, k_k, k_a, r_k, lnx_g, lnx_b)


QUAD = 4
QCH = RW_HEAD_DIM
QW = QUAD * RW_HEAD_DIM
QLEVELS = (2, 4, 8, 16, 32)
PREP_CHUNKS = 4


def _tile4(x):
    return jnp.concatenate([x] * QUAD, axis=0)


def _rwkv_quad_kernel(p_ref, s0_ref, mu_ref, w0_ref, wa2_ref, a0_ref, g2_ref, kk_ref, ka_ref, rk_ref, lg_ref, lb_ref,
                      tri_ref, bdm_ref, own_ref, sbm_ref,
                      o_ref, so_ref, carry_s, r_s, k_s, v_s, kk_s, b_s, ld_s, out_s,
                      lhs_c, arb_c, y_c, kt_c, dec_c, *, tb):
    gate, bonus = _rwkv_prologue(p_ref, s0_ref, so_ref, (mu_ref, w0_ref, wa2_ref, a0_ref, g2_ref, kk_ref, ka_ref, rk_ref),
                                 (carry_s, r_s, k_s, v_s, kk_s, b_s, ld_s), tb=tb, row_lo=0, row_hi=tb)
    ch = QCH
    nquad = RW_HEADS // QUAD
    bf = lambda x: x.astype(BF16)

    def bd(x):
        return _tile4(bf(x)) * bdm_ref[0]

    quads = [slice(q * QW, (q + 1) * QW) for q in range(nquad)]

    def prepare(chunks):
        inst = [(c, q, pl.ds(pl.multiple_of(c * ch, ch), ch), quads[q]) for c in chunks for q in range(nquad)]
        strict, incl, eye, lvl1 = sbm_ref[0] > 0, sbm_ref[1] > 0, sbm_ref[2], sbm_ref[3]
        css = []
        for c, q, ts, ls in inst:
            ld = ld_s[ts, ls]
            l1 = bf(ld)
            r1 = ld - l1.astype(F32)
            l2 = bf(r1)
            l3 = bf(r1 - l2.astype(F32))
            css.append(_dot(tri_ref[...], jnp.concatenate([l1, l2, l3], axis=0)))
        a1s, a2s = [], []
        for (c, q, ts, ls), cs in zip(inst, css):
            ld = ld_s[ts, ls]
            tot = cs[ch - 1:ch, :]
            e_inv = jnp.exp(-cs)
            kc, bc = k_s[ts, ls], b_s[ts, ls]
            lhs = bf(jnp.concatenate([kk_s[ts, ls] * jnp.exp(cs - ld), r_s[ts, ls] * jnp.exp(cs)], axis=0))
            lhs_c[c, q] = lhs
            e_tail = jnp.exp(tot - cs)
            kt_c[c, q] = bf(jnp.concatenate([kc * e_tail, bc * e_tail], axis=0))
            dec_c[c, q] = jnp.broadcast_to(jnp.exp(tot), (SUBLANE, QW))
            a1s.append(_dot_nt(lhs, bd(kc * e_inv)))
            a2s.append(_dot_nt(lhs, bd(bc * e_inv)))
        ys, akb4s, invs = [], [], []
        for (c, q, ts, ls), a1, a2 in zip(inst, a1s, a2s):
            a_kk = jnp.where(strict, a1[:ch], 0.0)
            a_rk = jnp.where(incl, a1[ch:], 0.0)
            a_kb = jnp.where(strict, a2[:ch], 0.0)
            arb_c[c, q] = bf(jnp.where(incl, a2[ch:], 0.0))
            ys.append(_dot(bf(jnp.concatenate([a_kk, a_rk], axis=0)), bd(v_s[ts, ls])))
            akb4s.append(_tile4(bf(a_kb)))
            invs.append(eye - a_kb * lvl1)
        for li in range(len(QLEVELS)):
            tmps = [_dot(bf(inv), akb4 * bdm_ref[1 + li]) for inv, akb4 in zip(invs, akb4s)]
            invs = [inv - _dot(bf(tmp), bd(inv)) for inv, tmp in zip(invs, tmps)]
        ws = [_dot(bf(inv), _tile4(lhs_c[c, q][:ch]) * bdm_ref[0]) for (c, q, ts, ls), inv in zip(inst, invs)]
        yks = [_dot(bf(inv), bd(y[:ch])) for inv, y in zip(invs, ys)]
        for (c, q, ts, ls), w, yk, y in zip(inst, ws, yks, ys):
            lhs_c[c, q, 0:ch, :] = bf(w)
            y_c[c, q] = jnp.concatenate([yk, y[ch:]], axis=0)

    nc = tb // ch
    if nc % PREP_CHUNKS == 0:
        def prepare_loop(cc, _):
            prepare([PREP_CHUNKS * cc + d for d in range(PREP_CHUNKS)])
            return 0
        lax.fori_loop(0, nc // PREP_CHUNKS, prepare_loop, 0)
    else:
        lax.fori_loop(0, nc, lambda c, _: (prepare([c]), 0)[1], 0)

    def recur(c, _):
        ts = pl.ds(pl.multiple_of(c * ch, ch), ch)
        sts = [so_ref[0, q] for q in range(nquad)]
        gys = [_dot_nt(lhs_c[c, q], bf(sts[q])) + y_c[c, q] for q in range(nquad)]
        us = [gys[q][:ch] for q in range(nquad)]
        outs = [gys[q][ch:] - _dot(arb_c[c, q], bd(us[q])) for q in range(nquad)]
        upds = [_dot_tn(bf(jnp.concatenate([v_s[ts, quads[q]], -us[q]], axis=0)), kt_c[c, q]) for q in range(nquad)]
        for q in range(nquad):
            out_s[ts, quads[q]] = outs[q]
            so_ref[0, q] = sts[q] * dec_c[c, q][0:1, :] + upds[q] * own_ref[...]
        return 0

    lax.fori_loop(0, nc, recur, 0)
    _rwkv_epilogue(out_s[...], gate, bonus, lg_ref, lb_ref, o_ref)


def _quad_masks():
    ch = QCH
    r = np.arange(QUAD * ch)[:, None]
    c = np.arange(QW)[None, :]
    own = (r // ch) == (c // RW_HEAD_DIM)
    sp, s = r % ch, c % ch
    bdm = [own]
    for m in QLEVELS:
        bdm.append(own & (sp // (2 * m) == s // (2 * m)) & ((sp // m) % 2 == 1) & ((s // m) % 2 == 0))
    t = np.arange(ch)[:, None]
    sbm = [s < t, s <= t, s == t, (t // 2 == s // 2) & (t % 2 == 1) & (s % 2 == 0)]
    tri = np.tile(np.arange(ch)[None, :] <= np.arange(ch)[:, None], (1, 3))
    return (jnp.asarray(tri, BF16), jnp.asarray(np.stack(bdm), BF16), jnp.asarray(own, F32),
            jnp.asarray(np.stack([np.broadcast_to(m, (ch, QW)) for m in sbm]), F32))


def _rwkv_quad(p, s0, mu, w0, wa2, a0, g2, k_k, k_a, r_k, lnx_g, lnx_b, *, tb):
    b, t, _ = p.shape
    nq = RW_HEADS // QUAD
    nc = tb // QCH
    masks = _quad_masks()
    vec = lambda a: pl.BlockSpec(a.shape, lambda bi, i: (0,) * a.ndim)
    scr = lambda: pltpu.VMEM((tb, RW_WIDTH), F32)
    vecs = (mu, w0, wa2, a0, g2, k_k, k_a, r_k, lnx_g, lnx_b) + masks
    return pl.pallas_call(
        functools.partial(_rwkv_quad_kernel, tb=tb), grid=(b, t // tb),
        in_specs=[pl.BlockSpec((1, tb, RW_PROJ), lambda bi, i: (bi, i, 0)),
                  pl.BlockSpec((1, nq, QW, QW), lambda bi, i: (bi, 0, 0, 0))] + [vec(a) for a in vecs],
        out_specs=[pl.BlockSpec((1, tb, RW_WIDTH), lambda bi, i: (bi, i, 0)),
                   pl.BlockSpec((1, nq, QW, QW), lambda bi, i: (bi, 0, 0, 0))],
        out_shape=[jax.ShapeDtypeStruct((b, t, RW_WIDTH), BF16), jax.ShapeDtypeStruct((b, nq, QW, QW), F32)],
        scratch_shapes=[pltpu.VMEM((SUBLANE, RW_PROJ), F32)] + [scr() for _ in range(7)] + [
            pltpu.VMEM((nc, nq, 2 * QCH, QW), BF16),
            pltpu.VMEM((nc, nq, QCH, QW), BF16), pltpu.VMEM((nc, nq, 2 * QCH, QW), F32),
            pltpu.VMEM((nc, nq, 2 * QCH, QW), BF16), pltpu.VMEM((nc, nq, SUBLANE, QW), F32)],
        compiler_params=_cparams("parallel", "arbitrary"), name="rwkv_quad")(p, s0, *vecs)


def _quads_to_state(sq):
    b = sq.shape[0]
    s6 = sq.reshape(b, RW_HEADS // QUAD, QUAD, RW_HEAD_DIM, QUAD, RW_HEAD_DIM)
    return jnp.stack([s6[:, :, h, :, h, :] for h in range(QUAD)], axis=2).reshape(b, RW_HEADS, RW_HEAD_DIM, RW_HEAD_DIM)


def _mix_kernel(x_ref, orw_ref, omla_ref, g_ref, wa_ref, wb_ref, o_ref):
    om = _rms(omla_ref[...], g_ref[...]).astype(BF16)
    o_ref[...] = x_ref[...] + _dot(orw_ref[...], wa_ref[...]) + _dot(om, wb_ref[...])


def _mix(x, orw, omla, g, wa, wb, *, tm):
    n, d = x.shape
    row = lambda i: (i, 0)
    const = lambda i: (0, 0)
    return pl.pallas_call(
        _mix_kernel, grid=(n // tm,),
        in_specs=[pl.BlockSpec((tm, d), row), pl.BlockSpec((tm, orw.shape[1]), row), pl.BlockSpec((tm, omla.shape[1]), row),
                  pl.BlockSpec(g.shape, const), pl.BlockSpec(wa.shape, const), pl.BlockSpec(wb.shape, const)],
        out_specs=pl.BlockSpec((tm, d), row), out_shape=jax.ShapeDtypeStruct((n, d), F32),
        compiler_params=_cparams("parallel"), name="mix_out")(x, orw, omla, g, wa, wb)


def _norm_mm_kernel(x_ref, g_ref, w_ref, o_ref):
    o_ref[...] = _dot(_rms(x_ref[...], g_ref[...]).astype(BF16), w_ref[...]).astype(o_ref.dtype)


def _norm_mm(x, g, w, out_dtype, *, tm):
    n, d = x.shape
    m = w.shape[1]
    return pl.pallas_call(
        _norm_mm_kernel, grid=(n // tm,),
        in_specs=[pl.BlockSpec((tm, d), lambda i: (i, 0)), pl.BlockSpec(g.shape, lambda i: (0, 0)),
                  pl.BlockSpec(w.shape, lambda i: (0, 0))],
        out_specs=pl.BlockSpec((tm, m), lambda i: (i, 0)), out_shape=jax.ShapeDtypeStruct((n, m), out_dtype),
        compiler_params=_cparams("parallel"), name="norm_mm")(x, g, w)


def _mm_res_kernel(a_ref, w_ref, x_ref, o_ref):
    o_ref[...] = x_ref[...] + _dot(a_ref[...], w_ref[...])


def _mm_res(a, w, x, *, tm):
    n, d = x.shape
    return pl.pallas_call(
        _mm_res_kernel, grid=(n // tm,),
        in_specs=[pl.BlockSpec((tm, a.shape[1]), lambda i: (i, 0)), pl.BlockSpec(w.shape, lambda i: (0, 0)),
                  pl.BlockSpec((tm, d), lambda i: (i, 0))],
        out_specs=pl.BlockSpec((tm, d), lambda i: (i, 0)), out_shape=jax.ShapeDtypeStruct((n, d), F32),
        compiler_params=_cparams("parallel"), name="mm_res")(a, w, x)


def _memattn_kernel(q_ref, k_ref, v_ref, o_ref):
    dh = q_ref.shape[2] // MEM_HEADS
    scale = dh ** -0.5
    for h in range(MEM_HEADS):
        cs = slice(h * dh, (h + 1) * dh)
        s = _dot_nt(q_ref[0, :, cs], k_ref[0, :, cs].astype(BF16)) * scale
        p = jnp.exp(s - jnp.max(s, axis=-1, keepdims=True))
        l = jnp.sum(p, axis=-1, keepdims=True)
        o_ref[0, :, cs] = (_dot(p.astype(BF16), v_ref[0, :, cs].astype(BF16)) / l).astype(o_ref.dtype)


def _memattn(q, mk, mv, *, tm):
    b, t, d = q.shape
    mem_spec = pl.BlockSpec((1,) + mk.shape[1:], lambda bi, i: (bi,) + (0,) * (mk.ndim - 1))
    return pl.pallas_call(
        _memattn_kernel, grid=(b, t // tm),
        in_specs=[pl.BlockSpec((1, tm, d), lambda bi, i: (bi, i, 0)), mem_spec, mem_spec],
        out_specs=pl.BlockSpec((1, tm, d), lambda bi, i: (bi, i, 0)), out_shape=jax.ShapeDtypeStruct((b, t, d), BF16),
        compiler_params=_cparams("parallel", "parallel"), name="mem_attn")(q, mk, mv)


def _memblock_kernel(x_ref, orw_ref, omla_ref, gmo_ref, wa_ref, wb_ref, gq_ref, wq_ref, k_ref, v_ref, wo_ref, o_ref):
    om = _rms(omla_ref[0], gmo_ref[...]).astype(BF16)
    x1 = x_ref[0] + _dot(orw_ref[0], wa_ref[...]) + _dot(om, wb_ref[...])
    q = _dot(_rms(x1, gq_ref[...]).astype(BF16), wq_ref[...]).astype(BF16)
    dh = q.shape[1] // MEM_HEADS
    scale = dh ** -0.5
    outs = []
    for h in range(MEM_HEADS):
        cs = slice(h * dh, (h + 1) * dh)
        s = _dot_nt(q[:, cs], k_ref[0, :, cs].astype(BF16)) * scale
        p = jnp.exp(s - jnp.max(s, axis=-1, keepdims=True))
        l = jnp.sum(p, axis=-1, keepdims=True)
        outs.append((_dot(p.astype(BF16), v_ref[0, :, cs].astype(BF16)) / l).astype(BF16))
    o_ref[0] = x1 + _dot(jnp.concatenate(outs, axis=1), wo_ref[...])


def _memblock(x, orw, omla, gmo, wa, wb, gq, wq, mk, mv, wo, *, tm):
    b, t, d = x.shape
    row = lambda bi, i: (bi, i, 0)
    seq = lambda bi, i: (bi, 0, 0)
    const = lambda bi, i: (0, 0)
    cs = lambda a: pl.BlockSpec(a.shape, const)
    return pl.pallas_call(
        _memblock_kernel, grid=(b, t // tm),
        in_specs=[pl.BlockSpec((1, tm, d), row), pl.BlockSpec((1, tm, orw.shape[2]), row), pl.BlockSpec((1, tm, omla.shape[2]), row),
                  cs(gmo), cs(wa), cs(wb), cs(gq), cs(wq), pl.BlockSpec((1,) + mk.shape[1:], seq),
                  pl.BlockSpec((1,) + mv.shape[1:], seq), cs(wo)],
        out_specs=pl.BlockSpec((1, tm, d), row), out_shape=jax.ShapeDtypeStruct((b, t, d), F32),
        compiler_params=_cparams("parallel", "parallel"), name="mem_block")(x, orw, omla, gmo, wa, wb, gq, wq, mk, mv, wo)


FFN_CHUNK = 256
FFN_GROUP = 4


def _ffn_kernel(x_ref, prev_ref, g_ref, wup_ref, cw_ref, cb_ref, wdn_ref, gf_ref, y_ref, u_ref, act_s, *, tm, nch, prev_rows):
    i = pl.program_id(1)
    x = x_ref[0]
    h = _rms(x, g_ref[...]).astype(BF16)
    rows = lax.broadcasted_iota(jnp.int32, (tm, 1), 0)
    if prev_rows is None:
        @pl.when(i == 0)
        def _():
            u_ref[0] = prev_ref[0]
    else:
        hist = (rows % SROWS) < prev_rows

    def up(c):
        return _dot(h, wup_ref[c]), _dot(h, wup_ref[c + nch])

    def conv(c, u):
        if prev_rows is None:
            tail = u_ref[0, c]
            u_ref[0, c] = u[tm - SUBLANE:tm, :]
            p1, p2 = tail[SUBLANE - 1:SUBLANE, :], tail[SUBLANE - 2:SUBLANE - 1, :]
            u1, u2 = pltpu.roll(u, 1, axis=0), pltpu.roll(u, 2, axis=0)
            r8 = rows[:SUBLANE]
            u1 = jnp.concatenate([jnp.where(r8 == 0, p1, u1[:SUBLANE]), u1[SUBLANE:]], axis=0)
            u2 = jnp.concatenate([jnp.where(r8 == 0, p2, jnp.where(r8 == 1, p1, u2[:SUBLANE])), u2[SUBLANE:]], axis=0)
        else:
            u = jnp.where(hist, prev_ref[c], u)
            u_ref[c] = u
            u1 = pltpu.roll(u, 1, axis=0)
            u2 = pltpu.roll(u, 2, axis=0)
        cw = cw_ref[c]
        return cb_ref[c] + cw[0:1, :] * u2 + cw[1:2, :] * u1 + cw[2:3, :] * u

    def gated(c, ug, uv):
        gate = conv(c, ug)
        val = conv(c + nch, uv)
        act_s[c] = (gate * jax.nn.sigmoid(gate) * val).astype(BF16)

    def group(cs):
        ups = [up(c) for c in cs]
        for c, (ug, uv) in zip(cs, ups):
            gated(c, ug, uv)

    ngrp = nch // FFN_GROUP

    def body(gi, _):
        group([gi * FFN_GROUP + d for d in range(FFN_GROUP)])
        return 0

    lax.fori_loop(0, ngrp, body, 0)
    if nch % FFN_GROUP:
        group(list(range(ngrp * FFN_GROUP, nch)))
    f = _dot(jnp.concatenate([act_s[c] for c in range(nch)], axis=1), wdn_ref[...])
    y_ref[0] = _rms(x + f, gf_ref[...])


def _ffn(x, prev, g, wup, cw, cb, wdn, gf, *, tm, prev_rows):
    b, t, d = x.shape
    nch = wup.shape[0] // 2
    c3 = lambda bi, i: (0, 0, 0)
    if prev_rows is None:
        prev_spec = pl.BlockSpec((1, 2 * nch, SUBLANE, FFN_CHUNK), lambda bi, i: (bi, 0, 0, 0))
        u_spec = pl.BlockSpec((1, 2 * nch, SUBLANE, FFN_CHUNK), lambda bi, i: (bi, 0, 0, 0))
        u_shape = jax.ShapeDtypeStruct((b, 2 * nch, SUBLANE, FFN_CHUNK), F32)
    else:
        prev_spec = pl.BlockSpec((2 * nch, tm, FFN_CHUNK), lambda bi, i: (0, bi * (t // tm) + i, 0))
        u_spec = pl.BlockSpec((2 * nch, tm, FFN_CHUNK), lambda bi, i: (0, bi * (t // tm) + i, 0))
        u_shape = jax.ShapeDtypeStruct((2 * nch, b * t, FFN_CHUNK), F32)
    return pl.pallas_call(
        functools.partial(_ffn_kernel, tm=tm, nch=nch, prev_rows=prev_rows), grid=(b, t // tm),
        in_specs=[pl.BlockSpec((1, tm, d), lambda bi, i: (bi, i, 0)), prev_spec, pl.BlockSpec(g.shape, lambda bi, i: (0, 0)),
                  pl.BlockSpec(wup.shape, c3), pl.BlockSpec(cw.shape, c3), pl.BlockSpec(cb.shape, c3),
                  pl.BlockSpec(wdn.shape, lambda bi, i: (0, 0)), pl.BlockSpec(gf.shape, lambda bi, i: (0, 0))],
        out_specs=[pl.BlockSpec((1, tm, d), lambda bi, i: (bi, i, 0)), u_spec],
        out_shape=[jax.ShapeDtypeStruct((b, t, d), F32), u_shape],
        scratch_shapes=[pltpu.VMEM((nch, tm, FFN_CHUNK), BF16)],
        compiler_params=_cparams("parallel", "arbitrary"), name="conv_ffn")(x, prev, g, wup, cw, cb, wdn, gf)


def _rope_tables(pos, scale):
    half = QK_ROPE // 2
    inv = 1.0 / (ROPE_THETA ** (np.arange(half, dtype=np.float64) / half))
    ang = np.asarray(pos, np.float64)[:, None] * inv[None, :]
    cos, sin = np.cos(ang), np.sin(ang)
    n = len(pos)
    c = np.zeros((n, HEAD_PAD))
    s = np.zeros((n, HEAD_PAD))
    c[:, :QK_NOPE] = 1.0
    c[:, ROPE_LO:ROPE_LO + half] = cos
    c[:, ROPE_LO + half:ROPE_LO + QK_ROPE] = cos
    s[:, ROPE_LO:ROPE_LO + half] = -sin
    s[:, ROPE_LO + half:ROPE_LO + QK_ROPE] = sin
    return (jnp.asarray(c * scale, F32), jnp.asarray(s * scale, F32), jnp.asarray(c, F32), jnp.asarray(s, F32))


def _swap_halves(w):
    half = w.shape[-1] // 2
    return jnp.concatenate([w[..., half:], w[..., :half]], axis=-1)


def _prep_weights(w_in, w_uq, w_ukv, rw_w2, rw_a2, w_up, conv_w, conv_b, w_down):
    d = w_in.shape[0]
    z = lambda *s: jnp.zeros(s, F32)
    w_kr = w_in[:, Q_LORA + KV_LORA:MLA_PROJ]
    pad_head = lambda w: jnp.concatenate([z(d, ROPE_LO), w, z(d, HEAD_PAD - ROPE_LO - QK_ROPE)], axis=1)
    w1 = jnp.concatenate([w_in[:, :Q_LORA + KV_LORA], pad_head(w_kr), pad_head(_swap_halves(w_kr)), w_in[:, MLA_PROJ:]],
                         axis=1).astype(BF16)
    wq3 = w_uq.reshape(Q_LORA, MLA_HEADS, QK_NOPE + QK_ROPE)
    zq = lambda n: z(Q_LORA, MLA_HEADS, n)
    q_plain = jnp.concatenate([wq3, zq(HEAD_PAD - QK_NOPE - QK_ROPE)], axis=2)
    q_swap = jnp.concatenate([zq(QK_NOPE), _swap_halves(wq3[..., QK_NOPE:]), zq(HEAD_PAD - QK_NOPE - QK_ROPE)], axis=2)
    wq = jnp.concatenate([q_plain.reshape(Q_LORA, -1), q_swap.reshape(Q_LORA, -1)], axis=1).astype(BF16)
    wkv3 = w_ukv.reshape(KV_LORA, MLA_HEADS, QK_NOPE + V_DIM)
    zk = z(KV_LORA, MLA_HEADS, HEAD_PAD - QK_NOPE)
    wk = jnp.concatenate([wkv3[..., :QK_NOPE], zk], axis=2).reshape(KV_LORA, -1).astype(BF16)
    wv = wkv3[..., QK_NOPE:].reshape(KV_LORA, -1).T.astype(BF16)
    w_uk = jnp.transpose(wkv3[..., :QK_NOPE], (1, 2, 0))
    w_uk = jnp.concatenate([w_uk, z(MLA_HEADS, HEAD_PAD - QK_NOPE, KV_LORA)], axis=1).astype(BF16)
    w_uv = jnp.transpose(wkv3[..., QK_NOPE:], (1, 0, 2)).astype(BF16)
    wa2 = jnp.concatenate([jnp.concatenate([rw_w2, z(A_LORA, RW_WIDTH)], axis=0),
                           jnp.concatenate([z(DECAY_LORA, RW_WIDTH), rw_a2], axis=0)], axis=1).astype(BF16)
    f2 = w_up.shape[1]
    nch2 = f2 // FFN_CHUNK
    wup = jnp.transpose(w_up.reshape(d, nch2, FFN_CHUNK), (1, 0, 2)).astype(BF16)
    cw = jnp.transpose(conv_w.reshape(CONV_W, nch2, FFN_CHUNK), (1, 0, 2))
    cw = jnp.concatenate([cw, z(nch2, SUBLANE - CONV_W, FFN_CHUNK)], axis=1)
    cb = conv_b.reshape(nch2, 1, FFN_CHUNK)
    wdn = w_down.astype(BF16)
    return w1, wq, wk, wv, w_uk, w_uv, wa2, wup, cw, cb, wdn


def _state_to_pairs(s):
    b = s.shape[0]
    s = s.reshape(b, PAIRS, 2, RW_HEAD_DIM, RW_HEAD_DIM)
    zz = jnp.zeros_like(s[:, :, 0])
    top = jnp.concatenate([s[:, :, 0], zz], axis=-1)
    bot = jnp.concatenate([zz, s[:, :, 1]], axis=-1)
    return jnp.concatenate([top, bot], axis=-2)


def _pairs_to_state(s):
    b = s.shape[0]
    h0 = s[:, :, :RW_HEAD_DIM, :RW_HEAD_DIM]
    h1 = s[:, :, RW_HEAD_DIM:, RW_HEAD_DIM:]
    return jnp.stack([h0, h1], axis=2).reshape(b, RW_HEADS, RW_HEAD_DIM, RW_HEAD_DIM)


def _pick(n, pref):
    for t in pref:
        if n % t == 0:
            return t
    return n


def kernel(x_prompt, x_sample, cache_mla_latent, cache_mla_krope, cache_mem_k, cache_mem_v, state_rwkv, state_rwkv_shift, state_ffn_conv, page_table, mem_prompt, g_mix, w_in, q_norm_g, kv_norm_g, w_uq, w_ukv, g_mla_out, rw_mu, rw_w0, rw_w2, rw_a0, rw_a2, rw_g2, rw_k_k, rw_k_a, rw_r_k, rw_lnx_g, rw_lnx_b, w_o, g_mem_q, g_mem_kv, w_mq, w_mk, w_mv, w_mo, g_ffn, w_up, conv_w, conv_b, w_down, g_final):
    depth = w_in.shape[0]
    assert depth == 1, "single-layer step"
    bp, tp, d = x_prompt.shape
    bs, ts, _ = x_sample.shape
    npages = page_table.shape[1]
    past_len = npages * PAGE_SIZE
    row2 = lambda a: a.reshape(1, -1)
    l = 0
    w1, wq, wk, wv, w_uk, w_uv, wa2, wup, cw, cb, wdn = _prep_weights(
        w_in[l], w_uq[l], w_ukv[l], rw_w2[l], rw_a2[l], w_up[l], conv_w[l], conv_b[l], w_down[l])
    wo_a, wo_b = w_o[l, :RW_WIDTH].astype(BF16), w_o[l, RW_WIDTH:].astype(BF16)
    wmq, wmo = w_mq[l].astype(BF16), w_mo[l].astype(BF16)
    wmkv = jnp.concatenate([w_mk[l], w_mv[l]], axis=1).astype(BF16)
    g2 = rw_g2[l].astype(BF16)
    nch2 = wup.shape[0]
    rw_vecs = (row2(rw_mu[l]), row2(rw_w0[l]), wa2, row2(rw_a0[l]), g2, row2(rw_k_k[l]), row2(rw_k_a[l]),
               row2(rw_r_k[l]), row2(rw_lnx_g[l]), row2(rw_lnx_b[l]))

    def after_attention(x2, orw, omla, mem_k, mem_v, conv_in, b, t, tm, tm_mem, tm_ffn, prev_rows):
        if prev_rows is None:
            xm = _memblock(x2.reshape(b, t, d), orw.reshape(b, t, -1), omla.reshape(b, t, -1), row2(g_mla_out[l]),
                           wo_a, wo_b, row2(g_mem_q[l]), wmq, mem_k, mem_v, wmo, tm=tm)
        else:
            x1 = _mix(x2, orw, omla, row2(g_mla_out[l]), wo_a, wo_b, tm=tm)
            qm = _norm_mm(x1, row2(g_mem_q[l]), wmq, BF16, tm=tm)
            om = _memattn(qm.reshape(b, t, d), mem_k, mem_v, tm=tm_mem)
            xm = _mm_res(om.reshape(b * t, d), wmo, x1, tm=tm)
        fb, ft = (b, t) if prev_rows is None else (1, b * t)
        return _ffn(xm.reshape(fb, ft, d), conv_in, row2(g_ffn[l]), wup, cw, cb, wdn, row2(g_final), tm=tm_ffn,
                    prev_rows=prev_rows)

    n_p = bp * tp
    tm_p = _pick(tp, (512, 256, 128, 64, 32, 16, 8))
    tabs_p = _rope_tables(np.arange(tp), MLA_SCALE * LOG2E)
    xp2 = x_prompt.reshape(n_p, d)
    q_p, lat_p, kr_p, prw_p, k_p, v_p = _inproj(
        xp2, row2(g_mix[l]), w1, row2(q_norm_g[l]), row2(kv_norm_g[l]), wq, tabs_p, wk, wv, tm=tm_p, with_kv=True, seq=tp)
    tq = FLASH_T
    omla_p = _flash(q_p, k_p, v_p, batch=bp, seq=tp, tq=tq)
    tb_p = _pick(tp, (512, 256, 128, 64, 32, 16, 8))
    if tb_p % QCH == 0:
        zeros_state = jnp.zeros((bp, RW_HEADS // QUAD, QW, QW), F32)
        orw_p, st_p = _rwkv_quad(prw_p.reshape(bp, tp, RW_PROJ), zeros_state, *rw_vecs, tb=tb_p)
        st_p = _quads_to_state(st_p)
    else:
        zeros_state = jnp.zeros((bp, PAIRS, LANE, LANE), F32)
        orw_p, st_p = _rwkv(prw_p.reshape(bp, tp, RW_PROJ), zeros_state, *rw_vecs, tb=tb_p,
                            ch=_pick(tb_p, (32, 16, 8)), row_lo=0, row_hi=tb_p)
        st_p = _pairs_to_state(st_p)
    mkv = _norm_mm(mem_prompt.reshape(-1, d), row2(g_mem_kv[l]), wmkv, F32, tm=_pick(mem_prompt.shape[0] * mem_prompt.shape[1], (512, 256, 128, 8)))
    n_mem = mem_prompt.shape[1]
    mk_p = mkv[:, :d].reshape(bp, n_mem, d)
    mv_p = mkv[:, d:].reshape(bp, n_mem, d)
    conv0_p = jnp.zeros((bp, nch2, SUBLANE, FFN_CHUNK), F32)
    y_p, u_p = after_attention(xp2, orw_p.reshape(n_p, RW_WIDTH), omla_p, mk_p, mv_p, conv0_p, bp, tp, tm_p, tm_p, tm_p, None)
    conv_p = jnp.transpose(u_p[:, :, SUBLANE - (CONV_W - 1):, :], (0, 2, 1, 3)).reshape(bp, CONV_W - 1, nch2 * FFN_CHUNK)

    n_s = bs * SROWS
    xs3 = jnp.pad(x_sample, ((0, 0), (SLO, SROWS - SLO - ts), (0, 0)))
    pos_s = np.tile(np.concatenate([np.zeros(SLO), past_len + np.arange(ts), np.zeros(SROWS - SLO - ts)]), bs)
    tm_s = _pick(n_s, (1024, 512, 256, 128, 64, 32, 16, 8))
    tabs_s = _rope_tables(pos_s[:tm_s], MLA_SCALE)
    q_s, lat_s, kr_s, prw_s = _inproj(
        xs3.reshape(n_s, d), row2(g_mix[l]), w1, row2(q_norm_g[l]), row2(kv_norm_g[l]), wq, tabs_s, None, None,
        tm=tm_s, with_kv=False)
    qabs = _bmm_cols(q_s, w_uk, BF16)
    qabs = qabs.reshape(MLA_HEADS, bs, SROWS, KV_LORA)[:, :, SLO:SLO + ts]
    qabs = jnp.transpose(qabs, (1, 0, 2, 3)).reshape(bs, MLA_HEADS * ts, KV_LORA)
    qrope = q_s.reshape(bs, SROWS, MLA_HEADS, HEAD_PAD)[:, SLO:SLO + ts, :, ROPE_LO:ROPE_LO + QK_ROPE]
    qrope = jnp.transpose(qrope, (0, 2, 1, 3)).reshape(bs, MLA_HEADS * ts, QK_ROPE)
    olat = _paged(page_table, qabs, qrope, lat_s.reshape(bs, SROWS, KV_LORA), kr_s.reshape(bs, SROWS, QK_ROPE),
                  cache_mla_latent[l], jnp.swapaxes(cache_mla_krope[l], 1, 2), n_new=ts)
    olat = jnp.transpose(olat.reshape(bs, MLA_HEADS, ts, KV_LORA), (1, 0, 2, 3))
    olat = jnp.pad(olat, ((0, 0), (0, 0), (SLO, SROWS - SLO - ts), (0, 0))).reshape(MLA_HEADS, n_s, KV_LORA)
    omla_s = _bmm(olat, w_uv, F32)
    omla_s = jnp.transpose(omla_s, (1, 0, 2)).reshape(n_s, MLA_HEADS * V_DIM)
    prw_s3 = prw_s.reshape(bs, SROWS, RW_PROJ).at[:, SLO - 1, :].set(state_rwkv_shift[l])
    orw_s, st_s = _rwkv(prw_s3, _state_to_pairs(state_rwkv[l]), *rw_vecs, tb=SROWS, ch=SROWS, row_lo=SLO, row_hi=SLO + ts)
    hist = jnp.transpose(state_ffn_conv[l].reshape(bs, CONV_W - 1, nch2, FFN_CHUNK), (2, 0, 1, 3))
    hist = jnp.pad(hist, ((0, 0), (0, 0), (SLO - (CONV_W - 1), SROWS - SLO), (0, 0))).reshape(nch2, n_s, FFN_CHUNK)
    mk_s = cache_mem_k[l].reshape(bs, cache_mem_k.shape[2], d)
    mv_s = cache_mem_v[l].reshape(bs, cache_mem_v.shape[2], d)
    y_s, u_s = after_attention(xs3.reshape(n_s, d), orw_s.reshape(n_s, RW_WIDTH), omla_s, mk_s, mv_s, hist,
                               bs, SROWS, tm_s, SROWS, _pick(n_s, (256, 128, 64, 32, 16, 8)), SLO)
    u_s = u_s.reshape(nch2, bs, SROWS, FFN_CHUNK)[:, :, SLO + ts - (CONV_W - 1):SLO + ts]
    conv_s = jnp.transpose(u_s, (1, 2, 0, 3)).reshape(bs, CONV_W - 1, nch2 * FFN_CHUNK)

    real = lambda a, w: a.reshape(bs, SROWS, w)[:, SLO:SLO + ts]
    mem5 = lambda a: a.reshape(1, bp, n_mem, MEM_HEADS, d // MEM_HEADS)
    return (y_p, real(y_s, d),
            lat_p.reshape(1, bp, tp, KV_LORA), kr_p.reshape(1, bp, tp, QK_ROPE), mem5(mk_p), mem5(mv_p),
            st_p[None], prw_p.reshape(bp, tp, RW_PROJ)[:, -1][None], conv_p[None],
            real(lat_s, KV_LORA)[None], real(kr_s, QK_ROPE)[None], _pairs_to_state(st_s)[None],
            prw_s.reshape(bs, SROWS, RW_PROJ)[:, SLO + ts - 1][None], conv_s[None])
```

```python
import functools

import numpy as np
import jax
import jax.numpy as jnp
from jax import lax
from jax.experimental import pallas as pl
from jax.experimental.pallas import tpu as pltpu

F32 = jnp.float32
BF16 = jnp.bfloat16

RW_HEADS = 8
RW_HEAD_DIM = 64
RW_WIDTH = 512
DECAY_LORA = 64
A_LORA = 64
GATE_LORA = 128
RW_PROJ = 3 * RW_WIDTH + DECAY_LORA + A_LORA + GATE_LORA
LNX_EPS = 64e-5
MLA_HEADS = 8
QK_NOPE = 64
QK_ROPE = 32
V_DIM = 64
Q_LORA = 384
KV_LORA = 256
MLA_PROJ = Q_LORA + KV_LORA + QK_ROPE
MLA_SCALE = (QK_NOPE + QK_ROPE) ** -0.5
ROPE_THETA = 10000.0
MEM_HEADS = 4
CONV_W = 3
NORM_EPS = 1e-6
PAGE_SIZE = 128

LANE = 128
SUBLANE = 8
VMEM_LIMIT = 56 * 1024 * 1024
HEAD_PAD = LANE
ROPE_LO = QK_NOPE
SROWS = 8
SLO = 2
NEG = -1e30
LOG2E = float(np.log2(np.e))
HI = lax.Precision.HIGHEST


def _cparams(*sem):
    return pltpu.CompilerParams(dimension_semantics=sem, vmem_limit_bytes=VMEM_LIMIT)


def _rms(x, g, eps=NORM_EPS):
    return x * lax.rsqrt(jnp.mean(x * x, axis=-1, keepdims=True) + eps) * g


def _dot(a, b):
    return jnp.dot(a, b, preferred_element_type=F32)


def _dot_nt(a, b, precision=None):
    return lax.dot_general(a, b, (((1,), (1,)), ((), ())), preferred_element_type=F32, precision=precision)


def _dot_tn(a, b, precision=None):
    return lax.dot_general(a, b, (((0,), (0,)), ((), ())), preferred_element_type=F32, precision=precision)


def _doth(a, b):
    return jnp.dot(a, b, preferred_element_type=F32, precision=HI)


C_Q0, C_KV0, C_KR0, C_KRS0, C_RW0 = 0, Q_LORA, Q_LORA + KV_LORA, Q_LORA + KV_LORA + LANE, Q_LORA + KV_LORA + 2 * LANE
W1_COLS = C_RW0 + RW_PROJ


def _inproj_kernel(x_ref, g_ref, w1_ref, qg_ref, kvg_ref, wq_ref, cq_ref, sq_ref, ck_ref, sk_ref, *rest, with_kv):
    if with_kv:
        wk_ref, wv_ref, q_out, lat_out, kr_out, prw_out, k_out, v_out = rest
    else:
        q_out, lat_out, kr_out, prw_out = rest
    h = _rms(x_ref[...], g_ref[...]).astype(BF16)
    prw_out[...] = _dot(h, w1_ref[:, C_RW0:W1_COLS])
    pm = _dot(h, w1_ref[:, 0:C_RW0])
    cqn = _rms(pm[:, C_Q0:C_KV0], qg_ref[...]).astype(BF16)
    lat = _rms(pm[:, C_KV0:C_KR0], kvg_ref[...])
    lat_out[...] = lat
    krope = pm[:, C_KR0:C_KRS0] * ck_ref[...] + pm[:, C_KRS0:C_RW0] * sk_ref[...]
    kr_out[...] = krope[:, ROPE_LO:ROPE_LO + QK_ROPE]
    q12 = _dot(cqn, wq_ref[...])
    nq = MLA_HEADS * HEAD_PAD
    cq, sq = cq_ref[...], sq_ref[...]
    for hd in range(MLA_HEADS):
        a, b = hd * HEAD_PAD, (hd + 1) * HEAD_PAD
        q_out[:, a:b] = (q12[:, a:b] * cq + q12[:, nq + a:nq + b] * sq).astype(BF16)
    if with_kv:
        latb = lat.astype(BF16)
        kn = _dot(latb, wk_ref[...])
        for hd in range(MLA_HEADS):
            a, b = hd * HEAD_PAD, (hd + 1) * HEAD_PAD
            k_out[:, a:b] = (kn[:, a:b] + krope).astype(BF16)
        vt = _dot_nt(wv_ref[...], latb).astype(BF16)
        for jb in range(v_out.shape[1]):
            v_out[0, jb] = vt[:, jb * v_out.shape[3]:(jb + 1) * v_out.shape[3]]


def _inproj(x, g, w1, qg, kvg, wq, tabs, wk, wv, *, tm, with_kv, seq=None, vblock=None):
    n, d = x.shape
    ttab = tabs[0].shape[0]
    nt = ttab // tm
    row = lambda i: (i, 0)
    const = lambda i: (0, 0)
    tab = lambda i: (i % nt, 0)
    in_specs = [pl.BlockSpec((tm, d), row), pl.BlockSpec(g.shape, const), pl.BlockSpec(w1.shape, const),
                pl.BlockSpec(qg.shape, const), pl.BlockSpec(kvg.shape, const), pl.BlockSpec(wq.shape, const)]
    in_specs += [pl.BlockSpec((tm, LANE), tab)] * 4
    args = [x, g, w1, qg, kvg, wq, *tabs]
    nq = MLA_HEADS * HEAD_PAD
    out_shape = [jax.ShapeDtypeStruct((n, nq), BF16), jax.ShapeDtypeStruct((n, KV_LORA), F32),
                 jax.ShapeDtypeStruct((n, QK_ROPE), F32), jax.ShapeDtypeStruct((n, RW_PROJ), F32)]
    out_specs = [pl.BlockSpec((tm, nq), row), pl.BlockSpec((tm, KV_LORA), row),
                 pl.BlockSpec((tm, QK_ROPE), row), pl.BlockSpec((tm, RW_PROJ), row)]
    if with_kv:
        in_specs += [pl.BlockSpec(wk.shape, const), pl.BlockSpec(wv.shape, const)]
        args += [wk, wv]
        nv = MLA_HEADS * V_DIM
        npb = seq // tm
        out_shape += [jax.ShapeDtypeStruct((n, nq), BF16), jax.ShapeDtypeStruct((n // seq, seq // vblock, nv, vblock), BF16)]
        out_specs += [pl.BlockSpec((tm, nq), row),
                      pl.BlockSpec((1, tm // vblock, nv, vblock), lambda i: (i // npb, i % npb, 0, 0))]
    return pl.pallas_call(
        functools.partial(_inproj_kernel, with_kv=with_kv), grid=(n // tm,), in_specs=in_specs,
        out_specs=out_specs, out_shape=out_shape, compiler_params=_cparams("parallel"),
        name="inproj_kv" if with_kv else "inproj")(*args)


FLASH_HPB = 4
FLASH_T = 512
FLASH_LROWS = 16


def _flash_kernel(q_ref, k_ref, vt_ref, o_ref, st_a, st_b, m_s, acc_s, *, tq):
    tk = tq
    qi = pl.program_id(2)
    assert tq == tk
    nfull = qi
    qs = [q_ref[:, h * HEAD_PAD:(h + 1) * HEAD_PAD] for h in range(FLASH_HPB)]
    kidx = lax.broadcasted_iota(jnp.int32, (tk, tq), 0)
    qidx = lax.broadcasted_iota(jnp.int32, (tk, tq), 1)

    ones = jnp.ones((FLASH_LROWS, tk), BF16)

    def scores(slot, j):
        start = pl.multiple_of(j * tk, tk)
        for h in range(FLASH_HPB):
            slot[h] = _dot_nt(k_ref[pl.ds(start, tk), h * HEAD_PAD:(h + 1) * HEAD_PAD], qs[h])

    def step(slot, j, diagonal):
        for h in range(FLASH_HPB):
            m = m_s[h]
            st = slot[h]
            if diagonal:
                st = jnp.where(kidx <= qidx, st, NEG)
            m_new = jnp.maximum(m, jnp.max(st, axis=0, keepdims=True))
            alpha = jnp.exp2(m - m_new)
            p = jnp.exp2(st - m_new).astype(BF16)
            vt1 = jnp.concatenate([vt_ref[0, j, h * V_DIM:(h + 1) * V_DIM, :], ones], axis=0)
            acc_s[h] = alpha * acc_s[h] + _dot(vt1, p)
            m_s[h] = m_new

    m_s[...] = jnp.full(m_s.shape, NEG, F32)
    acc_s[...] = jnp.zeros(acc_s.shape, F32)

    scores(st_a, 0)

    def pair(jj, _):
        scores(st_b, 2 * jj + 1)
        step(st_a, 2 * jj, False)
        scores(st_a, 2 * jj + 2)
        step(st_b, 2 * jj + 1, False)
        return 0

    npair = nfull // 2
    lax.fori_loop(0, npair, pair, 0)

    @pl.when(nfull % 2 == 1)
    def _():
        scores(st_b, nfull)
        step(st_a, nfull - 1, False)
        step(st_b, nfull, True)

    @pl.when(nfull % 2 == 0)
    def _():
        step(st_a, nfull, True)

    ot = jnp.concatenate([acc_s[h, :V_DIM] / acc_s[h, V_DIM:V_DIM + 1] for h in range(FLASH_HPB)], axis=0)
    o_ref[...] = ot.T


def _flash(q, k, vt, *, batch, seq, tq):
    nqb = seq // tq
    grid = (batch, MLA_HEADS // FLASH_HPB, nqb)
    w = FLASH_HPB * HEAD_PAD
    return pl.pallas_call(
        functools.partial(_flash_kernel, tq=tq), grid=grid,
        in_specs=[pl.BlockSpec((tq, w), lambda b, h, i: (b * nqb + i, h)),
                  pl.BlockSpec((seq, w), lambda b, h, i: (b, h)),
                  pl.BlockSpec((1, seq // tq, FLASH_HPB * V_DIM, tq), lambda b, h, i: (b, 0, h, 0))],
        out_specs=pl.BlockSpec((tq, FLASH_HPB * V_DIM), lambda b, h, i: (b * nqb + i, h)),
        out_shape=jax.ShapeDtypeStruct((batch * seq, MLA_HEADS * V_DIM), F32),
        scratch_shapes=[pltpu.VMEM((FLASH_HPB, tq, tq), F32), pltpu.VMEM((FLASH_HPB, tq, tq), F32),
                        pltpu.VMEM((FLASH_HPB, 1, tq), F32), pltpu.VMEM((FLASH_HPB, V_DIM + FLASH_LROWS, tq), F32)],
        compiler_params=_cparams("parallel", "parallel", "arbitrary"), name="mla_flash")(q, k, vt)


PAGES_PER_STEP = 32
PAGE_GROUP = 4
PAGE_STREAMS = 4


def _paged_kernel(pt_ref, qa_ref, qr_ref, latn_ref, krn_ref, lat_hbm, kr_hbm, o_ref,
                  lat_buf, kr_buf, sem, m_s, l_s, acc_s, *, n_new, g, group, streams):
    seq, step = pl.program_id(0), pl.program_id(1)
    nsteps = pl.num_programs(1)
    lin = seq * nsteps + step
    slot = lin % 2

    def page_copy(page, sl, j):
        return (pltpu.make_async_copy(lat_hbm.at[page], lat_buf.at[sl, j], sem.at[sl, 0]),
                pltpu.make_async_copy(kr_hbm.at[page], kr_buf.at[sl, j], sem.at[sl, 1]))

    def start_step(sq, st, sl):
        for j in range(g):
            for c in page_copy(pt_ref[sq, st * g + j], sl, j):
                c.start()

    @pl.when(lin == 0)
    def _():
        start_step(0, 0, 0)

    @pl.when(lin + 1 < pl.num_programs(0) * nsteps)
    def _():
        wrap = step + 1 == nsteps
        start_step(jnp.where(wrap, seq + 1, seq), jnp.where(wrap, 0, step + 1), 1 - slot)

    for j in range(g):
        for c in page_copy(0, slot, j):
            c.wait()

    lat_refs = [lat_buf.at[slot, j] for j in range(g)]
    kr_refs = [kr_buf.at[slot, j] for j in range(g)]

    @pl.when(step == 0)
    def _():
        m_s[...] = jnp.full(m_s.shape, NEG, F32)
        l_s[...] = jnp.zeros(l_s.shape, F32)
        acc_s[...] = jnp.zeros(acc_s.shape, F32)

    qa = qa_ref[0]
    qr = qr_ref[0]

    def update(carry, s, vb):
        m, l, acc = carry
        m_new = jnp.maximum(m, jnp.max(s, axis=-1, keepdims=True))
        alpha = jnp.exp(m - m_new)
        p = jnp.exp(s - m_new)
        l = alpha * l + jnp.sum(p, axis=-1, keepdims=True)
        acc = alpha * acc + _dot(p.astype(BF16), vb)
        return m_new, l, acc

    def scores(grp):
        pages = range(grp * group, (grp + 1) * group)
        latb = jnp.concatenate([lat_refs[j][...].astype(BF16) for j in pages], axis=0)
        krt = jnp.concatenate([kr_refs[j][...].astype(BF16) for j in pages], axis=1)
        return _dot_nt(qa, latb) + _dot(qr, krt), latb

    carries = [(m_s[t], l_s[t], acc_s[t]) for t in range(streams)]
    ngrp = g // group
    ahead = streams
    pend = [scores(i) for i in range(min(ahead, ngrp))]
    for grp in range(ngrp):
        if grp + ahead < ngrp:
            pend.append(scores(grp + ahead))
        t = grp % streams
        carries[t] = update(carries[t], *pend[grp])
        pend[grp] = None
    for t in range(streams):
        m_s[t], l_s[t], acc_s[t] = carries[t]

    @pl.when(step == pl.num_programs(1) - 1)
    def _():
        latn = latn_ref[0].astype(BF16)
        krn = krn_ref[0].astype(BF16)
        s = _dot_nt(qa, latn) + _dot_nt(qr, krn)
        qt = lax.broadcasted_iota(jnp.int32, s.shape, 0) % n_new
        kt = lax.broadcasted_iota(jnp.int32, s.shape, 1) - SLO
        s = jnp.where((kt >= 0) & (kt <= qt), s, NEG)
        m, l, acc = update((m_s[0], l_s[0], acc_s[0]), s, latn)
        for t in range(1, streams):
            mt = m_s[t]
            m_new = jnp.maximum(m, mt)
            a, bt = jnp.exp(m - m_new), jnp.exp(mt - m_new)
            l = a * l + bt * l_s[t]
            acc = a * acc + bt * acc_s[t]
            m = m_new
        o_ref[0] = acc / l


def _paged(page_table, qa, qr, lat_new, kr_new, cache_lat, cache_kr, *, n_new):
    b, npages = page_table.shape
    g = max(d for d in range(1, PAGES_PER_STEP + 1) if npages % d == 0)
    group = max(d for d in range(1, PAGE_GROUP + 1) if g % d == 0)
    streams = min(PAGE_STREAMS, g // group)
    rows = qa.shape[1]
    seq3 = lambda i, s, pt: (i, 0, 0)
    grid_spec = pltpu.PrefetchScalarGridSpec(
        num_scalar_prefetch=1, grid=(b, npages // g),
        in_specs=[pl.BlockSpec((1, rows, KV_LORA), seq3), pl.BlockSpec((1, rows, QK_ROPE), seq3),
                  pl.BlockSpec((1, SROWS, KV_LORA), seq3), pl.BlockSpec((1, SROWS, QK_ROPE), seq3),
                  pl.BlockSpec(memory_space=pl.ANY), pl.BlockSpec(memory_space=pl.ANY)],
        out_specs=pl.BlockSpec((1, rows, KV_LORA), seq3),
        scratch_shapes=[pltpu.VMEM((2, g, PAGE_SIZE, KV_LORA), F32), pltpu.VMEM((2, g, QK_ROPE, PAGE_SIZE), F32),
                        pltpu.SemaphoreType.DMA((2, 2)),
                        pltpu.VMEM((streams, rows, 1), F32), pltpu.VMEM((streams, rows, 1), F32),
                        pltpu.VMEM((streams, rows, KV_LORA), F32)])
    return pl.pallas_call(
        functools.partial(_paged_kernel, n_new=n_new, g=g, group=group, streams=streams), grid_spec=grid_spec,
        out_shape=jax.ShapeDtypeStruct((b, rows, KV_LORA), F32),
        compiler_params=_cparams("arbitrary", "arbitrary"), name="mla_paged")(
            page_table, qa, qr, lat_new, kr_new, cache_lat, cache_kr)


def _bmm_kernel(a_ref, w_ref, o_ref):
    o_ref[0] = _dot(a_ref[...].astype(BF16), w_ref[0]).astype(o_ref.dtype)


def _bmm_cols(a, w, out_dtype):
    n = a.shape[0]
    hh, k, m = w.shape
    return pl.pallas_call(
        _bmm_kernel, grid=(hh,),
        in_specs=[pl.BlockSpec((n, k), lambda h: (0, h)), pl.BlockSpec((1, k, m), lambda h: (h, 0, 0))],
        out_specs=pl.BlockSpec((1, n, m), lambda h: (h, 0, 0)),
        out_shape=jax.ShapeDtypeStruct((hh, n, m), out_dtype), compiler_params=_cparams("parallel"),
        name="bmm_cols")(a, w)


def _bmm_kernel3(a_ref, w_ref, o_ref):
    o_ref[0] = _dot(a_ref[0].astype(BF16), w_ref[0]).astype(o_ref.dtype)


def _bmm(a, w, out_dtype):
    hh, n, k = a.shape
    m = w.shape[2]
    return pl.pallas_call(
        _bmm_kernel3, grid=(hh,),
        in_specs=[pl.BlockSpec((1, n, k), lambda h: (h, 0, 0)), pl.BlockSpec((1, k, m), lambda h: (h, 0, 0))],
        out_specs=pl.BlockSpec((1, n, m), lambda h: (h, 0, 0)),
        out_shape=jax.ShapeDtypeStruct((hh, n, m), out_dtype), compiler_params=_cparams("parallel"),
        name="bmm")(a, w)


PAIRS = RW_HEADS // 2
C_R, C_K, C_V, C_WA, C_G = 0, RW_WIDTH, 2 * RW_WIDTH, 3 * RW_WIDTH, 3 * RW_WIDTH + DECAY_LORA + A_LORA


def _seg_sum(x):
    lane = lax.broadcasted_iota(jnp.int32, (x.shape[0], LANE), 1)
    low = lane < RW_HEAD_DIM
    outs = []
    for t in range(x.shape[1] // LANE):
        xt = x[:, t * LANE:(t + 1) * LANE]
        s0 = jnp.sum(jnp.where(low, xt, 0.0), axis=-1, keepdims=True)
        s1 = jnp.sum(jnp.where(low, 0.0, xt), axis=-1, keepdims=True)
        outs.append(jnp.where(low, s0, s1))
    return outs[0] if len(outs) == 1 else jnp.concatenate(outs, axis=-1)


def _rwkv_prologue(p_ref, s0_ref, so_ref, vec_refs, scratch, *, tb, row_lo, row_hi):
    mu_ref, w0_ref, wa2_ref, a0_ref, g2_ref, kk_ref, ka_ref, rk_ref = vec_refs
    carry_s, r_s, k_s, v_s, kk_s, b_s, ld_s = scratch
    i = pl.program_id(1)

    @pl.when(i == 0)
    def _():
        carry_s[...] = jnp.zeros(carry_s.shape, F32)
        so_ref[0] = s0_ref[0]

    p = p_ref[0]
    rows1 = lax.broadcasted_iota(jnp.int32, (tb, 1), 0)
    prev = jnp.where(rows1 == 0, carry_s[SUBLANE - 1:SUBLANE, :], pltpu.roll(p, 1, axis=0))
    carry_s[...] = p[tb - SUBLANE:tb, :]
    s = p + (prev - p) * mu_ref[...]
    r = s[:, C_R:C_K]
    k = s[:, C_K:C_V]
    v = s[:, C_V:C_WA]
    wa = s[:, C_WA:C_G]
    lane = lax.broadcasted_iota(jnp.int32, wa.shape, 1)
    z = jnp.where(lane < DECAY_LORA, jnp.tanh(wa), wa).astype(BF16)
    lin = _dot(z, wa2_ref[...])
    xw = -(w0_ref[...] + lin[:, :RW_WIDTH])
    w = -(jnp.maximum(xw, 0.0) + jnp.log(1.0 + jnp.exp(-jnp.abs(xw)))) - 0.5
    logd = -jnp.exp(w)
    a = jax.nn.sigmoid(a0_ref[...] + lin[:, RW_WIDTH:])
    gate = _dot(jax.nn.sigmoid(s[:, C_G:RW_PROJ]).astype(BF16), g2_ref[...])
    kk = k * kk_ref[...]
    kk = kk / jnp.maximum(jnp.sqrt(_seg_sum(kk * kk)), 1e-12)
    k = k * (1.0 + (a - 1.0) * ka_ref[...])
    bonus = _seg_sum(r * k * rk_ref[...]) * v
    if row_lo > 0 or row_hi < tb:
        live = (rows1 >= row_lo) & (rows1 < row_hi)
        logd = jnp.where(live, logd, 0.0)
        kk = jnp.where(live, kk, 0.0)
        k = jnp.where(live, k, 0.0)
        v = jnp.where(live, v, 0.0)
    r_s[...] = r
    k_s[...] = k
    v_s[...] = v
    kk_s[...] = kk
    b_s[...] = kk * a
    ld_s[...] = logd
    return gate, bonus


def _rwkv_epilogue(out, gate, bonus, lg_ref, lb_ref, o_ref):
    mean = _seg_sum(out) * (1.0 / RW_HEAD_DIM)
    cen = out - mean
    var = _seg_sum(cen * cen) * (1.0 / RW_HEAD_DIM)
    y = cen * lax.rsqrt(var + LNX_EPS) * lg_ref[...] + lb_ref[...]
    o_ref[0] = ((y + bonus) * gate).astype(o_ref.dtype)


def _rwkv_kernel(p_ref, s0_ref, mu_ref, w0_ref, wa2_ref, a0_ref, g2_ref, kk_ref, ka_ref, rk_ref, lg_ref, lb_ref,
                 o_ref, so_ref, carry_s, r_s, k_s, v_s, kk_s, b_s, ld_s, out_s, *, tb, ch, row_lo, row_hi):
    gate, bonus = _rwkv_prologue(p_ref, s0_ref, so_ref, (mu_ref, w0_ref, wa2_ref, a0_ref, g2_ref, kk_ref, ka_ref, rk_ref),
                                 (carry_s, r_s, k_s, v_s, kk_s, b_s, ld_s), tb=tb, row_lo=row_lo, row_hi=row_hi)
    c2 = 2 * ch
    ri = lax.broadcasted_iota(jnp.int32, (c2, c2), 0)
    ci = lax.broadcasted_iota(jnp.int32, (c2, c2), 1)
    strict = (ci % ch) < (ri % ch)
    incl = (ci % ch) <= (ri % ch)
    eye = (ri == ci).astype(F32)
    tri = (lax.broadcasted_iota(jnp.int32, (ch, ch), 1) <= lax.broadcasted_iota(jnp.int32, (ch, ch), 0)).astype(F32)
    lane2 = lax.broadcasted_iota(jnp.int32, (c2, LANE), 1)
    row2 = lax.broadcasted_iota(jnp.int32, (c2, LANE), 0)
    own = (lane2 < RW_HEAD_DIM) == (row2 < ch)

    def stack(x):
        return jnp.where(own, jnp.concatenate([x, x], axis=0), 0.0)

    bf = lambda x: x.astype(BF16)
    pairs = [slice(pr * LANE, (pr + 1) * LANE) for pr in range(PAIRS)]

    def chunk(c, _):
        ts = pl.ds(pl.multiple_of(c * ch, ch), ch)
        css = [_doth(tri, ld_s[ts, ls]) for ls in pairs]
        lhss, amats, tails = [], [], []
        for ls, cs in zip(pairs, css):
            ld = ld_s[ts, ls]
            e_inv = jnp.exp(-cs)
            lhs = bf(jnp.concatenate([stack(kk_s[ts, ls] * jnp.exp(cs - ld)), stack(r_s[ts, ls] * jnp.exp(cs))], axis=0))
            rhs = bf(jnp.concatenate([stack(k_s[ts, ls] * e_inv), stack(b_s[ts, ls] * e_inv)], axis=0))
            lhss.append(lhs)
            amats.append(_dot_nt(lhs, rhs))
        sts = [so_ref[0, pr] for pr in range(PAIRS)]
        gmats = [_dot_nt(lhs, bf(st)) for lhs, st in zip(lhss, sts)]
        vss = [bf(stack(v_s[ts, ls])) for ls in pairs]
        akbs = [jnp.where(strict, amat[:c2, c2:], 0.0) for amat in amats]
        ykk = [_dot(bf(jnp.where(strict, amat[:c2, :c2], 0.0)), vs) for amat, vs in zip(amats, vss)]
        yrk = [_dot(bf(jnp.where(incl, amat[c2:, :c2], 0.0)), vs) for amat, vs in zip(amats, vss)]
        invs = [eye - jnp.where((ri // 2 == ci // 2), a_kb, 0.0) for a_kb in akbs]
        m = 2
        while m < ch:
            lvl = (ri // (2 * m) == ci // (2 * m)) & ((ri // m) % 2 == 1) & ((ci // m) % 2 == 0)
            tmps = [_dot(bf(inv), bf(jnp.where(lvl, a_kb, 0.0))) for inv, a_kb in zip(invs, akbs)]
            invs = [inv - _dot(bf(tmp), bf(inv)) for inv, tmp in zip(invs, tmps)]
            m *= 2
        us = [_dot(bf(inv), bf(gmat[:c2] + y)) for inv, gmat, y in zip(invs, gmats, ykk)]
        o2s = [gmat[c2:] + y - _dot(bf(jnp.where(incl, amat[c2:, c2:], 0.0)), bf(u))
               for gmat, y, amat, u in zip(gmats, yrk, amats, us)]
        for pr, (ls, cs, st, vs, u, o2) in enumerate(zip(pairs, css, sts, vss, us, o2s)):
            tot = cs[ch - 1:ch, :]
            e_tail = jnp.exp(tot - cs)
            out_s[ts, ls] = o2[:ch] + o2[ch:]
            so_ref[0, pr] = (st * jnp.exp(tot) + _dot_tn(vs, bf(stack(k_s[ts, ls] * e_tail)))
                             - _dot_tn(bf(u), bf(stack(b_s[ts, ls] * e_tail))))
        return 0

    lax.fori_loop(0, tb // ch, chunk, 0)
    _rwkv_epilogue(out_s[...], gate, bonus, lg_ref, lb_ref, o_ref)


def _rwkv(p, s0, mu, w0, wa2, a0, g2, k_k, k_a, r_k, lnx_g, lnx_b, *, tb, ch, row_lo, row_hi):
    b, t, _ = p.shape
    const = lambda bi, i: (0, 0)
    vec = lambda a: pl.BlockSpec(a.shape, const)
    scr = lambda: pltpu.VMEM((tb, RW_WIDTH), F32)
    return pl.pallas_call(
        functools.partial(_rwkv_kernel, tb=tb, ch=ch, row_lo=row_lo, row_hi=row_hi), grid=(b, t // tb),
        in_specs=[pl.BlockSpec((1, tb, RW_PROJ), lambda bi, i: (bi, i, 0)),
                  pl.BlockSpec((1, PAIRS, LANE, LANE), lambda bi, i: (bi, 0, 0, 0)),
                  vec(mu), vec(w0), vec(wa2), vec(a0), vec(g2), vec(k_k), vec(k_a), vec(r_k), vec(lnx_g), vec(lnx_b)],
        out_specs=[pl.BlockSpec((1, tb, RW_WIDTH), lambda bi, i: (bi, i, 0)),
                   pl.BlockSpec((1, PAIRS, LANE, LANE), lambda bi, i: (bi, 0, 0, 0))],
        out_shape=[jax.ShapeDtypeStruct((b, t, RW_WIDTH), BF16), jax.ShapeDtypeStruct((b, PAIRS, LANE, LANE), F32)],
        scratch_shapes=[pltpu.VMEM((SUBLANE, RW_PROJ), F32)] + [scr() for _ in range(7)],
        compiler_params=_cparams("parallel", "arbitrary"), name="rwkv")(
            p, s0, mu, w0, wa2, a0, g2, k_k, k_a, r_k, lnx_g, lnx_b)


QUAD = 4
QCH = RW_HEAD_DIM
QW = QUAD * RW_HEAD_DIM
QLEVELS = (2, 4, 8, 16, 32)
PREP_CHUNKS = 4


def _tile4(x):
    return jnp.concatenate([x] * QUAD, axis=0)


def _rwkv_quad_kernel(p_ref, s0_ref, mu_ref, w0_ref, wa2_ref, a0_ref, g2_ref, kk_ref, ka_ref, rk_ref, lg_ref, lb_ref,
                      tri_ref, bdm_ref, own_ref, sbm_ref,
                      o_ref, so_ref, carry_s, r_s, k_s, v_s, kk_s, b_s, ld_s, out_s,
                      lhs_c, arb_c, y_c, kt_c, dec_c, *, tb):
    gate, bonus = _rwkv_prologue(p_ref, s0_ref, so_ref, (mu_ref, w0_ref, wa2_ref, a0_ref, g2_ref, kk_ref, ka_ref, rk_ref),
                                 (carry_s, r_s, k_s, v_s, kk_s, b_s, ld_s), tb=tb, row_lo=0, row_hi=tb)
    ch = QCH
    nquad = RW_HEADS // QUAD
    bf = lambda x: x.astype(BF16)

    def bd(x):
        return _tile4(bf(x)) * bdm_ref[0]

    quads = [slice(q * QW, (q + 1) * QW) for q in range(nquad)]

    def prepare(chunks):
        inst = [(c, q, pl.ds(pl.multiple_of(c * ch, ch), ch), quads[q]) for c in chunks for q in range(nquad)]
        strict, incl, eye, lvl1 = sbm_ref[0] > 0, sbm_ref[1] > 0, sbm_ref[2], sbm_ref[3]
        css = []
        for c, q, ts, ls in inst:
            ld = ld_s[ts, ls]
            l1 = bf(ld)
            r1 = ld - l1.astype(F32)
            l2 = bf(r1)
            l3 = bf(r1 - l2.astype(F32))
            css.append(_dot(tri_ref[...], jnp.concatenate([l1, l2, l3], axis=0)))
        a1s, a2s = [], []
        for (c, q, ts, ls), cs in zip(inst, css):
            ld = ld_s[ts, ls]
            tot = cs[ch - 1:ch, :]
            e_inv = jnp.exp(-cs)
            kc, bc = k_s[ts, ls], b_s[ts, ls]
            lhs = bf(jnp.concatenate([kk_s[ts, ls] * jnp.exp(cs - ld), r_s[ts, ls] * jnp.exp(cs)], axis=0))
            lhs_c[c, q] = lhs
            e_tail = jnp.exp(tot - cs)
            kt_c[c, q] = bf(jnp.concatenate([kc * e_tail, bc * e_tail], axis=0))
            dec_c[c, q] = jnp.broadcast_to(jnp.exp(tot), (SUBLANE, QW))
            a1s.append(_dot_nt(lhs, bd(kc * e_inv)))
            a2s.append(_dot_nt(lhs, bd(bc * e_inv)))
        ys, akb4s, invs = [], [], []
        for (c, q, ts, ls), a1, a2 in zip(inst, a1s, a2s):
            a_kk = jnp.where(strict, a1[:ch], 0.0)
            a_rk = jnp.where(incl, a1[ch:], 0.0)
            a_kb = jnp.where(strict, a2[:ch], 0.0)
            arb_c[c, q] = bf(jnp.where(incl, a2[ch:], 0.0))
            ys.append(_dot(bf(jnp.concatenate([a_kk, a_rk], axis=0)), bd(v_s[ts, ls])))
            akb4s.append(_tile4(bf(a_kb)))
            invs.append(eye - a_kb * lvl1)
        for li in range(len(QLEVELS)):
            tmps = [_dot(bf(inv), akb4 * bdm_ref[1 + li]) for inv, akb4 in zip(invs, akb4s)]
            invs = [inv - _dot(bf(tmp), bd(inv)) for inv, tmp in zip(invs, tmps)]
        ws = [_dot(bf(inv), _tile4(lhs_c[c, q][:ch]) * bdm_ref[0]) for (c, q, ts, ls), inv in zip(inst, invs)]
        yks = [_dot(bf(inv), bd(y[:ch])) for inv, y in zip(invs, ys)]
        for (c, q, ts, ls), w, yk, y in zip(inst, ws, yks, ys):
            lhs_c[c, q, 0:ch, :] = bf(w)
            y_c[c, q] = jnp.concatenate([yk, y[ch:]], axis=0)

    nc = tb // ch
    if nc % PREP_CHUNKS == 0:
        def prepare_loop(cc, _):
            prepare([PREP_CHUNKS * cc + d for d in range(PREP_CHUNKS)])
            return 0
        lax.fori_loop(0, nc // PREP_CHUNKS, prepare_loop, 0)
    else:
        lax.fori_loop(0, nc, lambda c, _: (prepare([c]), 0)[1], 0)

    def recur(c, _):
        ts = pl.ds(pl.multiple_of(c * ch, ch), ch)
        sts = [so_ref[0, q] for q in range(nquad)]
        gys = [_dot_nt(lhs_c[c, q], bf(sts[q])) + y_c[c, q] for q in range(nquad)]
        us = [gys[q][:ch] for q in range(nquad)]
        outs = [gys[q][ch:] - _dot(arb_c[c, q], bd(us[q])) for q in range(nquad)]
        upds = [_dot_tn(bf(jnp.concatenate([v_s[ts, quads[q]], -us[q]], axis=0)), kt_c[c, q]) for q in range(nquad)]
        for q in range(nquad):
            out_s[ts, quads[q]] = outs[q]
            so_ref[0, q] = sts[q] * dec_c[c, q][0:1, :] + upds[q] * own_ref[...]
        return 0

    lax.fori_loop(0, nc, recur, 0)
    _rwkv_epilogue(out_s[...], gate, bonus, lg_ref, lb_ref, o_ref)


def _quad_masks():
    ch = QCH
    r = np.arange(QUAD * ch)[:, None]
    c = np.arange(QW)[None, :]
    own = (r // ch) == (c // RW_HEAD_DIM)
    sp, s = r % ch, c % ch
    bdm = [own]
    for m in QLEVELS:
        bdm.append(own & (sp // (2 * m) == s // (2 * m)) & ((sp // m) % 2 == 1) & ((s // m) % 2 == 0))
    t = np.arange(ch)[:, None]
    sbm = [s < t, s <= t, s == t, (t // 2 == s // 2) & (t % 2 == 1) & (s % 2 == 0)]
    tri = np.tile(np.arange(ch)[None, :] <= np.arange(ch)[:, None], (1, 3))
    return (jnp.asarray(tri, BF16), jnp.asarray(np.stack(bdm), BF16), jnp.asarray(own, F32),
            jnp.asarray(np.stack([np.broadcast_to(m, (ch, QW)) for m in sbm]), F32))


def _rwkv_quad(p, s0, mu, w0, wa2, a0, g2, k_k, k_a, r_k, lnx_g, lnx_b, *, tb):
    b, t, _ = p.shape
    nq = RW_HEADS // QUAD
    nc = tb // QCH
    masks = _quad_masks()
    vec = lambda a: pl.BlockSpec(a.shape, lambda bi, i: (0,) * a.ndim)
    scr = lambda: pltpu.VMEM((tb, RW_WIDTH), F32)
    vecs = (mu, w0, wa2, a0, g2, k_k, k_a, r_k, lnx_g, lnx_b) + masks
    return pl.pallas_call(
        functools.partial(_rwkv_quad_kernel, tb=tb), grid=(b, t // tb),
        in_specs=[pl.BlockSpec((1, tb, RW_PROJ), lambda bi, i: (bi, i, 0)),
                  pl.BlockSpec((1, nq, QW, QW), lambda bi, i: (bi, 0, 0, 0))] + [vec(a) for a in vecs],
        out_specs=[pl.BlockSpec((1, tb, RW_WIDTH), lambda bi, i: (bi, i, 0)),
                   pl.BlockSpec((1, nq, QW, QW), lambda bi, i: (bi, 0, 0, 0))],
        out_shape=[jax.ShapeDtypeStruct((b, t, RW_WIDTH), BF16), jax.ShapeDtypeStruct((b, nq, QW, QW), F32)],
        scratch_shapes=[pltpu.VMEM((SUBLANE, RW_PROJ), F32)] + [scr() for _ in range(7)] + [
            pltpu.VMEM((nc, nq, 2 * QCH, QW), BF16),
            pltpu.VMEM((nc, nq, QCH, QW), BF16), pltpu.VMEM((nc, nq, 2 * QCH, QW), F32),
            pltpu.VMEM((nc, nq, 2 * QCH, QW), BF16), pltpu.VMEM((nc, nq, SUBLANE, QW), F32)],
        compiler_params=_cparams("parallel", "arbitrary"), name="rwkv_quad")(p, s0, *vecs)


def _quads_to_state(sq):
    b = sq.shape[0]
    s6 = sq.reshape(b, RW_HEADS // QUAD, QUAD, RW_HEAD_DIM, QUAD, RW_HEAD_DIM)
    return jnp.stack([s6[:, :, h, :, h, :] for h in range(QUAD)], axis=2).reshape(b, RW_HEADS, RW_HEAD_DIM, RW_HEAD_DIM)


def _mix_kernel(x_ref, orw_ref, omla_ref, g_ref, wa_ref, wb_ref, o_ref):
    om = _rms(omla_ref[...], g_ref[...]).astype(BF16)
    o_ref[...] = x_ref[...] + _dot(orw_ref[...], wa_ref[...]) + _dot(om, wb_ref[...])


def _mix(x, orw, omla, g, wa, wb, *, tm):
    n, d = x.shape
    row = lambda i: (i, 0)
    const = lambda i: (0, 0)
    return pl.pallas_call(
        _mix_kernel, grid=(n // tm,),
        in_specs=[pl.BlockSpec((tm, d), row), pl.BlockSpec((tm, orw.shape[1]), row), pl.BlockSpec((tm, omla.shape[1]), row),
                  pl.BlockSpec(g.shape, const), pl.BlockSpec(wa.shape, const), pl.BlockSpec(wb.shape, const)],
        out_specs=pl.BlockSpec((tm, d), row), out_shape=jax.ShapeDtypeStruct((n, d), F32),
        compiler_params=_cparams("parallel"), name="mix_out")(x, orw, omla, g, wa, wb)


def _norm_mm_kernel(x_ref, g_ref, w_ref, o_ref):
    o_ref[...] = _dot(_rms(x_ref[...], g_ref[...]).astype(BF16), w_ref[...]).astype(o_ref.dtype)


def _norm_mm(x, g, w, out_dtype, *, tm):
    n, d = x.shape
    m = w.shape[1]
    return pl.pallas_call(
        _norm_mm_kernel, grid=(n // tm,),
        in_specs=[pl.BlockSpec((tm, d), lambda i: (i, 0)), pl.BlockSpec(g.shape, lambda i: (0, 0)),
                  pl.BlockSpec(w.shape, lambda i: (0, 0))],
        out_specs=pl.BlockSpec((tm, m), lambda i: (i, 0)), out_shape=jax.ShapeDtypeStruct((n, m), out_dtype),
        compiler_params=_cparams("parallel"), name="norm_mm")(x, g, w)


def _mm_res_kernel(a_ref, w_ref, x_ref, o_ref):
    o_ref[...] = x_ref[...] + _dot(a_ref[...], w_ref[...])


def _mm_res(a, w, x, *, tm):
    n, d = x.shape
    return pl.pallas_call(
        _mm_res_kernel, grid=(n // tm,),
        in_specs=[pl.BlockSpec((tm, a.shape[1]), lambda i: (i, 0)), pl.BlockSpec(w.shape, lambda i: (0, 0)),
                  pl.BlockSpec((tm, d), lambda i: (i, 0))],
        out_specs=pl.BlockSpec((tm, d), lambda i: (i, 0)), out_shape=jax.ShapeDtypeStruct((n, d), F32),
        compiler_params=_cparams("parallel"), name="mm_res")(a, w, x)


def _memattn_kernel(q_ref, k_ref, v_ref, o_ref):
    dh = q_ref.shape[2] // MEM_HEADS
    scale = dh ** -0.5
    for h in range(MEM_HEADS):
        cs = slice(h * dh, (h + 1) * dh)
        s = _dot_nt(q_ref[0, :, cs], k_ref[0, :, cs].astype(BF16)) * scale
        p = jnp.exp(s - jnp.max(s, axis=-1, keepdims=True))
        l = jnp.sum(p, axis=-1, keepdims=True)
        o_ref[0, :, cs] = (_dot(p.astype(BF16), v_ref[0, :, cs].astype(BF16)) / l).astype(o_ref.dtype)


def _memattn_tiled_kernel(q_ref, k_ref, v_ref, o_ref, *, nm):
    dh = q_ref.shape[2] // MEM_HEADS
    nj = dh // LANE
    scale = dh ** -0.5
    for h in range(MEM_HEADS):
        rows = [pl.ds(j * MEM_HEADS + h, nm, stride=nj * MEM_HEADS) for j in range(nj)]
        s = sum(_dot_nt(q_ref[0, :, h * dh + j * LANE:h * dh + (j + 1) * LANE], k_ref[0, rows[j], :].astype(BF16))
                for j in range(nj)) * scale
        p = jnp.exp(s - jnp.max(s, axis=-1, keepdims=True))
        l = jnp.sum(p, axis=-1, keepdims=True)
        pb = p.astype(BF16)
        for j in range(nj):
            o_ref[0, :, h * dh + j * LANE:h * dh + (j + 1) * LANE] = (
                _dot(pb, v_ref[0, rows[j], :].astype(BF16)) / l).astype(o_ref.dtype)


def _memattn_tiled(q, mk, mv, *, nm):
    b, t, d = q.shape
    mem_spec = pl.BlockSpec((1,) + mk.shape[1:], lambda bi: (bi, 0, 0))
    return pl.pallas_call(
        functools.partial(_memattn_tiled_kernel, nm=nm), grid=(b,),
        in_specs=[pl.BlockSpec((1, t, d), lambda bi: (bi, 0, 0)), mem_spec, mem_spec],
        out_specs=pl.BlockSpec((1, t, d), lambda bi: (bi, 0, 0)), out_shape=jax.ShapeDtypeStruct((b, t, d), BF16),
        compiler_params=_cparams("parallel"), name="mem_attn_tiled")(q, mk, mv)


def _memattn(q, mk, mv, *, tm):
    b, t, d = q.shape
    mem_spec = pl.BlockSpec((1,) + mk.shape[1:], lambda bi, i: (bi,) + (0,) * (mk.ndim - 1))
    return pl.pallas_call(
        _memattn_kernel, grid=(b, t // tm),
        in_specs=[pl.BlockSpec((1, tm, d), lambda bi, i: (bi, i, 0)), mem_spec, mem_spec],
        out_specs=pl.BlockSpec((1, tm, d), lambda bi, i: (bi, i, 0)), out_shape=jax.ShapeDtypeStruct((b, t, d), BF16),
        compiler_params=_cparams("parallel", "parallel"), name="mem_attn")(q, mk, mv)


def _memblock_kernel(x_ref, orw_ref, omla_ref, gmo_ref, wa_ref, wb_ref, gq_ref, wq_ref, k_ref, v_ref, wo_ref, o_ref):
    om = _rms(omla_ref[0], gmo_ref[...]).astype(BF16)
    x1 = x_ref[0] + _dot(orw_ref[0], wa_ref[...]) + _dot(om, wb_ref[...])
    q = _dot(_rms(x1, gq_ref[...]).astype(BF16), wq_ref[...]).astype(BF16)
    dh = q.shape[1] // MEM_HEADS
    scale = dh ** -0.5
    outs = []
    for h in range(MEM_HEADS):
        cs = slice(h * dh, (h + 1) * dh)
        s = _dot_nt(q[:, cs], k_ref[0, :, cs].astype(BF16)) * scale
        p = jnp.exp(s - jnp.max(s, axis=-1, keepdims=True))
        l = jnp.sum(p, axis=-1, keepdims=True)
        outs.append((_dot(p.astype(BF16), v_ref[0, :, cs].astype(BF16)) / l).astype(BF16))
    o_ref[0] = x1 + _dot(jnp.concatenate(outs, axis=1), wo_ref[...])


def _memblock(x, orw, omla, gmo, wa, wb, gq, wq, mk, mv, wo, *, tm):
    b, t, d = x.shape
    row = lambda bi, i: (bi, i, 0)
    seq = lambda bi, i: (bi, 0, 0)
    const = lambda bi, i: (0, 0)
    cs = lambda a: pl.BlockSpec(a.shape, const)
    return pl.pallas_call(
        _memblock_kernel, grid=(b, t // tm),
        in_specs=[pl.BlockSpec((1, tm, d), row), pl.BlockSpec((1, tm, orw.shape[2]), row), pl.BlockSpec((1, tm, omla.shape[2]), row),
                  cs(gmo), cs(wa), cs(wb), cs(gq), cs(wq), pl.BlockSpec((1,) + mk.shape[1:], seq),
                  pl.BlockSpec((1,) + mv.shape[1:], seq), cs(wo)],
        out_specs=pl.BlockSpec((1, tm, d), row), out_shape=jax.ShapeDtypeStruct((b, t, d), F32),
        compiler_params=_cparams("parallel", "parallel"), name="mem_block")(x, orw, omla, gmo, wa, wb, gq, wq, mk, mv, wo)


FFN_CHUNK = 256
FFN_GROUP = 4


def _ffn_kernel(x_ref, prev_ref, g_ref, wup_ref, cw_ref, cb_ref, wdn_ref, gf_ref, y_ref, u_ref, act_s, *, tm, nch, prev_rows):
    i = pl.program_id(1)
    x = x_ref[0]
    h = _rms(x, g_ref[...]).astype(BF16)
    rows = lax.broadcasted_iota(jnp.int32, (tm, 1), 0)
    if prev_rows is None:
        @pl.when(i == 0)
        def _():
            u_ref[0] = prev_ref[0]
    else:
        hist = (rows % SROWS) < prev_rows

    def up(c):
        return _dot(h, wup_ref[c]), _dot(h, wup_ref[c + nch])

    def conv(c, u):
        if prev_rows is None:
            tail = u_ref[0, c]
            u_ref[0, c] = u[tm - SUBLANE:tm, :]
            p1, p2 = tail[SUBLANE - 1:SUBLANE, :], tail[SUBLANE - 2:SUBLANE - 1, :]
            u1, u2 = pltpu.roll(u, 1, axis=0), pltpu.roll(u, 2, axis=0)
            r8 = rows[:SUBLANE]
            u1 = jnp.concatenate([jnp.where(r8 == 0, p1, u1[:SUBLANE]), u1[SUBLANE:]], axis=0)
            u2 = jnp.concatenate([jnp.where(r8 == 0, p2, jnp.where(r8 == 1, p1, u2[:SUBLANE])), u2[SUBLANE:]], axis=0)
        else:
            u = jnp.where(hist, prev_ref[c], u)
            u_ref[c] = u
            u1 = pltpu.roll(u, 1, axis=0)
            u2 = pltpu.roll(u, 2, axis=0)
        cw = cw_ref[c]
        return cb_ref[c] + cw[0:1, :] * u2 + cw[1:2, :] * u1 + cw[2:3, :] * u

    def gated(c, ug, uv):
        gate = conv(c, ug)
        val = conv(c + nch, uv)
        act_s[c] = (gate * jax.nn.sigmoid(gate) * val).astype(BF16)

    def group(cs):
        ups = [up(c) for c in cs]
        for c, (ug, uv) in zip(cs, ups):
            gated(c, ug, uv)

    ngrp = nch // FFN_GROUP

    def body(gi, _):
        group([gi * FFN_GROUP + d for d in range(FFN_GROUP)])
        return 0

    lax.fori_loop(0, ngrp, body, 0)
    if nch % FFN_GROUP:
        group(list(range(ngrp * FFN_GROUP, nch)))
    f = _dot(jnp.concatenate([act_s[c] for c in range(nch)], axis=1), wdn_ref[...])
    y_ref[0] = _rms(x + f, gf_ref[...])


def _ffn(x, prev, g, wup, cw, cb, wdn, gf, *, tm, prev_rows):
    b, t, d = x.shape
    nch = wup.shape[0] // 2
    c3 = lambda bi, i: (0, 0, 0)
    if prev_rows is None:
        prev_spec = pl.BlockSpec((1, 2 * nch, SUBLANE, FFN_CHUNK), lambda bi, i: (bi, 0, 0, 0))
        u_spec = pl.BlockSpec((1, 2 * nch, SUBLANE, FFN_CHUNK), lambda bi, i: (bi, 0, 0, 0))
        u_shape = jax.ShapeDtypeStruct((b, 2 * nch, SUBLANE, FFN_CHUNK), F32)
    else:
        prev_spec = pl.BlockSpec((2 * nch, tm, FFN_CHUNK), lambda bi, i: (0, bi * (t // tm) + i, 0))
        u_spec = pl.BlockSpec((2 * nch, tm, FFN_CHUNK), lambda bi, i: (0, bi * (t // tm) + i, 0))
        u_shape = jax.ShapeDtypeStruct((2 * nch, b * t, FFN_CHUNK), F32)
    return pl.pallas_call(
        functools.partial(_ffn_kernel, tm=tm, nch=nch, prev_rows=prev_rows), grid=(b, t // tm),
        in_specs=[pl.BlockSpec((1, tm, d), lambda bi, i: (bi, i, 0)), prev_spec, pl.BlockSpec(g.shape, lambda bi, i: (0, 0)),
                  pl.BlockSpec(wup.shape, c3), pl.BlockSpec(cw.shape, c3), pl.BlockSpec(cb.shape, c3),
                  pl.BlockSpec(wdn.shape, lambda bi, i: (0, 0)), pl.BlockSpec(gf.shape, lambda bi, i: (0, 0))],
        out_specs=[pl.BlockSpec((1, tm, d), lambda bi, i: (bi, i, 0)), u_spec],
        out_shape=[jax.ShapeDtypeStruct((b, t, d), F32), u_shape],
        scratch_shapes=[pltpu.VMEM((nch, tm, FFN_CHUNK), BF16)],
        compiler_params=_cparams("parallel", "arbitrary"), name="conv_ffn")(x, prev, g, wup, cw, cb, wdn, gf)


def _rope_tables(pos, scale):
    half = QK_ROPE // 2
    inv = 1.0 / (ROPE_THETA ** (np.arange(half, dtype=np.float64) / half))
    ang = np.asarray(pos, np.float64)[:, None] * inv[None, :]
    cos, sin = np.cos(ang), np.sin(ang)
    n = len(pos)
    c = np.zeros((n, HEAD_PAD))
    s = np.zeros((n, HEAD_PAD))
    c[:, :QK_NOPE] = 1.0
    c[:, ROPE_LO:ROPE_LO + half] = cos
    c[:, ROPE_LO + half:ROPE_LO + QK_ROPE] = cos
    s[:, ROPE_LO:ROPE_LO + half] = -sin
    s[:, ROPE_LO + half:ROPE_LO + QK_ROPE] = sin
    return (jnp.asarray(c * scale, F32), jnp.asarray(s * scale, F32), jnp.asarray(c, F32), jnp.asarray(s, F32))


def _swap_halves(w):
    half = w.shape[-1] // 2
    return jnp.concatenate([w[..., half:], w[..., :half]], axis=-1)


def _prep_weights(w_in, w_uq, w_ukv, rw_w2, rw_a2, w_up, conv_w, conv_b, w_down):
    d = w_in.shape[0]
    z = lambda *s: jnp.zeros(s, F32)
    w_kr = w_in[:, Q_LORA + KV_LORA:MLA_PROJ]
    pad_head = lambda w: jnp.concatenate([z(d, ROPE_LO), w, z(d, HEAD_PAD - ROPE_LO - QK_ROPE)], axis=1)
    w1 = jnp.concatenate([w_in[:, :Q_LORA + KV_LORA], pad_head(w_kr), pad_head(_swap_halves(w_kr)), w_in[:, MLA_PROJ:]],
                         axis=1).astype(BF16)
    wq3 = w_uq.reshape(Q_LORA, MLA_HEADS, QK_NOPE + QK_ROPE)
    zq = lambda n: z(Q_LORA, MLA_HEADS, n)
    q_plain = jnp.concatenate([wq3, zq(HEAD_PAD - QK_NOPE - QK_ROPE)], axis=2)
    q_swap = jnp.concatenate([zq(QK_NOPE), _swap_halves(wq3[..., QK_NOPE:]), zq(HEAD_PAD - QK_NOPE - QK_ROPE)], axis=2)
    wq = jnp.concatenate([q_plain.reshape(Q_LORA, -1), q_swap.reshape(Q_LORA, -1)], axis=1).astype(BF16)
    wkv3 = w_ukv.reshape(KV_LORA, MLA_HEADS, QK_NOPE + V_DIM)
    zk = z(KV_LORA, MLA_HEADS, HEAD_PAD - QK_NOPE)
    wk = jnp.concatenate([wkv3[..., :QK_NOPE], zk], axis=2).reshape(KV_LORA, -1).astype(BF16)
    wv = wkv3[..., QK_NOPE:].reshape(KV_LORA, -1).T.astype(BF16)
    w_uk = jnp.transpose(wkv3[..., :QK_NOPE], (1, 2, 0))
    w_uk = jnp.concatenate([w_uk, z(MLA_HEADS, HEAD_PAD - QK_NOPE, KV_LORA)], axis=1).astype(BF16)
    w_uv = jnp.transpose(wkv3[..., QK_NOPE:], (1, 0, 2)).astype(BF16)
    wa2 = jnp.concatenate([jnp.concatenate([rw_w2, z(A_LORA, RW_WIDTH)], axis=0),
                           jnp.concatenate([z(DECAY_LORA, RW_WIDTH), rw_a2], axis=0)], axis=1).astype(BF16)
    f2 = w_up.shape[1]
    nch2 = f2 // FFN_CHUNK
    wup = jnp.transpose(w_up.reshape(d, nch2, FFN_CHUNK), (1, 0, 2)).astype(BF16)
    cw = jnp.transpose(conv_w.reshape(CONV_W, nch2, FFN_CHUNK), (1, 0, 2))
    cw = jnp.concatenate([cw, z(nch2, SUBLANE - CONV_W, FFN_CHUNK)], axis=1)
    cb = conv_b.reshape(nch2, 1, FFN_CHUNK)
    wdn = w_down.astype(BF16)
    return w1, wq, wk, wv, w_uk, w_uv, wa2, wup, cw, cb, wdn


def _state_to_pairs(s):
    b = s.shape[0]
    s = s.reshape(b, PAIRS, 2, RW_HEAD_DIM, RW_HEAD_DIM)
    zz = jnp.zeros_like(s[:, :, 0])
    top = jnp.concatenate([s[:, :, 0], zz], axis=-1)
    bot = jnp.concatenate([zz, s[:, :, 1]], axis=-1)
    return jnp.concatenate([top, bot], axis=-2)


def _pairs_to_state(s):
    b = s.shape[0]
    h0 = s[:, :, :RW_HEAD_DIM, :RW_HEAD_DIM]
    h1 = s[:, :, RW_HEAD_DIM:, RW_HEAD_DIM:]
    return jnp.stack([h0, h1], axis=2).reshape(b, RW_HEADS, RW_HEAD_DIM, RW_HEAD_DIM)


def _pick(n, pref):
    for t in pref:
        if n % t == 0:
            return t
    return n


def kernel(x_prompt, x_sample, cache_mla_latent, cache_mla_krope, cache_mem_k, cache_mem_v, state_rwkv, state_rwkv_shift, state_ffn_conv, page_table, mem_prompt, g_mix, w_in, q_norm_g, kv_norm_g, w_uq, w_ukv, g_mla_out, rw_mu, rw_w0, rw_w2, rw_a0, rw_a2, rw_g2, rw_k_k, rw_k_a, rw_r_k, rw_lnx_g, rw_lnx_b, w_o, g_mem_q, g_mem_kv, w_mq, w_mk, w_mv, w_mo, g_ffn, w_up, conv_w, conv_b, w_down, g_final):
    depth = w_in.shape[0]
    assert depth == 1, "single-layer step"
    bp, tp, d = x_prompt.shape
    bs, ts, _ = x_sample.shape
    npages = page_table.shape[1]
    past_len = npages * PAGE_SIZE
    row2 = lambda a: a.reshape(1, -1)
    l = 0
    w1, wq, wk, wv, w_uk, w_uv, wa2, wup, cw, cb, wdn = _prep_weights(
        w_in[l], w_uq[l], w_ukv[l], rw_w2[l], rw_a2[l], w_up[l], conv_w[l], conv_b[l], w_down[l])
    wo_a, wo_b = w_o[l, :RW_WIDTH].astype(BF16), w_o[l, RW_WIDTH:].astype(BF16)
    wmq, wmo = w_mq[l].astype(BF16), w_mo[l].astype(BF16)
    wmkv = jnp.concatenate([w_mk[l], w_mv[l]], axis=1).astype(BF16)
    g2 = rw_g2[l].astype(BF16)
    nch2 = wup.shape[0]
    n_mem_s = cache_mem_k.shape[2]
    rw_vecs = (row2(rw_mu[l]), row2(rw_w0[l]), wa2, row2(rw_a0[l]), g2, row2(rw_k_k[l]), row2(rw_k_a[l]),
               row2(rw_r_k[l]), row2(rw_lnx_g[l]), row2(rw_lnx_b[l]))

    def after_attention(x2, orw, omla, mem_k, mem_v, conv_in, b, t, tm, tm_mem, tm_ffn, prev_rows):
        if prev_rows is None:
            xm = _memblock(x2.reshape(b, t, d), orw.reshape(b, t, -1), omla.reshape(b, t, -1), row2(g_mla_out[l]),
                           wo_a, wo_b, row2(g_mem_q[l]), wmq, mem_k, mem_v, wmo, tm=tm)
        else:
            x1 = _mix(x2, orw, omla, row2(g_mla_out[l]), wo_a, wo_b, tm=tm)
            qm = _norm_mm(x1, row2(g_mem_q[l]), wmq, BF16, tm=tm)
            om = _memattn_tiled(qm.reshape(b, t, d), mem_k, mem_v, nm=n_mem_s)
            xm = _mm_res(om.reshape(b * t, d), wmo, x1, tm=tm)
        fb, ft = (b, t) if prev_rows is None else (1, b * t)
        return _ffn(xm.reshape(fb, ft, d), conv_in, row2(g_ffn[l]), wup, cw, cb, wdn, row2(g_final), tm=tm_ffn,
                    prev_rows=prev_rows)

    n_p = bp * tp
    tm_p = _pick(tp, (512, 256, 128, 64, 32, 16, 8))
    tabs_p = _rope_tables(np.arange(tp), MLA_SCALE * LOG2E)
    xp2 = x_prompt.reshape(n_p, d)
    tq = _pick(tm_p, (FLASH_T, 256, 128))
    q_p, lat_p, kr_p, prw_p, k_p, v_p = _inproj(
        xp2, row2(g_mix[l]), w1, row2(q_norm_g[l]), row2(kv_norm_g[l]), wq, tabs_p, wk, wv, tm=tm_p, with_kv=True, seq=tp, vblock=tq)
    omla_p = _flash(q_p, k_p, v_p, batch=bp, seq=tp, tq=tq)
    tb_p = _pick(tp, (512, 256, 128, 64, 32, 16, 8))
    if tb_p % QCH == 0:
        zeros_state = jnp.zeros((bp, RW_HEADS // QUAD, QW, QW), F32)
        orw_p, st_p = _rwkv_quad(prw_p.reshape(bp, tp, RW_PROJ), zeros_state, *rw_vecs, tb=tb_p)
        st_p = _quads_to_state(st_p)
    else:
        zeros_state = jnp.zeros((bp, PAIRS, LANE, LANE), F32)
        orw_p, st_p = _rwkv(prw_p.reshape(bp, tp, RW_PROJ), zeros_state, *rw_vecs, tb=tb_p,
                            ch=_pick(tb_p, (32, 16, 8)), row_lo=0, row_hi=tb_p)
        st_p = _pairs_to_state(st_p)
    mkv = _norm_mm(mem_prompt.reshape(-1, d), row2(g_mem_kv[l]), wmkv, F32, tm=_pick(mem_prompt.shape[0] * mem_prompt.shape[1], (512, 256, 128, 8)))
    n_mem = mem_prompt.shape[1]
    mk_p = mkv[:, :d].reshape(bp, n_mem, d)
    mv_p = mkv[:, d:].reshape(bp, n_mem, d)
    conv0_p = jnp.zeros((bp, nch2, SUBLANE, FFN_CHUNK), F32)
    y_p, u_p = after_attention(xp2, orw_p.reshape(n_p, RW_WIDTH), omla_p, mk_p, mv_p, conv0_p, bp, tp, tm_p, tm_p, tm_p, None)
    conv_p = jnp.transpose(u_p[:, :, SUBLANE - (CONV_W - 1):, :], (0, 2, 1, 3)).reshape(bp, CONV_W - 1, nch2 * FFN_CHUNK)

    n_s = bs * SROWS
    xs3 = jnp.pad(x_sample, ((0, 0), (SLO, SROWS - SLO - ts), (0, 0)))
    pos_s = np.tile(np.concatenate([np.zeros(SLO), past_len + np.arange(ts), np.zeros(SROWS - SLO - ts)]), bs)
    tm_s = _pick(n_s, (1024, 512, 256, 128, 64, 32, 16, 8))
    tabs_s = _rope_tables(pos_s[:tm_s], MLA_SCALE)
    q_s, lat_s, kr_s, prw_s = _inproj(
        xs3.reshape(n_s, d), row2(g_mix[l]), w1, row2(q_norm_g[l]), row2(kv_norm_g[l]), wq, tabs_s, None, None,
        tm=tm_s, with_kv=False)
    qabs = _bmm_cols(q_s, w_uk, BF16)
    qabs = qabs.reshape(MLA_HEADS, bs, SROWS, KV_LORA)[:, :, SLO:SLO + ts]
    qabs = jnp.transpose(qabs, (1, 0, 2, 3)).reshape(bs, MLA_HEADS * ts, KV_LORA)
    qrope = q_s.reshape(bs, SROWS, MLA_HEADS, HEAD_PAD)[:, SLO:SLO + ts, :, ROPE_LO:ROPE_LO + QK_ROPE]
    qrope = jnp.transpose(qrope, (0, 2, 1, 3)).reshape(bs, MLA_HEADS * ts, QK_ROPE)
    olat = _paged(page_table, qabs, qrope, lat_s.reshape(bs, SROWS, KV_LORA), kr_s.reshape(bs, SROWS, QK_ROPE),
                  cache_mla_latent[l], jnp.swapaxes(cache_mla_krope[l], 1, 2), n_new=ts)
    olat = jnp.transpose(olat.reshape(bs, MLA_HEADS, ts, KV_LORA), (1, 0, 2, 3))
    olat = jnp.pad(olat, ((0, 0), (0, 0), (SLO, SROWS - SLO - ts), (0, 0))).reshape(MLA_HEADS, n_s, KV_LORA)
    omla_s = _bmm(olat, w_uv, F32)
    omla_s = jnp.transpose(omla_s, (1, 0, 2)).reshape(n_s, MLA_HEADS * V_DIM)
    prw_s3 = prw_s.reshape(bs, SROWS, RW_PROJ).at[:, SLO - 1, :].set(state_rwkv_shift[l])
    orw_s, st_s = _rwkv(prw_s3, _state_to_pairs(state_rwkv[l]), *rw_vecs, tb=SROWS, ch=SROWS, row_lo=SLO, row_hi=SLO + ts)
    hist = jnp.transpose(state_ffn_conv[l].reshape(bs, CONV_W - 1, nch2, FFN_CHUNK), (2, 0, 1, 3))
    hist = jnp.pad(hist, ((0, 0), (0, 0), (SLO - (CONV_W - 1), SROWS - SLO), (0, 0))).reshape(nch2, n_s, FFN_CHUNK)
    def mem_rows(c):
        _, nm, hh, dh = c.shape
        return jnp.transpose(c.reshape(bs, nm, hh, dh // LANE, LANE), (0, 1, 3, 2, 4)).reshape(bs, nm * hh * (dh // LANE), LANE)
    mk_s, mv_s = mem_rows(cache_mem_k[l]), mem_rows(cache_mem_v[l])
    y_s, u_s = after_attention(xs3.reshape(n_s, d), orw_s.reshape(n_s, RW_WIDTH), omla_s, mk_s, mv_s, hist,
                               bs, SROWS, tm_s, SROWS, _pick(n_s, (256, 128, 64, 32, 16, 8)), SLO)
    u_s = u_s.reshape(nch2, bs, SROWS, FFN_CHUNK)[:, :, SLO + ts - (CONV_W - 1):SLO + ts]
    conv_s = jnp.transpose(u_s, (1, 2, 0, 3)).reshape(bs, CONV_W - 1, nch2 * FFN_CHUNK)

    real = lambda a, w: a.reshape(bs, SROWS, w)[:, SLO:SLO + ts]
    mem5 = lambda a: a.reshape(1, bp, n_mem, MEM_HEADS, d // MEM_HEADS)
    return (y_p, real(y_s, d),
            lat_p.reshape(1, bp, tp, KV_LORA), kr_p.reshape(1, bp, tp, QK_ROPE), mem5(mk_p), mem5(mv_p),
            st_p[None], prw_p.reshape(bp, tp, RW_PROJ)[:, -1][None], conv_p[None],
            real(lat_s, KV_LORA)[None], real(kr_s, QK_ROPE)[None], _pairs_to_state(st_s)[None],
            prw_s.reshape(bs, SROWS, RW_PROJ)[:, SLO + ts - 1][None], conv_s[None])
```

```python
import functools

import numpy as np
import jax
import jax.numpy as jnp
from jax import lax
from jax.experimental import pallas as pl
from jax.experimental.pallas import tpu as pltpu

F32 = jnp.float32
BF16 = jnp.bfloat16

RW_HEADS = 8
RW_HEAD_DIM = 64
RW_WIDTH = 512
DECAY_LORA = 64
A_LORA = 64
GATE_LORA = 128
RW_PROJ = 3 * RW_WIDTH + DECAY_LORA + A_LORA + GATE_LORA
LNX_EPS = 64e-5
MLA_HEADS = 8
QK_NOPE = 64
QK_ROPE = 32
V_DIM = 64
Q_LORA = 384
KV_LORA = 256
MLA_PROJ = Q_LORA + KV_LORA + QK_ROPE
MLA_SCALE = (QK_NOPE + QK_ROPE) ** -0.5
ROPE_THETA = 10000.0
MEM_HEADS = 4
CONV_W = 3
NORM_EPS = 1e-6
PAGE_SIZE = 128

LANE = 128
SUBLANE = 8
VMEM_LIMIT = 56 * 1024 * 1024
HEAD_PAD = LANE
ROPE_LO = QK_NOPE
SROWS = 8
SLO = 2
NEG = -1e30
LOG2E = float(np.log2(np.e))
HI = lax.Precision.HIGHEST


def _cparams(*sem):
    return pltpu.CompilerParams(dimension_semantics=sem, vmem_limit_bytes=VMEM_LIMIT)


def _rms(x, g, eps=NORM_EPS):
    return x * lax.rsqrt(jnp.mean(x * x, axis=-1, keepdims=True) + eps) * g


def _dot(a, b):
    return jnp.dot(a, b, preferred_element_type=F32)


def _dot_nt(a, b, precision=None):
    return lax.dot_general(a, b, (((1,), (1,)), ((), ())), preferred_element_type=F32, precision=precision)


def _dot_tn(a, b, precision=None):
    return lax.dot_general(a, b, (((0,), (0,)), ((), ())), preferred_element_type=F32, precision=precision)


def _doth(a, b):
    return jnp.dot(a, b, preferred_element_type=F32, precision=HI)


C_Q0, C_KV0, C_KR0, C_KRS0, C_RW0 = 0, Q_LORA, Q_LORA + KV_LORA, Q_LORA + KV_LORA + LANE, Q_LORA + KV_LORA + 2 * LANE
W1_COLS = C_RW0 + RW_PROJ


def _inproj_kernel(x_ref, g_ref, w1_ref, qg_ref, kvg_ref, wq_ref, cq_ref, sq_ref, ck_ref, sk_ref, *rest, with_kv):
    if with_kv:
        wk_ref, wv_ref, q_out, lat_out, kr_out, prw_out, k_out, v_out = rest
    else:
        q_out, lat_out, kr_out, prw_out = rest
    h = _rms(x_ref[...], g_ref[...]).astype(BF16)
    prw_out[...] = _dot(h, w1_ref[:, C_RW0:W1_COLS])
    pm = _dot(h, w1_ref[:, 0:C_RW0])
    cqn = _rms(pm[:, C_Q0:C_KV0], qg_ref[...]).astype(BF16)
    lat = _rms(pm[:, C_KV0:C_KR0], kvg_ref[...])
    lat_out[...] = lat
    krope = pm[:, C_KR0:C_KRS0] * ck_ref[...] + pm[:, C_KRS0:C_RW0] * sk_ref[...]
    kr_out[...] = krope[:, ROPE_LO:ROPE_LO + QK_ROPE]
    q12 = _dot(cqn, wq_ref[...])
    nq = MLA_HEADS * HEAD_PAD
    cq, sq = cq_ref[...], sq_ref[...]
    for hd in range(MLA_HEADS):
        a, b = hd * HEAD_PAD, (hd + 1) * HEAD_PAD
        q_out[:, a:b] = (q12[:, a:b] * cq + q12[:, nq + a:nq + b] * sq).astype(BF16)
    if with_kv:
        latb = lat.astype(BF16)
        kn = _dot(latb, wk_ref[...])
        for hd in range(MLA_HEADS):
            a, b = hd * HEAD_PAD, (hd + 1) * HEAD_PAD
            k_out[:, a:b] = (kn[:, a:b] + krope).astype(BF16)
        vt = _dot_nt(wv_ref[...], latb).astype(BF16)
        for jb in range(v_out.shape[1]):
            v_out[0, jb] = vt[:, jb * v_out.shape[3]:(jb + 1) * v_out.shape[3]]


def _inproj(x, g, w1, qg, kvg, wq, tabs, wk, wv, *, tm, with_kv, seq=None, vblock=None):
    n, d = x.shape
    ttab = tabs[0].shape[0]
    nt = ttab // tm
    row = lambda i: (i, 0)
    const = lambda i: (0, 0)
    tab = lambda i: (i % nt, 0)
    in_specs = [pl.BlockSpec((tm, d), row), pl.BlockSpec(g.shape, const), pl.BlockSpec(w1.shape, const),
                pl.BlockSpec(qg.shape, const), pl.BlockSpec(kvg.shape, const), pl.BlockSpec(wq.shape, const)]
    in_specs += [pl.BlockSpec((tm, LANE), tab)] * 4
    args = [x, g, w1, qg, kvg, wq, *tabs]
    nq = MLA_HEADS * HEAD_PAD
    out_shape = [jax.ShapeDtypeStruct((n, nq), BF16), jax.ShapeDtypeStruct((n, KV_LORA), F32),
                 jax.ShapeDtypeStruct((n, QK_ROPE), F32), jax.ShapeDtypeStruct((n, RW_PROJ), F32)]
    out_specs = [pl.BlockSpec((tm, nq), row), pl.BlockSpec((tm, KV_LORA), row),
                 pl.BlockSpec((tm, QK_ROPE), row), pl.BlockSpec((tm, RW_PROJ), row)]
    if with_kv:
        in_specs += [pl.BlockSpec(wk.shape, const), pl.BlockSpec(wv.shape, const)]
        args += [wk, wv]
        nv = MLA_HEADS * V_DIM
        npb = seq // tm
        out_shape += [jax.ShapeDtypeStruct((n, nq), BF16), jax.ShapeDtypeStruct((n // seq, seq // vblock, nv, vblock), BF16)]
        out_specs += [pl.BlockSpec((tm, nq), row),
                      pl.BlockSpec((1, tm // vblock, nv, vblock), lambda i: (i // npb, i % npb, 0, 0))]
    return pl.pallas_call(
        functools.partial(_inproj_kernel, with_kv=with_kv), grid=(n // tm,), in_specs=in_specs,
        out_specs=out_specs, out_shape=out_shape, compiler_params=_cparams("parallel"),
        name="inproj_kv" if with_kv else "inproj")(*args)


FLASH_HPB = 4
FLASH_T = 512
FLASH_LROWS = 16


def _flash_kernel(q_ref, k_ref, vt_ref, o_ref, st_a, st_b, m_s, acc_s, *, tq):
    tk = tq
    qi = pl.program_id(2)
    assert tq == tk
    nfull = qi
    qs = [q_ref[:, h * HEAD_PAD:(h + 1) * HEAD_PAD] for h in range(FLASH_HPB)]
    kidx = lax.broadcasted_iota(jnp.int32, (tk, tq), 0)
    qidx = lax.broadcasted_iota(jnp.int32, (tk, tq), 1)

    ones = jnp.ones((FLASH_LROWS, tk), BF16)

    def scores(slot, j):
        start = pl.multiple_of(j * tk, tk)
        for h in range(FLASH_HPB):
            slot[h] = _dot_nt(k_ref[pl.ds(start, tk), h * HEAD_PAD:(h + 1) * HEAD_PAD], qs[h])

    def step(slot, j, diagonal):
        for h in range(FLASH_HPB):
            m = m_s[h]
            st = slot[h]
            if diagonal:
                st = jnp.where(kidx <= qidx, st, NEG)
            m_new = jnp.maximum(m, jnp.max(st, axis=0, keepdims=True))
            alpha = jnp.exp2(m - m_new)
            p = jnp.exp2(st - m_new).astype(BF16)
            vt1 = jnp.concatenate([vt_ref[0, j, h * V_DIM:(h + 1) * V_DIM, :], ones], axis=0)
            acc_s[h] = alpha * acc_s[h] + _dot(vt1, p)
            m_s[h] = m_new

    m_s[...] = jnp.full(m_s.shape, NEG, F32)
    acc_s[...] = jnp.zeros(acc_s.shape, F32)

    scores(st_a, 0)

    def pair(jj, _):
        scores(st_b, 2 * jj + 1)
        step(st_a, 2 * jj, False)
        scores(st_a, 2 * jj + 2)
        step(st_b, 2 * jj + 1, False)
        return 0

    npair = nfull // 2
    lax.fori_loop(0, npair, pair, 0)

    @pl.when(nfull % 2 == 1)
    def _():
        scores(st_b, nfull)
        step(st_a, nfull - 1, False)
        step(st_b, nfull, True)

    @pl.when(nfull % 2 == 0)
    def _():
        step(st_a, nfull, True)

    ot = jnp.concatenate([acc_s[h, :V_DIM] / acc_s[h, V_DIM:V_DIM + 1] for h in range(FLASH_HPB)], axis=0)
    o_ref[...] = ot.T


def _flash(q, k, vt, *, batch, seq, tq):
    nqb = seq // tq
    grid = (batch, MLA_HEADS // FLASH_HPB, nqb)
    w = FLASH_HPB * HEAD_PAD
    return pl.pallas_call(
        functools.partial(_flash_kernel, tq=tq), grid=grid,
        in_specs=[pl.BlockSpec((tq, w), lambda b, h, i: (b * nqb + i, h)),
                  pl.BlockSpec((seq, w), lambda b, h, i: (b, h)),
                  pl.BlockSpec((1, seq // tq, FLASH_HPB * V_DIM, tq), lambda b, h, i: (b, 0, h, 0))],
        out_specs=pl.BlockSpec((tq, FLASH_HPB * V_DIM), lambda b, h, i: (b * nqb + i, h)),
        out_shape=jax.ShapeDtypeStruct((batch * seq, MLA_HEADS * V_DIM), F32),
        scratch_shapes=[pltpu.VMEM((FLASH_HPB, tq, tq), F32), pltpu.VMEM((FLASH_HPB, tq, tq), F32),
                        pltpu.VMEM((FLASH_HPB, 1, tq), F32), pltpu.VMEM((FLASH_HPB, V_DIM + FLASH_LROWS, tq), F32)],
        compiler_params=_cparams("parallel", "parallel", "arbitrary"), name="mla_flash")(q, k, vt)


PAGES_PER_STEP = 32
PAGE_GROUP = 4
PAGE_STREAMS = 4


def _paged_kernel(pt_ref, qa_ref, qr_ref, latn_ref, krn_ref, lat_hbm, kr_hbm, o_ref,
                  lat_buf, kr_buf, sem, m_s, l_s, acc_s, *, n_new, g, group, streams):
    seq, step = pl.program_id(0), pl.program_id(1)
    nsteps = pl.num_programs(1)
    lin = seq * nsteps + step
    slot = lin % 2

    def page_copy(page, sl, j):
        return (pltpu.make_async_copy(lat_hbm.at[page], lat_buf.at[sl, j], sem.at[sl, 0]),
                pltpu.make_async_copy(kr_hbm.at[page], kr_buf.at[sl, j], sem.at[sl, 1]))

    def start_step(sq, st, sl):
        for j in range(g):
            for c in page_copy(pt_ref[sq, st * g + j], sl, j):
                c.start(priority=j % 2)

    @pl.when(lin == 0)
    def _():
        start_step(0, 0, 0)

    @pl.when(lin + 1 < pl.num_programs(0) * nsteps)
    def _():
        wrap = step + 1 == nsteps
        start_step(jnp.where(wrap, seq + 1, seq), jnp.where(wrap, 0, step + 1), 1 - slot)

    for j in range(g):
        for c in page_copy(0, slot, j):
            c.wait()

    lat_refs = [lat_buf.at[slot, j] for j in range(g)]
    kr_refs = [kr_buf.at[slot, j] for j in range(g)]

    @pl.when(step == 0)
    def _():
        m_s[...] = jnp.full(m_s.shape, NEG, F32)
        l_s[...] = jnp.zeros(l_s.shape, F32)
        acc_s[...] = jnp.zeros(acc_s.shape, F32)

    qa = qa_ref[0]
    qr = qr_ref[0]

    def update(carry, s, vb):
        m, l, acc = carry
        m_new = jnp.maximum(m, jnp.max(s, axis=-1, keepdims=True))
        alpha = jnp.exp(m - m_new)
        p = jnp.exp(s - m_new)
        l = alpha * l + jnp.sum(p, axis=-1, keepdims=True)
        acc = alpha * acc + _dot(p.astype(BF16), vb)
        return m_new, l, acc

    def scores(grp):
        pages = range(grp * group, (grp + 1) * group)
        latb = jnp.concatenate([lat_refs[j][...].astype(BF16) for j in pages], axis=0)
        krt = jnp.concatenate([kr_refs[j][...].astype(BF16) for j in pages], axis=1)
        return _dot_nt(qa, latb) + _dot(qr, krt), latb

    carries = [(m_s[t], l_s[t], acc_s[t]) for t in range(streams)]
    ngrp = g // group
    ahead = streams
    pend = [scores(i) for i in range(min(ahead, ngrp))]
    for grp in range(ngrp):
        if grp + ahead < ngrp:
            pend.append(scores(grp + ahead))
        t = grp % streams
        carries[t] = update(carries[t], *pend[grp])
        pend[grp] = None
    for t in range(streams):
        m_s[t], l_s[t], acc_s[t] = carries[t]

    @pl.when(step == pl.num_programs(1) - 1)
    def _():
        latn = latn_ref[0].astype(BF16)
        krn = krn_ref[0].astype(BF16)
        s = _dot_nt(qa, latn) + _dot_nt(qr, krn)
        qt = lax.broadcasted_iota(jnp.int32, s.shape, 0) % n_new
        kt = lax.broadcasted_iota(jnp.int32, s.shape, 1) - SLO
        s = jnp.where((kt >= 0) & (kt <= qt), s, NEG)
        m, l, acc = update((m_s[0], l_s[0], acc_s[0]), s, latn)
        for t in range(1, streams):
            mt = m_s[t]
            m_new = jnp.maximum(m, mt)
            a, bt = jnp.exp(m - m_new), jnp.exp(mt - m_new)
            l = a * l + bt * l_s[t]
            acc = a * acc + bt * acc_s[t]
            m = m_new
        o_ref[0] = acc / l


def _paged(page_table, qa, qr, lat_new, kr_new, cache_lat, cache_kr, *, n_new):
    b, npages = page_table.shape
    g = max(d for d in range(1, PAGES_PER_STEP + 1) if npages % d == 0)
    group = max(d for d in range(1, PAGE_GROUP + 1) if g % d == 0)
    streams = min(PAGE_STREAMS, g // group)
    rows = qa.shape[1]
    seq3 = lambda i, s, pt: (i, 0, 0)
    grid_spec = pltpu.PrefetchScalarGridSpec(
        num_scalar_prefetch=1, grid=(b, npages // g),
        in_specs=[pl.BlockSpec((1, rows, KV_LORA), seq3), pl.BlockSpec((1, rows, QK_ROPE), seq3),
                  pl.BlockSpec((1, SROWS, KV_LORA), seq3), pl.BlockSpec((1, SROWS, QK_ROPE), seq3),
                  pl.BlockSpec(memory_space=pl.ANY), pl.BlockSpec(memory_space=pl.ANY)],
        out_specs=pl.BlockSpec((1, rows, KV_LORA), seq3),
        scratch_shapes=[pltpu.VMEM((2, g, PAGE_SIZE, KV_LORA), F32), pltpu.VMEM((2, g, QK_ROPE, PAGE_SIZE), F32),
                        pltpu.SemaphoreType.DMA((2, 2)),
                        pltpu.VMEM((streams, rows, 1), F32), pltpu.VMEM((streams, rows, 1), F32),
                        pltpu.VMEM((streams, rows, KV_LORA), F32)])
    return pl.pallas_call(
        functools.partial(_paged_kernel, n_new=n_new, g=g, group=group, streams=streams), grid_spec=grid_spec,
        out_shape=jax.ShapeDtypeStruct((b, rows, KV_LORA), F32),
        compiler_params=_cparams("arbitrary", "arbitrary"), name="mla_paged")(
            page_table, qa, qr, lat_new, kr_new, cache_lat, cache_kr)


def _bmm_kernel(a_ref, w_ref, o_ref):
    o_ref[0] = _dot(a_ref[...].astype(BF16), w_ref[0]).astype(o_ref.dtype)


def _bmm_cols(a, w, out_dtype):
    n = a.shape[0]
    hh, k, m = w.shape
    return pl.pallas_call(
        _bmm_kernel, grid=(hh,),
        in_specs=[pl.BlockSpec((n, k), lambda h: (0, h)), pl.BlockSpec((1, k, m), lambda h: (h, 0, 0))],
        out_specs=pl.BlockSpec((1, n, m), lambda h: (h, 0, 0)),
        out_shape=jax.ShapeDtypeStruct((hh, n, m), out_dtype), compiler_params=_cparams("parallel"),
        name="bmm_cols")(a, w)


def _bmm_kernel3(a_ref, w_ref, o_ref):
    o_ref[0] = _dot(a_ref[0].astype(BF16), w_ref[0]).astype(o_ref.dtype)


def _bmm(a, w, out_dtype):
    hh, n, k = a.shape
    m = w.shape[2]
    return pl.pallas_call(
        _bmm_kernel3, grid=(hh,),
        in_specs=[pl.BlockSpec((1, n, k), lambda h: (h, 0, 0)), pl.BlockSpec((1, k, m), lambda h: (h, 0, 0))],
        out_specs=pl.BlockSpec((1, n, m), lambda h: (h, 0, 0)),
        out_shape=jax.ShapeDtypeStruct((hh, n, m), out_dtype), compiler_params=_cparams("parallel"),
        name="bmm")(a, w)


PAIRS = RW_HEADS // 2
C_R, C_K, C_V, C_WA, C_G = 0, RW_WIDTH, 2 * RW_WIDTH, 3 * RW_WIDTH, 3 * RW_WIDTH + DECAY_LORA + A_LORA


def _seg_sum(x):
    lane = lax.broadcasted_iota(jnp.int32, (x.shape[0], LANE), 1)
    low = lane < RW_HEAD_DIM
    outs = []
    for t in range(x.shape[1] // LANE):
        xt = x[:, t * LANE:(t + 1) * LANE]
        s0 = jnp.sum(jnp.where(low, xt, 0.0), axis=-1, keepdims=True)
        s1 = jnp.sum(jnp.where(low, 0.0, xt), axis=-1, keepdims=True)
        outs.append(jnp.where(low, s0, s1))
    return outs[0] if len(outs) == 1 else jnp.concatenate(outs, axis=-1)


def _rwkv_prologue(p_ref, s0_ref, so_ref, vec_refs, scratch, *, tb, row_lo, row_hi):
    mu_ref, w0_ref, wa2_ref, a0_ref, g2_ref, kk_ref, ka_ref, rk_ref = vec_refs
    carry_s, r_s, k_s, v_s, kk_s, b_s, ld_s = scratch
    i = pl.program_id(1)

    @pl.when(i == 0)
    def _():
        carry_s[...] = jnp.zeros(carry_s.shape, F32)
        so_ref[0] = s0_ref[0]

    p = p_ref[0]
    rows1 = lax.broadcasted_iota(jnp.int32, (tb, 1), 0)
    prev = jnp.where(rows1 == 0, carry_s[SUBLANE - 1:SUBLANE, :], pltpu.roll(p, 1, axis=0))
    carry_s[...] = p[tb - SUBLANE:tb, :]
    s = p + (prev - p) * mu_ref[...]
    r = s[:, C_R:C_K]
    k = s[:, C_K:C_V]
    v = s[:, C_V:C_WA]
    wa = s[:, C_WA:C_G]
    lane = lax.broadcasted_iota(jnp.int32, wa.shape, 1)
    z = jnp.where(lane < DECAY_LORA, jnp.tanh(wa), wa).astype(BF16)
    lin = _dot(z, wa2_ref[...])
    xw = -(w0_ref[...] + lin[:, :RW_WIDTH])
    w = -(jnp.maximum(xw, 0.0) + jnp.log(1.0 + jnp.exp(-jnp.abs(xw)))) - 0.5
    logd = -jnp.exp(w)
    a = jax.nn.sigmoid(a0_ref[...] + lin[:, RW_WIDTH:])
    gate = _dot(jax.nn.sigmoid(s[:, C_G:RW_PROJ]).astype(BF16), g2_ref[...])
    kk = k * kk_ref[...]
    kk = kk / jnp.maximum(jnp.sqrt(_seg_sum(kk * kk)), 1e-12)
    k = k * (1.0 + (a - 1.0) * ka_ref[...])
    bonus = _seg_sum(r * k * rk_ref[...]) * v
    if row_lo > 0 or row_hi < tb:
        live = (rows1 >= row_lo) & (rows1 < row_hi)
        logd = jnp.where(live, logd, 0.0)
        kk = jnp.where(live, kk, 0.0)
        k = jnp.where(live, k, 0.0)
        v = jnp.where(live, v, 0.0)
    r_s[...] = r
    k_s[...] = k
    v_s[...] = v
    kk_s[...] = kk
    b_s[...] = kk * a
    ld_s[...] = logd
    return gate, bonus


def _rwkv_epilogue(out, gate, bonus, lg_ref, lb_ref, o_ref):
    mean = _seg_sum(out) * (1.0 / RW_HEAD_DIM)
    cen = out - mean
    var = _seg_sum(cen * cen) * (1.0 / RW_HEAD_DIM)
    y = cen * lax.rsqrt(var + LNX_EPS) * lg_ref[...] + lb_ref[...]
    o_ref[0] = ((y + bonus) * gate).astype(o_ref.dtype)


def _rwkv_kernel(p_ref, s0_ref, mu_ref, w0_ref, wa2_ref, a0_ref, g2_ref, kk_ref, ka_ref, rk_ref, lg_ref, lb_ref,
                 o_ref, so_ref, carry_s, r_s, k_s, v_s, kk_s, b_s, ld_s, out_s, *, tb, ch, row_lo, row_hi):
    gate, bonus = _rwkv_prologue(p_ref, s0_ref, so_ref, (mu_ref, w0_ref, wa2_ref, a0_ref, g2_ref, kk_ref, ka_ref, rk_ref),
                                 (carry_s, r_s, k_s, v_s, kk_s, b_s, ld_s), tb=tb, row_lo=row_lo, row_hi=row_hi)
    c2 = 2 * ch
    ri = lax.broadcasted_iota(jnp.int32, (c2, c2), 0)
    ci = lax.broadcasted_iota(jnp.int32, (c2, c2), 1)
    strict = (ci % ch) < (ri % ch)
    incl = (ci % ch) <= (ri % ch)
    eye = (ri == ci).astype(F32)
    tri = (lax.broadcasted_iota(jnp.int32, (ch, ch), 1) <= lax.broadcasted_iota(jnp.int32, (ch, ch), 0)).astype(F32)
    lane2 = lax.broadcasted_iota(jnp.int32, (c2, LANE), 1)
    row2 = lax.broadcasted_iota(jnp.int32, (c2, LANE), 0)
    own = (lane2 < RW_HEAD_DIM) == (row2 < ch)

    def stack(x):
        return jnp.where(own, jnp.concatenate([x, x], axis=0), 0.0)

    bf = lambda x: x.astype(BF16)
    pairs = [slice(pr * LANE, (pr + 1) * LANE) for pr in range(PAIRS)]

    def chunk(c, _):
        ts = pl.ds(pl.multiple_of(c * ch, ch), ch)
        css = [_doth(tri, ld_s[ts, ls]) for ls in pairs]
        lhss, amats, tails = [], [], []
        for ls, cs in zip(pairs, css):
            ld = ld_s[ts, ls]
            e_inv = jnp.exp(-cs)
            lhs = bf(jnp.concatenate([stack(kk_s[ts, ls] * jnp.exp(cs - ld)), stack(r_s[ts, ls] * jnp.exp(cs))], axis=0))
            rhs = bf(jnp.concatenate([stack(k_s[ts, ls] * e_inv), stack(b_s[ts, ls] * e_inv)], axis=0))
            lhss.append(lhs)
            amats.append(_dot_nt(lhs, rhs))
        sts = [so_ref[0, pr] for pr in range(PAIRS)]
        gmats = [_dot_nt(lhs, bf(st)) for lhs, st in zip(lhss, sts)]
        vss = [bf(stack(v_s[ts, ls])) for ls in pairs]
        akbs = [jnp.where(strict, amat[:c2, c2:], 0.0) for amat in amats]
        ykk = [_dot(bf(jnp.where(strict, amat[:c2, :c2], 0.0)), vs) for amat, vs in zip(amats, vss)]
        yrk = [_dot(bf(jnp.where(incl, amat[c2:, :c2], 0.0)), vs) for amat, vs in zip(amats, vss)]
        invs = [eye - jnp.where((ri // 2 == ci // 2), a_kb, 0.0) for a_kb in akbs]
        m = 2
        while m < ch:
            lvl = (ri // (2 * m) == ci // (2 * m)) & ((ri // m) % 2 == 1) & ((ci // m) % 2 == 0)
            tmps = [_dot(bf(inv), bf(jnp.where(lvl, a_kb, 0.0))) for inv, a_kb in zip(invs, akbs)]
            invs = [inv - _dot(bf(tmp), bf(inv)) for inv, tmp in zip(invs, tmps)]
            m *= 2
        us = [_dot(bf(inv), bf(gmat[:c2] + y)) for inv, gmat, y in zip(invs, gmats, ykk)]
        o2s = [gmat[c2:] + y - _dot(bf(jnp.where(incl, amat[c2:, c2:], 0.0)), bf(u))
               for gmat, y, amat, u in zip(gmats, yrk, amats, us)]
        for pr, (ls, cs, st, vs, u, o2) in enumerate(zip(pairs, css, sts, vss, us, o2s)):
            tot = cs[ch - 1:ch, :]
            e_tail = jnp.exp(tot - cs)
            out_s[ts, ls] = o2[:ch] + o2[ch:]
            so_ref[0, pr] = (st * jnp.exp(tot) + _dot_tn(vs, bf(stack(k_s[ts, ls] * e_tail)))
                             - _dot_tn(bf(u), bf(stack(b_s[ts, ls] * e_tail))))
        return 0

    lax.fori_loop(0, tb // ch, chunk, 0)
    _rwkv_epilogue(out_s[...], gate, bonus, lg_ref, lb_ref, o_ref)


def _rwkv(p, s0, mu, w0, wa2, a0, g2, k_k, k_a, r_k, lnx_g, lnx_b, *, tb, ch, row_lo, row_hi):
    b, t, _ = p.shape
    const = lambda bi, i: (0, 0)
    vec = lambda a: pl.BlockSpec(a.shape, const)
    scr = lambda: pltpu.VMEM((tb, RW_WIDTH), F32)
    return pl.pallas_call(
        functools.partial(_rwkv_kernel, tb=tb, ch=ch, row_lo=row_lo, row_hi=row_hi), grid=(b, t // tb),
        in_specs=[pl.BlockSpec((1, tb, RW_PROJ), lambda bi, i: (bi, i, 0)),
                  pl.BlockSpec((1, PAIRS, LANE, LANE), lambda bi, i: (bi, 0, 0, 0)),
                  vec(mu), vec(w0), vec(wa2), vec(a0), vec(g2), vec(k_k), vec(k_a), vec(r_k), vec(lnx_g), vec(lnx_b)],
        out_specs=[pl.BlockSpec((1, tb, RW_WIDTH), lambda bi, i: (bi, i, 0)),
                   pl.BlockSpec((1, PAIRS, LANE, LANE), lambda bi, i: (bi, 0, 0, 0))],
        out_shape=[jax.ShapeDtypeStruct((b, t, RW_WIDTH), BF16), jax.ShapeDtypeStruct((b, PAIRS, LANE, LANE), F32)],
        scratch_shapes=[pltpu.VMEM((SUBLANE, RW_PROJ), F32)] + [scr() for _ in range(7)],
        compiler_params=_cparams("parallel", "arbitrary"), name="rwkv")(
            p, s0, mu, w0, wa2, a0, g2, k_k, k_a, r_k, lnx_g, lnx_b)


QUAD = 4
QCH = RW_HEAD_DIM
QW = QUAD * RW_HEAD_DIM
QLEVELS = (2, 4, 8, 16, 32)
PREP_CHUNKS = 4


def _tile4(x):
    return jnp.concatenate([x] * QUAD, axis=0)


def _rwkv_quad_kernel(p_ref, s0_ref, mu_ref, w0_ref, wa2_ref, a0_ref, g2_ref, kk_ref, ka_ref, rk_ref, lg_ref, lb_ref,
                      tri_ref, bdm_ref, own_ref, sbm_ref,
                      o_ref, so_ref, carry_s, r_s, k_s, v_s, kk_s, b_s, ld_s, out_s,
                      lhs_c, arb_c, y_c, kt_c, dec_c, *, tb):
    gate, bonus = _rwkv_prologue(p_ref, s0_ref, so_ref, (mu_ref, w0_ref, wa2_ref, a0_ref, g2_ref, kk_ref, ka_ref, rk_ref),
                                 (carry_s, r_s, k_s, v_s, kk_s, b_s, ld_s), tb=tb, row_lo=0, row_hi=tb)
    ch = QCH
    nquad = RW_HEADS // QUAD
    bf = lambda x: x.astype(BF16)

    def bd(x):
        return _tile4(bf(x)) * bdm_ref[0]

    quads = [slice(q * QW, (q + 1) * QW) for q in range(nquad)]

    def prepare(chunks):
        inst = [(c, q, pl.ds(pl.multiple_of(c * ch, ch), ch), quads[q]) for c in chunks for q in range(nquad)]
        strict, incl, eye, lvl1 = sbm_ref[0] > 0, sbm_ref[1] > 0, sbm_ref[2], sbm_ref[3]
        css = []
        for c, q, ts, ls in inst:
            ld = ld_s[ts, ls]
            l1 = bf(ld)
            r1 = ld - l1.astype(F32)
            l2 = bf(r1)
            l3 = bf(r1 - l2.astype(F32))
            css.append(_dot(tri_ref[...], jnp.concatenate([l1, l2, l3], axis=0)))
        a1s, a2s = [], []
        for (c, q, ts, ls), cs in zip(inst, css):
            ld = ld_s[ts, ls]
            tot = cs[ch - 1:ch, :]
            e_inv = jnp.exp(-cs)
            kc, bc = k_s[ts, ls], b_s[ts, ls]
            lhs = bf(jnp.concatenate([kk_s[ts, ls] * jnp.exp(cs - ld), r_s[ts, ls] * jnp.exp(cs)], axis=0))
            lhs_c[c, q] = lhs
            e_tail = jnp.exp(tot - cs)
            kt_c[c, q] = bf(jnp.concatenate([kc * e_tail, bc * e_tail], axis=0))
            dec_c[c, q] = jnp.broadcast_to(jnp.exp(tot), (SUBLANE, QW))
            a1s.append(_dot_nt(lhs, bd(kc * e_inv)))
            a2s.append(_dot_nt(lhs, bd(bc * e_inv)))
        ys, akb4s, invs = [], [], []
        for (c, q, ts, ls), a1, a2 in zip(inst, a1s, a2s):
            a_kk = jnp.where(strict, a1[:ch], 0.0)
            a_rk = jnp.where(incl, a1[ch:], 0.0)
            a_kb = jnp.where(strict, a2[:ch], 0.0)
            arb_c[c, q] = bf(jnp.where(incl, a2[ch:], 0.0))
            ys.append(_dot(bf(jnp.concatenate([a_kk, a_rk], axis=0)), bd(v_s[ts, ls])))
            akb4s.append(_tile4(bf(a_kb)))
            invs.append(eye - a_kb * lvl1)
        for li in range(len(QLEVELS)):
            tmps = [_dot(bf(inv), akb4 * bdm_ref[1 + li]) for inv, akb4 in zip(invs, akb4s)]
            invs = [inv - _dot(bf(tmp), bd(inv)) for inv, tmp in zip(invs, tmps)]
        ws = [_dot(bf(inv), _tile4(lhs_c[c, q][:ch]) * bdm_ref[0]) for (c, q, ts, ls), inv in zip(inst, invs)]
        yks = [_dot(bf(inv), bd(y[:ch])) for inv, y in zip(invs, ys)]
        for (c, q, ts, ls), w, yk, y in zip(inst, ws, yks, ys):
            lhs_c[c, q, 0:ch, :] = bf(w)
            y_c[c, q] = jnp.concatenate([yk, y[ch:]], axis=0)

    nc = tb // ch
    if nc % PREP_CHUNKS == 0:
        def prepare_loop(cc, _):
            prepare([PREP_CHUNKS * cc + d for d in range(PREP_CHUNKS)])
            return 0
        lax.fori_loop(0, nc // PREP_CHUNKS, prepare_loop, 0)
    else:
        lax.fori_loop(0, nc, lambda c, _: (prepare([c]), 0)[1], 0)

    def recur(c, _):
        ts = pl.ds(pl.multiple_of(c * ch, ch), ch)
        sts = [so_ref[0, q] for q in range(nquad)]
        gys = [_dot_nt(lhs_c[c, q], bf(sts[q])) + y_c[c, q] for q in range(nquad)]
        us = [gys[q][:ch] for q in range(nquad)]
        outs = [gys[q][ch:] - _dot(arb_c[c, q], bd(us[q])) for q in range(nquad)]
        upds = [_dot_tn(bf(jnp.concatenate([v_s[ts, quads[q]], -us[q]], axis=0)), kt_c[c, q]) for q in range(nquad)]
        for q in range(nquad):
            out_s[ts, quads[q]] = outs[q]
            so_ref[0, q] = sts[q] * dec_c[c, q][0:1, :] + upds[q] * own_ref[...]
        return 0

    lax.fori_loop(0, nc, recur, 0)
    _rwkv_epilogue(out_s[...], gate, bonus, lg_ref, lb_ref, o_ref)


def _quad_masks():
    ch = QCH
    r = np.arange(QUAD * ch)[:, None]
    c = np.arange(QW)[None, :]
    own = (r // ch) == (c // RW_HEAD_DIM)
    sp, s = r % ch, c % ch
    bdm = [own]
    for m in QLEVELS:
        bdm.append(own & (sp // (2 * m) == s // (2 * m)) & ((sp // m) % 2 == 1) & ((s // m) % 2 == 0))
    t = np.arange(ch)[:, None]
    sbm = [s < t, s <= t, s == t, (t // 2 == s // 2) & (t % 2 == 1) & (s % 2 == 0)]
    tri = np.tile(np.arange(ch)[None, :] <= np.arange(ch)[:, None], (1, 3))
    return (jnp.asarray(tri, BF16), jnp.asarray(np.stack(bdm), BF16), jnp.asarray(own, F32),
            jnp.asarray(np.stack([np.broadcast_to(m, (ch, QW)) for m in sbm]), F32))


def _rwkv_quad(p, s0, mu, w0, wa2, a0, g2, k_k, k_a, r_k, lnx_g, lnx_b, *, tb):
    b, t, _ = p.shape
    nq = RW_HEADS // QUAD
    nc = tb // QCH
    masks = _quad_masks()
    vec = lambda a: pl.BlockSpec(a.shape, lambda bi, i: (0,) * a.ndim)
    scr = lambda: pltpu.VMEM((tb, RW_WIDTH), F32)
    vecs = (mu, w0, wa2, a0, g2, k_k, k_a, r_k, lnx_g, lnx_b) + masks
    return pl.pallas_call(
        functools.partial(_rwkv_quad_kernel, tb=tb), grid=(b, t // tb),
        in_specs=[pl.BlockSpec((1, tb, RW_PROJ), lambda bi, i: (bi, i, 0)),
                  pl.BlockSpec((1, nq, QW, QW), lambda bi, i: (bi, 0, 0, 0))] + [vec(a) for a in vecs],
        out_specs=[pl.BlockSpec((1, tb, RW_WIDTH), lambda bi, i: (bi, i, 0)),
                   pl.BlockSpec((1, nq, QW, QW), lambda bi, i: (bi, 0, 0, 0))],
        out_shape=[jax.ShapeDtypeStruct((b, t, RW_WIDTH), BF16), jax.ShapeDtypeStruct((b, nq, QW, QW), F32)],
        scratch_shapes=[pltpu.VMEM((SUBLANE, RW_PROJ), F32)] + [scr() for _ in range(7)] + [
            pltpu.VMEM((nc, nq, 2 * QCH, QW), BF16),
            pltpu.VMEM((nc, nq, QCH, QW), BF16), pltpu.VMEM((nc, nq, 2 * QCH, QW), F32),
            pltpu.VMEM((nc, nq, 2 * QCH, QW), BF16), pltpu.VMEM((nc, nq, SUBLANE, QW), F32)],
        compiler_params=_cparams("parallel", "arbitrary"), name="rwkv_quad")(p, s0, *vecs)


def _quads_to_state(sq):
    b = sq.shape[0]
    s6 = sq.reshape(b, RW_HEADS // QUAD, QUAD, RW_HEAD_DIM, QUAD, RW_HEAD_DIM)
    return jnp.stack([s6[:, :, h, :, h, :] for h in range(QUAD)], axis=2).reshape(b, RW_HEADS, RW_HEAD_DIM, RW_HEAD_DIM)


def _mix_kernel(x_ref, orw_ref, omla_ref, g_ref, wa_ref, wb_ref, o_ref):
    om = _rms(omla_ref[...], g_ref[...]).astype(BF16)
    o_ref[...] = x_ref[...] + _dot(orw_ref[...], wa_ref[...]) + _dot(om, wb_ref[...])


def _mix(x, orw, omla, g, wa, wb, *, tm):
    n, d = x.shape
    row = lambda i: (i, 0)
    const = lambda i: (0, 0)
    return pl.pallas_call(
        _mix_kernel, grid=(n // tm,),
        in_specs=[pl.BlockSpec((tm, d), row), pl.BlockSpec((tm, orw.shape[1]), row), pl.BlockSpec((tm, omla.shape[1]), row),
                  pl.BlockSpec(g.shape, const), pl.BlockSpec(wa.shape, const), pl.BlockSpec(wb.shape, const)],
        out_specs=pl.BlockSpec((tm, d), row), out_shape=jax.ShapeDtypeStruct((n, d), F32),
        compiler_params=_cparams("parallel"), name="mix_out")(x, orw, omla, g, wa, wb)


def _norm_mm_kernel(x_ref, g_ref, w_ref, o_ref):
    o_ref[...] = _dot(_rms(x_ref[...], g_ref[...]).astype(BF16), w_ref[...]).astype(o_ref.dtype)


def _norm_mm(x, g, w, out_dtype, *, tm):
    n, d = x.shape
    m = w.shape[1]
    return pl.pallas_call(
        _norm_mm_kernel, grid=(n // tm,),
        in_specs=[pl.BlockSpec((tm, d), lambda i: (i, 0)), pl.BlockSpec(g.shape, lambda i: (0, 0)),
                  pl.BlockSpec(w.shape, lambda i: (0, 0))],
        out_specs=pl.BlockSpec((tm, m), lambda i: (i, 0)), out_shape=jax.ShapeDtypeStruct((n, m), out_dtype),
        compiler_params=_cparams("parallel"), name="norm_mm")(x, g, w)


def _mm_res_kernel(a_ref, w_ref, x_ref, o_ref):
    o_ref[...] = x_ref[...] + _dot(a_ref[...], w_ref[...])


def _mm_res(a, w, x, *, tm):
    n, d = x.shape
    return pl.pallas_call(
        _mm_res_kernel, grid=(n // tm,),
        in_specs=[pl.BlockSpec((tm, a.shape[1]), lambda i: (i, 0)), pl.BlockSpec(w.shape, lambda i: (0, 0)),
                  pl.BlockSpec((tm, d), lambda i: (i, 0))],
        out_specs=pl.BlockSpec((tm, d), lambda i: (i, 0)), out_shape=jax.ShapeDtypeStruct((n, d), F32),
        compiler_params=_cparams("parallel"), name="mm_res")(a, w, x)


def _memattn_kernel(q_ref, k_ref, v_ref, o_ref):
    dh = q_ref.shape[2] // MEM_HEADS
    scale = dh ** -0.5
    for h in range(MEM_HEADS):
        cs = slice(h * dh, (h + 1) * dh)
        s = _dot_nt(q_ref[0, :, cs], k_ref[0, :, cs].astype(BF16)) * scale
        p = jnp.exp(s - jnp.max(s, axis=-1, keepdims=True))
        l = jnp.sum(p, axis=-1, keepdims=True)
        o_ref[0, :, cs] = (_dot(p.astype(BF16), v_ref[0, :, cs].astype(BF16)) / l).astype(o_ref.dtype)


def _memattn_tiled_kernel(q_ref, k_ref, v_ref, o_ref, *, nm):
    dh = q_ref.shape[2] // MEM_HEADS
    nj = dh // LANE
    scale = dh ** -0.5
    for h in range(MEM_HEADS):
        rows = [pl.ds(j * MEM_HEADS + h, nm, stride=nj * MEM_HEADS) for j in range(nj)]
        s = sum(_dot_nt(q_ref[0, :, h * dh + j * LANE:h * dh + (j + 1) * LANE], k_ref[0, rows[j], :].astype(BF16))
                for j in range(nj)) * scale
        p = jnp.exp(s - jnp.max(s, axis=-1, keepdims=True))
        l = jnp.sum(p, axis=-1, keepdims=True)
        pb = p.astype(BF16)
        for j in range(nj):
            o_ref[0, :, h * dh + j * LANE:h * dh + (j + 1) * LANE] = (
                _dot(pb, v_ref[0, rows[j], :].astype(BF16)) / l).astype(o_ref.dtype)


def _memattn_tiled(q, mk, mv, *, nm):
    b, t, d = q.shape
    mem_spec = pl.BlockSpec((1,) + mk.shape[1:], lambda bi: (bi, 0, 0))
    return pl.pallas_call(
        functools.partial(_memattn_tiled_kernel, nm=nm), grid=(b,),
        in_specs=[pl.BlockSpec((1, t, d), lambda bi: (bi, 0, 0)), mem_spec, mem_spec],
        out_specs=pl.BlockSpec((1, t, d), lambda bi: (bi, 0, 0)), out_shape=jax.ShapeDtypeStruct((b, t, d), BF16),
        compiler_params=_cparams("parallel"), name="mem_attn_tiled")(q, mk, mv)


def _memattn(q, mk, mv, *, tm):
    b, t, d = q.shape
    mem_spec = pl.BlockSpec((1,) + mk.shape[1:], lambda bi, i: (bi,) + (0,) * (mk.ndim - 1))
    return pl.pallas_call(
        _memattn_kernel, grid=(b, t // tm),
        in_specs=[pl.BlockSpec((1, tm, d), lambda bi, i: (bi, i, 0)), mem_spec, mem_spec],
        out_specs=pl.BlockSpec((1, tm, d), lambda bi, i: (bi, i, 0)), out_shape=jax.ShapeDtypeStruct((b, t, d), BF16),
        compiler_params=_cparams("parallel", "parallel"), name="mem_attn")(q, mk, mv)


def _memblock_kernel(x_ref, orw_ref, omla_ref, gmo_ref, wa_ref, wb_ref, gq_ref, wq_ref, k_ref, v_ref, wo_ref, o_ref):
    om = _rms(omla_ref[0], gmo_ref[...]).astype(BF16)
    x1 = x_ref[0] + _dot(orw_ref[0], wa_ref[...]) + _dot(om, wb_ref[...])
    q = _dot(_rms(x1, gq_ref[...]).astype(BF16), wq_ref[...]).astype(BF16)
    dh = q.shape[1] // MEM_HEADS
    scale = dh ** -0.5
    outs = []
    for h in range(MEM_HEADS):
        cs = slice(h * dh, (h + 1) * dh)
        s = _dot_nt(q[:, cs], k_ref[0, :, cs].astype(BF16)) * scale
        p = jnp.exp(s - jnp.max(s, axis=-1, keepdims=True))
        l = jnp.sum(p, axis=-1, keepdims=True)
        outs.append((_dot(p.astype(BF16), v_ref[0, :, cs].astype(BF16)) / l).astype(BF16))
    o_ref[0] = x1 + _dot(jnp.concatenate(outs, axis=1), wo_ref[...])


def _memblock(x, orw, omla, gmo, wa, wb, gq, wq, mk, mv, wo, *, tm):
    b, t, d = x.shape
    row = lambda bi, i: (bi, i, 0)
    seq = lambda bi, i: (bi, 0, 0)
    const = lambda bi, i: (0, 0)
    cs = lambda a: pl.BlockSpec(a.shape, const)
    return pl.pallas_call(
        _memblock_kernel, grid=(b, t // tm),
        in_specs=[pl.BlockSpec((1, tm, d), row), pl.BlockSpec((1, tm, orw.shape[2]), row), pl.BlockSpec((1, tm, omla.shape[2]), row),
                  cs(gmo), cs(wa), cs(wb), cs(gq), cs(wq), pl.BlockSpec((1,) + mk.shape[1:], seq),
                  pl.BlockSpec((1,) + mv.shape[1:], seq), cs(wo)],
        out_specs=pl.BlockSpec((1, tm, d), row), out_shape=jax.ShapeDtypeStruct((b, t, d), F32),
        compiler_params=_cparams("parallel", "parallel"), name="mem_block")(x, orw, omla, gmo, wa, wb, gq, wq, mk, mv, wo)


FFN_CHUNK = 256
FFN_GROUP = 4


def _ffn_kernel(x_ref, prev_ref, g_ref, wup_ref, cw_ref, cb_ref, wdn_ref, gf_ref, y_ref, u_ref, act_s, *, tm, nch, prev_rows):
    i = pl.program_id(1)
    x = x_ref[0]
    h = _rms(x, g_ref[...]).astype(BF16)
    rows = lax.broadcasted_iota(jnp.int32, (tm, 1), 0)
    if prev_rows is None:
        @pl.when(i == 0)
        def _():
            u_ref[0] = prev_ref[0]
    else:
        hist = (rows % SROWS) < prev_rows

    def up(c):
        return _dot(h, wup_ref[c]), _dot(h, wup_ref[c + nch])

    def conv(c, u):
        if prev_rows is None:
            tail = u_ref[0, c]
            u_ref[0, c] = u[tm - SUBLANE:tm, :]
            p1, p2 = tail[SUBLANE - 1:SUBLANE, :], tail[SUBLANE - 2:SUBLANE - 1, :]
            u1, u2 = pltpu.roll(u, 1, axis=0), pltpu.roll(u, 2, axis=0)
            r8 = rows[:SUBLANE]
            u1 = jnp.concatenate([jnp.where(r8 == 0, p1, u1[:SUBLANE]), u1[SUBLANE:]], axis=0)
            u2 = jnp.concatenate([jnp.where(r8 == 0, p2, jnp.where(r8 == 1, p1, u2[:SUBLANE])), u2[SUBLANE:]], axis=0)
        else:
            u = jnp.where(hist, prev_ref[c], u)
            u_ref[c] = u
            u1 = pltpu.roll(u, 1, axis=0)
            u2 = pltpu.roll(u, 2, axis=0)
        cw = cw_ref[c]
        return cb_ref[c] + cw[0:1, :] * u2 + cw[1:2, :] * u1 + cw[2:3, :] * u

    def gated(c, ug, uv):
        gate = conv(c, ug)
        val = conv(c + nch, uv)
        act_s[c] = (gate * jax.nn.sigmoid(gate) * val).astype(BF16)

    def group(cs):
        ups = [up(c) for c in cs]
        for c, (ug, uv) in zip(cs, ups):
            gated(c, ug, uv)

    ngrp = nch // FFN_GROUP

    def body(gi, _):
        group([gi * FFN_GROUP + d for d in range(FFN_GROUP)])
        return 0

    lax.fori_loop(0, ngrp, body, 0)
    if nch % FFN_GROUP:
        group(list(range(ngrp * FFN_GROUP, nch)))
    f = _dot(jnp.concatenate([act_s[c] for c in range(nch)], axis=1), wdn_ref[...])
    y_ref[0] = _rms(x + f, gf_ref[...])


def _ffn(x, prev, g, wup, cw, cb, wdn, gf, *, tm, prev_rows):
    b, t, d = x.shape
    nch = wup.shape[0] // 2
    c3 = lambda bi, i: (0, 0, 0)
    if prev_rows is None:
        prev_spec = pl.BlockSpec((1, 2 * nch, SUBLANE, FFN_CHUNK), lambda bi, i: (bi, 0, 0, 0))
        u_spec = pl.BlockSpec((1, 2 * nch, SUBLANE, FFN_CHUNK), lambda bi, i: (bi, 0, 0, 0))
        u_shape = jax.ShapeDtypeStruct((b, 2 * nch, SUBLANE, FFN_CHUNK), F32)
    else:
        prev_spec = pl.BlockSpec((2 * nch, tm, FFN_CHUNK), lambda bi, i: (0, bi * (t // tm) + i, 0))
        u_spec = pl.BlockSpec((2 * nch, tm, FFN_CHUNK), lambda bi, i: (0, bi * (t // tm) + i, 0))
        u_shape = jax.ShapeDtypeStruct((2 * nch, b * t, FFN_CHUNK), F32)
    return pl.pallas_call(
        functools.partial(_ffn_kernel, tm=tm, nch=nch, prev_rows=prev_rows), grid=(b, t // tm),
        in_specs=[pl.BlockSpec((1, tm, d), lambda bi, i: (bi, i, 0)), prev_spec, pl.BlockSpec(g.shape, lambda bi, i: (0, 0)),
                  pl.BlockSpec(wup.shape, c3), pl.BlockSpec(cw.shape, c3), pl.BlockSpec(cb.shape, c3),
                  pl.BlockSpec(wdn.shape, lambda bi, i: (0, 0)), pl.BlockSpec(gf.shape, lambda bi, i: (0, 0))],
        out_specs=[pl.BlockSpec((1, tm, d), lambda bi, i: (bi, i, 0)), u_spec],
        out_shape=[jax.ShapeDtypeStruct((b, t, d), F32), u_shape],
        scratch_shapes=[pltpu.VMEM((nch, tm, FFN_CHUNK), BF16)],
        compiler_params=_cparams("parallel", "arbitrary"), name="conv_ffn")(x, prev, g, wup, cw, cb, wdn, gf)


def _rope_tables(pos, scale):
    half = QK_ROPE // 2
    inv = 1.0 / (ROPE_THETA ** (np.arange(half, dtype=np.float64) / half))
    ang = np.asarray(pos, np.float64)[:, None] * inv[None, :]
    cos, sin = np.cos(ang), np.sin(ang)
    n = len(pos)
    c = np.zeros((n, HEAD_PAD))
    s = np.zeros((n, HEAD_PAD))
    c[:, :QK_NOPE] = 1.0
    c[:, ROPE_LO:ROPE_LO + half] = cos
    c[:, ROPE_LO + half:ROPE_LO + QK_ROPE] = cos
    s[:, ROPE_LO:ROPE_LO + half] = -sin
    s[:, ROPE_LO + half:ROPE_LO + QK_ROPE] = sin
    return (jnp.asarray(c * scale, F32), jnp.asarray(s * scale, F32), jnp.asarray(c, F32), jnp.asarray(s, F32))


def _swap_halves(w):
    half = w.shape[-1] // 2
    return jnp.concatenate([w[..., half:], w[..., :half]], axis=-1)


def _prep_weights(w_in, w_uq, w_ukv, rw_w2, rw_a2, w_up, conv_w, conv_b, w_down):
    d = w_in.shape[0]
    z = lambda *s: jnp.zeros(s, F32)
    w_kr = w_in[:, Q_LORA + KV_LORA:MLA_PROJ]
    pad_head = lambda w: jnp.concatenate([z(d, ROPE_LO), w, z(d, HEAD_PAD - ROPE_LO - QK_ROPE)], axis=1)
    w1 = jnp.concatenate([w_in[:, :Q_LORA + KV_LORA], pad_head(w_kr), pad_head(_swap_halves(w_kr)), w_in[:, MLA_PROJ:]],
                         axis=1).astype(BF16)
    wq3 = w_uq.reshape(Q_LORA, MLA_HEADS, QK_NOPE + QK_ROPE)
    zq = lambda n: z(Q_LORA, MLA_HEADS, n)
    q_plain = jnp.concatenate([wq3, zq(HEAD_PAD - QK_NOPE - QK_ROPE)], axis=2)
    q_swap = jnp.concatenate([zq(QK_NOPE), _swap_halves(wq3[..., QK_NOPE:]), zq(HEAD_PAD - QK_NOPE - QK_ROPE)], axis=2)
    wq = jnp.concatenate([q_plain.reshape(Q_LORA, -1), q_swap.reshape(Q_LORA, -1)], axis=1).astype(BF16)
    wkv3 = w_ukv.reshape(KV_LORA, MLA_HEADS, QK_NOPE + V_DIM)
    zk = z(KV_LORA, MLA_HEADS, HEAD_PAD - QK_NOPE)
    wk = jnp.concatenate([wkv3[..., :QK_NOPE], zk], axis=2).reshape(KV_LORA, -1).astype(BF16)
    wv = wkv3[..., QK_NOPE:].reshape(KV_LORA, -1).T.astype(BF16)
    w_uk = jnp.transpose(wkv3[..., :QK_NOPE], (1, 2, 0))
    w_uk = jnp.concatenate([w_uk, z(MLA_HEADS, HEAD_PAD - QK_NOPE, KV_LORA)], axis=1).astype(BF16)
    w_uv = jnp.transpose(wkv3[..., QK_NOPE:], (1, 0, 2)).astype(BF16)
    wa2 = jnp.concatenate([jnp.concatenate([rw_w2, z(A_LORA, RW_WIDTH)], axis=0),
                           jnp.concatenate([z(DECAY_LORA, RW_WIDTH), rw_a2], axis=0)], axis=1).astype(BF16)
    f2 = w_up.shape[1]
    nch2 = f2 // FFN_CHUNK
    wup = jnp.transpose(w_up.reshape(d, nch2, FFN_CHUNK), (1, 0, 2)).astype(BF16)
    cw = jnp.transpose(conv_w.reshape(CONV_W, nch2, FFN_CHUNK), (1, 0, 2))
    cw = jnp.concatenate([cw, z(nch2, SUBLANE - CONV_W, FFN_CHUNK)], axis=1)
    cb = conv_b.reshape(nch2, 1, FFN_CHUNK)
    wdn = w_down.astype(BF16)
    return w1, wq, wk, wv, w_uk, w_uv, wa2, wup, cw, cb, wdn


def _state_to_pairs(s):
    b = s.shape[0]
    s = s.reshape(b, PAIRS, 2, RW_HEAD_DIM, RW_HEAD_DIM)
    zz = jnp.zeros_like(s[:, :, 0])
    top = jnp.concatenate([s[:, :, 0], zz], axis=-1)
    bot = jnp.concatenate([zz, s[:, :, 1]], axis=-1)
    return jnp.concatenate([top, bot], axis=-2)


def _pairs_to_state(s):
    b = s.shape[0]
    h0 = s[:, :, :RW_HEAD_DIM, :RW_HEAD_DIM]
    h1 = s[:, :, RW_HEAD_DIM:, RW_HEAD_DIM:]
    return jnp.stack([h0, h1], axis=2).reshape(b, RW_HEADS, RW_HEAD_DIM, RW_HEAD_DIM)


def _pick(n, pref):
    for t in pref:
        if n % t == 0:
            return t
    return n


def kernel(x_prompt, x_sample, cache_mla_latent, cache_mla_krope, cache_mem_k, cache_mem_v, state_rwkv, state_rwkv_shift, state_ffn_conv, page_table, mem_prompt, g_mix, w_in, q_norm_g, kv_norm_g, w_uq, w_ukv, g_mla_out, rw_mu, rw_w0, rw_w2, rw_a0, rw_a2, rw_g2, rw_k_k, rw_k_a, rw_r_k, rw_lnx_g, rw_lnx_b, w_o, g_mem_q, g_mem_kv, w_mq, w_mk, w_mv, w_mo, g_ffn, w_up, conv_w, conv_b, w_down, g_final):
    depth = w_in.shape[0]
    assert depth == 1, "single-layer step"
    bp, tp, d = x_prompt.shape
    bs, ts, _ = x_sample.shape
    npages = page_table.shape[1]
    past_len = npages * PAGE_SIZE
    row2 = lambda a: a.reshape(1, -1)
    l = 0
    w1, wq, wk, wv, w_uk, w_uv, wa2, wup, cw, cb, wdn = _prep_weights(
        w_in[l], w_uq[l], w_ukv[l], rw_w2[l], rw_a2[l], w_up[l], conv_w[l], conv_b[l], w_down[l])
    wo_a, wo_b = w_o[l, :RW_WIDTH].astype(BF16), w_o[l, RW_WIDTH:].astype(BF16)
    wmq, wmo = w_mq[l].astype(BF16), w_mo[l].astype(BF16)
    wmkv = jnp.concatenate([w_mk[l], w_mv[l]], axis=1).astype(BF16)
    g2 = rw_g2[l].astype(BF16)
    nch2 = wup.shape[0]
    n_mem_s = cache_mem_k.shape[2]
    rw_vecs = (row2(rw_mu[l]), row2(rw_w0[l]), wa2, row2(rw_a0[l]), g2, row2(rw_k_k[l]), row2(rw_k_a[l]),
               row2(rw_r_k[l]), row2(rw_lnx_g[l]), row2(rw_lnx_b[l]))

    def after_attention(x2, orw, omla, mem_k, mem_v, conv_in, b, t, tm, tm_mem, tm_ffn, prev_rows):
        if prev_rows is None:
            xm = _memblock(x2.reshape(b, t, d), orw.reshape(b, t, -1), omla.reshape(b, t, -1), row2(g_mla_out[l]),
                           wo_a, wo_b, row2(g_mem_q[l]), wmq, mem_k, mem_v, wmo, tm=tm)
        else:
            x1 = _mix(x2, orw, omla, row2(g_mla_out[l]), wo_a, wo_b, tm=tm)
            qm = _norm_mm(x1, row2(g_mem_q[l]), wmq, BF16, tm=tm)
            om = _memattn_tiled(qm.reshape(b, t, d), mem_k, mem_v, nm=n_mem_s)
            xm = _mm_res(om.reshape(b * t, d), wmo, x1, tm=tm)
        fb, ft = (b, t) if prev_rows is None else (1, b * t)
        return _ffn(xm.reshape(fb, ft, d), conv_in, row2(g_ffn[l]), wup, cw, cb, wdn, row2(g_final), tm=tm_ffn,
                    prev_rows=prev_rows)

    n_p = bp * tp
    tm_p = _pick(tp, (512, 256, 128, 64, 32, 16, 8))
    tabs_p = _rope_tables(np.arange(tp), MLA_SCALE * LOG2E)
    xp2 = x_prompt.reshape(n_p, d)
    tq = _pick(tm_p, (FLASH_T, 256, 128))
    q_p, lat_p, kr_p, prw_p, k_p, v_p = _inproj(
        xp2, row2(g_mix[l]), w1, row2(q_norm_g[l]), row2(kv_norm_g[l]), wq, tabs_p, wk, wv, tm=tm_p, with_kv=True, seq=tp, vblock=tq)
    omla_p = _flash(q_p, k_p, v_p, batch=bp, seq=tp, tq=tq)
    tb_p = _pick(tp, (512, 256, 128, 64, 32, 16, 8))
    if tb_p % QCH == 0:
        zeros_state = jnp.zeros((bp, RW_HEADS // QUAD, QW, QW), F32)
        orw_p, st_p = _rwkv_quad(prw_p.reshape(bp, tp, RW_PROJ), zeros_state, *rw_vecs, tb=tb_p)
        st_p = _quads_to_state(st_p)
    else:
        zeros_state = jnp.zeros((bp, PAIRS, LANE, LANE), F32)
        orw_p, st_p = _rwkv(prw_p.reshape(bp, tp, RW_PROJ), zeros_state, *rw_vecs, tb=tb_p,
                            ch=_pick(tb_p, (32, 16, 8)), row_lo=0, row_hi=tb_p)
        st_p = _pairs_to_state(st_p)
    mkv = _norm_mm(mem_prompt.reshape(-1, d), row2(g_mem_kv[l]), wmkv, F32, tm=_pick(mem_prompt.shape[0] * mem_prompt.shape[1], (512, 256, 128, 8)))
    n_mem = mem_prompt.shape[1]
    mk_p = mkv[:, :d].reshape(bp, n_mem, d)
    mv_p = mkv[:, d:].reshape(bp, n_mem, d)
    conv0_p = jnp.zeros((bp, nch2, SUBLANE, FFN_CHUNK), F32)
    y_p, u_p = after_attention(xp2, orw_p.reshape(n_p, RW_WIDTH), omla_p, mk_p, mv_p, conv0_p, bp, tp, tm_p, tm_p, tm_p, None)
    conv_p = jnp.transpose(u_p[:, :, SUBLANE - (CONV_W - 1):, :], (0, 2, 1, 3)).reshape(bp, CONV_W - 1, nch2 * FFN_CHUNK)

    n_s = bs * SROWS
    xs3 = jnp.pad(x_sample, ((0, 0), (SLO, SROWS - SLO - ts), (0, 0)))
    pos_s = np.tile(np.concatenate([np.zeros(SLO), past_len + np.arange(ts), np.zeros(SROWS - SLO - ts)]), bs)
    tm_s = _pick(n_s, (1024, 512, 256, 128, 64, 32, 16, 8))
    tabs_s = _rope_tables(pos_s[:tm_s], MLA_SCALE)
    q_s, lat_s, kr_s, prw_s = _inproj(
        xs3.reshape(n_s, d), row2(g_mix[l]), w1, row2(q_norm_g[l]), row2(kv_norm_g[l]), wq, tabs_s, None, None,
        tm=tm_s, with_kv=False)
    qabs = _bmm_cols(q_s, w_uk, BF16)
    qabs = qabs.reshape(MLA_HEADS, bs, SROWS, KV_LORA)[:, :, SLO:SLO + ts]
    qabs = jnp.transpose(qabs, (1, 0, 2, 3)).reshape(bs, MLA_HEADS * ts, KV_LORA)
    qrope = q_s.reshape(bs, SROWS, MLA_HEADS, HEAD_PAD)[:, SLO:SLO + ts, :, ROPE_LO:ROPE_LO + QK_ROPE]
    qrope = jnp.transpose(qrope, (0, 2, 1, 3)).reshape(bs, MLA_HEADS * ts, QK_ROPE)
    olat = _paged(page_table, qabs, qrope, lat_s.reshape(bs, SROWS, KV_LORA), kr_s.reshape(bs, SROWS, QK_ROPE),
                  cache_mla_latent[l], jnp.swapaxes(cache_mla_krope[l], 1, 2), n_new=ts)
    olat = jnp.transpose(olat.reshape(bs, MLA_HEADS, ts, KV_LORA), (1, 0, 2, 3))
    olat = jnp.pad(olat, ((0, 0), (0, 0), (SLO, SROWS - SLO - ts), (0, 0))).reshape(MLA_HEADS, n_s, KV_LORA)
    omla_s = _bmm(olat, w_uv, F32)
    omla_s = jnp.transpose(omla_s, (1, 0, 2)).reshape(n_s, MLA_HEADS * V_DIM)
    prw_s3 = prw_s.reshape(bs, SROWS, RW_PROJ).at[:, SLO - 1, :].set(state_rwkv_shift[l])
    orw_s, st_s = _rwkv(prw_s3, _state_to_pairs(state_rwkv[l]), *rw_vecs, tb=SROWS, ch=SROWS, row_lo=SLO, row_hi=SLO + ts)
    hist = jnp.transpose(state_ffn_conv[l].reshape(bs, CONV_W - 1, nch2, FFN_CHUNK), (2, 0, 1, 3))
    hist = jnp.pad(hist, ((0, 0), (0, 0), (SLO - (CONV_W - 1), SROWS - SLO), (0, 0))).reshape(nch2, n_s, FFN_CHUNK)
    def mem_rows(c):
        _, nm, hh, dh = c.shape
        return jnp.transpose(c.reshape(bs, nm, hh, dh // LANE, LANE), (0, 1, 3, 2, 4)).reshape(bs, nm * hh * (dh // LANE), LANE)
    mk_s, mv_s = mem_rows(cache_mem_k[l]), mem_rows(cache_mem_v[l])
    y_s, u_s = after_attention(xs3.reshape(n_s, d), orw_s.reshape(n_s, RW_WIDTH), omla_s, mk_s, mv_s, hist,
                               bs, SROWS, tm_s, SROWS, _pick(n_s, (256, 128, 64, 32, 16, 8)), SLO)
    u_s = u_s.reshape(nch2, bs, SROWS, FFN_CHUNK)[:, :, SLO + ts - (CONV_W - 1):SLO + ts]
    conv_s = jnp.transpose(u_s, (1, 2, 0, 3)).reshape(bs, CONV_W - 1, nch2 * FFN_CHUNK)

    real = lambda a, w: a.reshape(bs, SROWS, w)[:, SLO:SLO + ts]
    mem5 = lambda a: a.reshape(1, bp, n_mem, MEM_HEADS, d // MEM_HEADS)
    return (y_p, real(y_s, d),
            lat_p.reshape(1, bp, tp, KV_LORA), kr_p.reshape(1, bp, tp, QK_ROPE), mem5(mk_p), mem5(mv_p),
            st_p[None], prw_p.reshape(bp, tp, RW_PROJ)[:, -1][None], conv_p[None],
            real(lat_s, KV_LORA)[None], real(kr_s, QK_ROPE)[None], _pairs_to_state(st_s)[None],
            prw_s.reshape(bs, SROWS, RW_PROJ)[:, SLO + ts - 1][None], conv_s[None])
```

```python
import functools

import numpy as np
import jax
import jax.numpy as jnp
from jax import lax
from jax.experimental import pallas as pl
from jax.experimental.pallas import tpu as pltpu

F32 = jnp.float32
BF16 = jnp.bfloat16

RW_HEADS = 8
RW_HEAD_DIM = 64
RW_WIDTH = 512
DECAY_LORA = 64
A_LORA = 64
GATE_LORA = 128
RW_PROJ = 3 * RW_WIDTH + DECAY_LORA + A_LORA + GATE_LORA
LNX_EPS = 64e-5
MLA_HEADS = 8
QK_NOPE = 64
QK_ROPE = 32
V_DIM = 64
Q_LORA = 384
KV_LORA = 256
MLA_PROJ = Q_LORA + KV_LORA + QK_ROPE
MLA_SCALE = (QK_NOPE + QK_ROPE) ** -0.5
ROPE_THETA = 10000.0
MEM_HEADS = 4
CONV_W = 3
NORM_EPS = 1e-6
PAGE_SIZE = 128

LANE = 128
SUBLANE = 8
VMEM_LIMIT = 56 * 1024 * 1024
HEAD_PAD = LANE
ROPE_LO = QK_NOPE
SROWS = 8
SLO = 2
NEG = -1e30
LOG2E = float(np.log2(np.e))
HI = lax.Precision.HIGHEST


def _cparams(*sem):
    return pltpu.CompilerParams(dimension_semantics=sem, vmem_limit_bytes=VMEM_LIMIT)


def _rms(x, g, eps=NORM_EPS):
    return x * lax.rsqrt(jnp.mean(x * x, axis=-1, keepdims=True) + eps) * g


def _dot(a, b):
    return jnp.dot(a, b, preferred_element_type=F32)


def _dot_nt(a, b, precision=None):
    return lax.dot_general(a, b, (((1,), (1,)), ((), ())), preferred_element_type=F32, precision=precision)


def _dot_tn(a, b, precision=None):
    return lax.dot_general(a, b, (((0,), (0,)), ((), ())), preferred_element_type=F32, precision=precision)


def _doth(a, b):
    return jnp.dot(a, b, preferred_element_type=F32, precision=HI)


C_Q0, C_KV0, C_KR0, C_KRS0, C_RW0 = 0, Q_LORA, Q_LORA + KV_LORA, Q_LORA + KV_LORA + LANE, Q_LORA + KV_LORA + 2 * LANE
W1_COLS = C_RW0 + RW_PROJ


def _inproj_kernel(x_ref, g_ref, w1_ref, qg_ref, kvg_ref, wq_ref, cq_ref, sq_ref, ck_ref, sk_ref, *rest, with_kv):
    if with_kv:
        wk_ref, wv_ref, q_out, lat_out, kr_out, prw_out, k_out, v_out = rest
    else:
        q_out, lat_out, kr_out, prw_out = rest
    h = _rms(x_ref[...], g_ref[...]).astype(BF16)
    prw_out[...] = _dot(h, w1_ref[:, C_RW0:W1_COLS])
    pm = _dot(h, w1_ref[:, 0:C_RW0])
    cqn = _rms(pm[:, C_Q0:C_KV0], qg_ref[...]).astype(BF16)
    lat = _rms(pm[:, C_KV0:C_KR0], kvg_ref[...])
    lat_out[...] = lat
    krope = pm[:, C_KR0:C_KRS0] * ck_ref[...] + pm[:, C_KRS0:C_RW0] * sk_ref[...]
    kr_out[...] = krope[:, ROPE_LO:ROPE_LO + QK_ROPE]
    q12 = _dot(cqn, wq_ref[...])
    nq = MLA_HEADS * HEAD_PAD
    cq, sq = cq_ref[...], sq_ref[...]
    for hd in range(MLA_HEADS):
        a, b = hd * HEAD_PAD, (hd + 1) * HEAD_PAD
        q_out[:, a:b] = (q12[:, a:b] * cq + q12[:, nq + a:nq + b] * sq).astype(BF16)
    if with_kv:
        latb = lat.astype(BF16)
        kn = _dot(latb, wk_ref[...])
        for hd in range(MLA_HEADS):
            a, b = hd * HEAD_PAD, (hd + 1) * HEAD_PAD
            k_out[:, a:b] = (kn[:, a:b] + krope).astype(BF16)
        vt = _dot_nt(wv_ref[...], latb).astype(BF16)
        for jb in range(v_out.shape[1]):
            v_out[0, jb] = vt[:, jb * v_out.shape[3]:(jb + 1) * v_out.shape[3]]


def _inproj(x, g, w1, qg, kvg, wq, tabs, wk, wv, *, tm, with_kv, seq=None, vblock=None):
    n, d = x.shape
    ttab = tabs[0].shape[0]
    nt = ttab // tm
    row = lambda i: (i, 0)
    const = lambda i: (0, 0)
    tab = lambda i: (i % nt, 0)
    in_specs = [pl.BlockSpec((tm, d), row), pl.BlockSpec(g.shape, const), pl.BlockSpec(w1.shape, const),
                pl.BlockSpec(qg.shape, const), pl.BlockSpec(kvg.shape, const), pl.BlockSpec(wq.shape, const)]
    in_specs += [pl.BlockSpec((tm, LANE), tab)] * 4
    args = [x, g, w1, qg, kvg, wq, *tabs]
    nq = MLA_HEADS * HEAD_PAD
    out_shape = [jax.ShapeDtypeStruct((n, nq), BF16), jax.ShapeDtypeStruct((n, KV_LORA), F32),
                 jax.ShapeDtypeStruct((n, QK_ROPE), F32), jax.ShapeDtypeStruct((n, RW_PROJ), F32)]
    out_specs = [pl.BlockSpec((tm, nq), row), pl.BlockSpec((tm, KV_LORA), row),
                 pl.BlockSpec((tm, QK_ROPE), row), pl.BlockSpec((tm, RW_PROJ), row)]
    if with_kv:
        in_specs += [pl.BlockSpec(wk.shape, const), pl.BlockSpec(wv.shape, const)]
        args += [wk, wv]
        nv = MLA_HEADS * V_DIM
        npb = seq // tm
        out_shape += [jax.ShapeDtypeStruct((n, nq), BF16), jax.ShapeDtypeStruct((n // seq, seq // vblock, nv, vblock), BF16)]
        out_specs += [pl.BlockSpec((tm, nq), row),
                      pl.BlockSpec((1, tm // vblock, nv, vblock), lambda i: (i // npb, i % npb, 0, 0))]
    return pl.pallas_call(
        functools.partial(_inproj_kernel, with_kv=with_kv), grid=(n // tm,), in_specs=in_specs,
        out_specs=out_specs, out_shape=out_shape, compiler_params=_cparams("parallel"),
        name="inproj_kv" if with_kv else "inproj")(*args)


FLASH_HPB = 4
FLASH_T = 512
FLASH_LROWS = 16


def _flash_kernel(q_ref, k_ref, vt_ref, o_ref, st_a, st_b, m_s, acc_s, *, tq):
    tk = tq
    qi = pl.program_id(2)
    assert tq == tk
    nfull = qi
    qs = [q_ref[:, h * HEAD_PAD:(h + 1) * HEAD_PAD] for h in range(FLASH_HPB)]
    kidx = lax.broadcasted_iota(jnp.int32, (tk, tq), 0)
    qidx = lax.broadcasted_iota(jnp.int32, (tk, tq), 1)

    ones = jnp.ones((FLASH_LROWS, tk), BF16)

    def scores(slot, j):
        start = pl.multiple_of(j * tk, tk)
        for h in range(FLASH_HPB):
            slot[h] = _dot_nt(k_ref[pl.ds(start, tk), h * HEAD_PAD:(h + 1) * HEAD_PAD], qs[h])

    def step(slot, j, diagonal):
        for h in range(FLASH_HPB):
            m = m_s[h]
            st = slot[h]
            if diagonal:
                st = jnp.where(kidx <= qidx, st, NEG)
            m_new = jnp.maximum(m, jnp.max(st, axis=0, keepdims=True))
            alpha = jnp.exp2(m - m_new)
            p = jnp.exp2(st - m_new).astype(BF16)
            vt1 = jnp.concatenate([vt_ref[0, j, h * V_DIM:(h + 1) * V_DIM, :], ones], axis=0)
            acc_s[h] = alpha * acc_s[h] + _dot(vt1, p)
            m_s[h] = m_new

    m_s[...] = jnp.full(m_s.shape, NEG, F32)
    acc_s[...] = jnp.zeros(acc_s.shape, F32)

    scores(st_a, 0)

    def pair(jj, _):
        scores(st_b, 2 * jj + 1)
        step(st_a, 2 * jj, False)
        scores(st_a, 2 * jj + 2)
        step(st_b, 2 * jj + 1, False)
        return 0

    npair = nfull // 2
    lax.fori_loop(0, npair, pair, 0)

    @pl.when(nfull % 2 == 1)
    def _():
        scores(st_b, nfull)
        step(st_a, nfull - 1, False)
        step(st_b, nfull, True)

    @pl.when(nfull % 2 == 0)
    def _():
        step(st_a, nfull, True)

    ot = jnp.concatenate([acc_s[h, :V_DIM] / acc_s[h, V_DIM:V_DIM + 1] for h in range(FLASH_HPB)], axis=0)
    o_ref[...] = ot.T


def _flash(q, k, vt, *, batch, seq, tq):
    nqb = seq // tq
    grid = (batch, MLA_HEADS // FLASH_HPB, nqb)
    w = FLASH_HPB * HEAD_PAD
    return pl.pallas_call(
        functools.partial(_flash_kernel, tq=tq), grid=grid,
        in_specs=[pl.BlockSpec((tq, w), lambda b, h, i: (b * nqb + i, h)),
                  pl.BlockSpec((seq, w), lambda b, h, i: (b, h)),
                  pl.BlockSpec((1, seq // tq, FLASH_HPB * V_DIM, tq), lambda b, h, i: (b, 0, h, 0))],
        out_specs=pl.BlockSpec((tq, FLASH_HPB * V_DIM), lambda b, h, i: (b * nqb + i, h)),
        out_shape=jax.ShapeDtypeStruct((batch * seq, MLA_HEADS * V_DIM), F32),
        scratch_shapes=[pltpu.VMEM((FLASH_HPB, tq, tq), F32), pltpu.VMEM((FLASH_HPB, tq, tq), F32),
                        pltpu.VMEM((FLASH_HPB, 1, tq), F32), pltpu.VMEM((FLASH_HPB, V_DIM + FLASH_LROWS, tq), F32)],
        compiler_params=_cparams("parallel", "parallel", "arbitrary"), name="mla_flash")(q, k, vt)


PAGES_PER_STEP = 32
PAGE_GROUP = 4
PAGE_STREAMS = 4


def _paged_kernel(pt_ref, qa_ref, qr_ref, latn_ref, krn_ref, lat_hbm, kr_hbm, o_ref,
                  lat_buf, kr_buf, sem, m_s, l_s, acc_s, *, n_new, g, group, streams):
    seq, step = pl.program_id(0), pl.program_id(1)
    nsteps = pl.num_programs(1)
    lin = seq * nsteps + step
    slot = lin % 2

    def page_copy(page, sl, j):
        return (pltpu.make_async_copy(lat_hbm.at[page], lat_buf.at[sl, j], sem.at[sl, 0]),
                pltpu.make_async_copy(kr_hbm.at[page], kr_buf.at[sl, j], sem.at[sl, 1]))

    def start_step(sq, st, sl):
        for j in range(g):
            for c in page_copy(pt_ref[sq, st * g + j], sl, j):
                c.start(priority=j % 2)

    @pl.when(lin == 0)
    def _():
        start_step(0, 0, 0)

    @pl.when(lin + 1 < pl.num_programs(0) * nsteps)
    def _():
        wrap = step + 1 == nsteps
        start_step(jnp.where(wrap, seq + 1, seq), jnp.where(wrap, 0, step + 1), 1 - slot)

    for j in range(g):
        for c in page_copy(0, slot, j):
            c.wait()

    lat_refs = [lat_buf.at[slot, j] for j in range(g)]
    kr_refs = [kr_buf.at[slot, j] for j in range(g)]

    @pl.when(step == 0)
    def _():
        m_s[...] = jnp.full(m_s.shape, NEG, F32)
        l_s[...] = jnp.zeros(l_s.shape, F32)
        acc_s[...] = jnp.zeros(acc_s.shape, F32)

    qa = qa_ref[0]
    qr = qr_ref[0]

    def update(carry, s, vb):
        m, l, acc = carry
        m_new = jnp.maximum(m, jnp.max(s, axis=-1, keepdims=True))
        alpha = jnp.exp(m - m_new)
        p = jnp.exp(s - m_new)
        l = alpha * l + jnp.sum(p, axis=-1, keepdims=True)
        acc = alpha * acc + _dot(p.astype(BF16), vb)
        return m_new, l, acc

    def scores(grp):
        pages = range(grp * group, (grp + 1) * group)
        latb = jnp.concatenate([lat_refs[j][...].astype(BF16) for j in pages], axis=0)
        krt = jnp.concatenate([kr_refs[j][...].astype(BF16) for j in pages], axis=1)
        return _dot_nt(qa, latb) + _dot(qr, krt), latb

    carries = [(m_s[t], l_s[t], acc_s[t]) for t in range(streams)]
    ngrp = g // group
    ahead = streams
    pend = [scores(i) for i in range(min(ahead, ngrp))]
    for grp in range(ngrp):
        if grp + ahead < ngrp:
            pend.append(scores(grp + ahead))
        t = grp % streams
        carries[t] = update(carries[t], *pend[grp])
        pend[grp] = None
    for t in range(streams):
        m_s[t], l_s[t], acc_s[t] = carries[t]

    @pl.when(step == pl.num_programs(1) - 1)
    def _():
        latn = latn_ref[0].astype(BF16)
        krn = krn_ref[0].astype(BF16)
        s = _dot_nt(qa, latn) + _dot_nt(qr, krn)
        qt = lax.broadcasted_iota(jnp.int32, s.shape, 0) % n_new
        kt = lax.broadcasted_iota(jnp.int32, s.shape, 1) - SLO
        s = jnp.where((kt >= 0) & (kt <= qt), s, NEG)
        m, l, acc = update((m_s[0], l_s[0], acc_s[0]), s, latn)
        for t in range(1, streams):
            mt = m_s[t]
            m_new = jnp.maximum(m, mt)
            a, bt = jnp.exp(m - m_new), jnp.exp(mt - m_new)
            l = a * l + bt * l_s[t]
            acc = a * acc + bt * acc_s[t]
            m = m_new
        o_ref[0] = acc / l


def _paged(page_table, qa, qr, lat_new, kr_new, cache_lat, cache_kr, *, n_new):
    b, npages = page_table.shape
    g = max(d for d in range(1, PAGES_PER_STEP + 1) if npages % d == 0)
    group = max(d for d in range(1, PAGE_GROUP + 1) if g % d == 0)
    streams = min(PAGE_STREAMS, g // group)
    rows = qa.shape[1]
    seq3 = lambda i, s, pt: (i, 0, 0)
    grid_spec = pltpu.PrefetchScalarGridSpec(
        num_scalar_prefetch=1, grid=(b, npages // g),
        in_specs=[pl.BlockSpec((1, rows, KV_LORA), seq3), pl.BlockSpec((1, rows, QK_ROPE), seq3),
                  pl.BlockSpec((1, SROWS, KV_LORA), seq3), pl.BlockSpec((1, SROWS, QK_ROPE), seq3),
                  pl.BlockSpec(memory_space=pl.ANY), pl.BlockSpec(memory_space=pl.ANY)],
        out_specs=pl.BlockSpec((1, rows, KV_LORA), seq3),
        scratch_shapes=[pltpu.VMEM((2, g, PAGE_SIZE, KV_LORA), F32), pltpu.VMEM((2, g, QK_ROPE, PAGE_SIZE), F32),
                        pltpu.SemaphoreType.DMA((2, 2)),
                        pltpu.VMEM((streams, rows, 1), F32), pltpu.VMEM((streams, rows, 1), F32),
                        pltpu.VMEM((streams, rows, KV_LORA), F32)])
    return pl.pallas_call(
        functools.partial(_paged_kernel, n_new=n_new, g=g, group=group, streams=streams), grid_spec=grid_spec,
        out_shape=jax.ShapeDtypeStruct((b, rows, KV_LORA), F32),
        compiler_params=_cparams("arbitrary", "arbitrary"), name="mla_paged")(
            page_table, qa, qr, lat_new, kr_new, cache_lat, cache_kr)


def _bmm_kernel(a_ref, w_ref, o_ref):
    o_ref[0] = _dot(a_ref[...].astype(BF16), w_ref[0]).astype(o_ref.dtype)


def _bmm_cols(a, w, out_dtype):
    n = a.shape[0]
    hh, k, m = w.shape
    return pl.pallas_call(
        _bmm_kernel, grid=(hh,),
        in_specs=[pl.BlockSpec((n, k), lambda h: (0, h)), pl.BlockSpec((1, k, m), lambda h: (h, 0, 0))],
        out_specs=pl.BlockSpec((1, n, m), lambda h: (h, 0, 0)),
        out_shape=jax.ShapeDtypeStruct((hh, n, m), out_dtype), compiler_params=_cparams("parallel"),
        name="bmm_cols")(a, w)


def _bmm_kernel3(a_ref, w_ref, o_ref):
    o_ref[0] = _dot(a_ref[0].astype(BF16), w_ref[0]).astype(o_ref.dtype)


def _bmm(a, w, out_dtype):
    hh, n, k = a.shape
    m = w.shape[2]
    return pl.pallas_call(
        _bmm_kernel3, grid=(hh,),
        in_specs=[pl.BlockSpec((1, n, k), lambda h: (h, 0, 0)), pl.BlockSpec((1, k, m), lambda h: (h, 0, 0))],
        out_specs=pl.BlockSpec((1, n, m), lambda h: (h, 0, 0)),
        out_shape=jax.ShapeDtypeStruct((hh, n, m), out_dtype), compiler_params=_cparams("parallel"),
        name="bmm")(a, w)


PAIRS = RW_HEADS // 2
C_R, C_K, C_V, C_WA, C_G = 0, RW_WIDTH, 2 * RW_WIDTH, 3 * RW_WIDTH, 3 * RW_WIDTH + DECAY_LORA + A_LORA


def _seg_sum(x):
    lane = lax.broadcasted_iota(jnp.int32, (x.shape[0], LANE), 1)
    low = lane < RW_HEAD_DIM
    outs = []
    for t in range(x.shape[1] // LANE):
        xt = x[:, t * LANE:(t + 1) * LANE]
        s0 = jnp.sum(jnp.where(low, xt, 0.0), axis=-1, keepdims=True)
        s1 = jnp.sum(jnp.where(low, 0.0, xt), axis=-1, keepdims=True)
        outs.append(jnp.where(low, s0, s1))
    return outs[0] if len(outs) == 1 else jnp.concatenate(outs, axis=-1)


def _rwkv_prologue(p_ref, s0_ref, so_ref, vec_refs, scratch, *, tb, row_lo, row_hi):
    mu_ref, w0_ref, wa2_ref, a0_ref, g2_ref, kk_ref, ka_ref, rk_ref = vec_refs
    carry_s, r_s, k_s, v_s, kk_s, b_s, ld_s = scratch
    i = pl.program_id(1)

    @pl.when(i == 0)
    def _():
        carry_s[...] = jnp.zeros(carry_s.shape, F32)
        so_ref[0] = s0_ref[0]

    p = p_ref[0]
    rows1 = lax.broadcasted_iota(jnp.int32, (tb, 1), 0)
    prev = jnp.where(rows1 == 0, carry_s[SUBLANE - 1:SUBLANE, :], pltpu.roll(p, 1, axis=0))
    carry_s[...] = p[tb - SUBLANE:tb, :]
    s = p + (prev - p) * mu_ref[...]
    r = s[:, C_R:C_K]
    k = s[:, C_K:C_V]
    v = s[:, C_V:C_WA]
    wa = s[:, C_WA:C_G]
    lane = lax.broadcasted_iota(jnp.int32, wa.shape, 1)
    z = jnp.where(lane < DECAY_LORA, jnp.tanh(wa), wa).astype(BF16)
    lin = _dot(z, wa2_ref[...])
    xw = -(w0_ref[...] + lin[:, :RW_WIDTH])
    w = -(jnp.maximum(xw, 0.0) + jnp.log(1.0 + jnp.exp(-jnp.abs(xw)))) - 0.5
    logd = -jnp.exp(w)
    a = jax.nn.sigmoid(a0_ref[...] + lin[:, RW_WIDTH:])
    gate = _dot(jax.nn.sigmoid(s[:, C_G:RW_PROJ]).astype(BF16), g2_ref[...])
    kk = k * kk_ref[...]
    kk = kk / jnp.maximum(jnp.sqrt(_seg_sum(kk * kk)), 1e-12)
    k = k * (1.0 + (a - 1.0) * ka_ref[...])
    bonus = _seg_sum(r * k * rk_ref[...]) * v
    if row_lo > 0 or row_hi < tb:
        live = (rows1 >= row_lo) & (rows1 < row_hi)
        logd = jnp.where(live, logd, 0.0)
        kk = jnp.where(live, kk, 0.0)
        k = jnp.where(live, k, 0.0)
        v = jnp.where(live, v, 0.0)
    r_s[...] = r
    k_s[...] = k
    v_s[...] = v
    kk_s[...] = kk
    b_s[...] = kk * a
    ld_s[...] = logd
    return gate, bonus


def _rwkv_epilogue(out, gate, bonus, lg_ref, lb_ref, o_ref):
    mean = _seg_sum(out) * (1.0 / RW_HEAD_DIM)
    cen = out - mean
    var = _seg_sum(cen * cen) * (1.0 / RW_HEAD_DIM)
    y = cen * lax.rsqrt(var + LNX_EPS) * lg_ref[...] + lb_ref[...]
    o_ref[0] = ((y + bonus) * gate).astype(o_ref.dtype)


def _rwkv_kernel(p_ref, s0_ref, mu_ref, w0_ref, wa2_ref, a0_ref, g2_ref, kk_ref, ka_ref, rk_ref, lg_ref, lb_ref,
                 o_ref, so_ref, carry_s, r_s, k_s, v_s, kk_s, b_s, ld_s, out_s, *, tb, ch, row_lo, row_hi):
    gate, bonus = _rwkv_prologue(p_ref, s0_ref, so_ref, (mu_ref, w0_ref, wa2_ref, a0_ref, g2_ref, kk_ref, ka_ref, rk_ref),
                                 (carry_s, r_s, k_s, v_s, kk_s, b_s, ld_s), tb=tb, row_lo=row_lo, row_hi=row_hi)
    c2 = 2 * ch
    ri = lax.broadcasted_iota(jnp.int32, (c2, c2), 0)
    ci = lax.broadcasted_iota(jnp.int32, (c2, c2), 1)
    strict = (ci % ch) < (ri % ch)
    incl = (ci % ch) <= (ri % ch)
    eye = (ri == ci).astype(F32)
    tri = (lax.broadcasted_iota(jnp.int32, (ch, ch), 1) <= lax.broadcasted_iota(jnp.int32, (ch, ch), 0)).astype(F32)
    lane2 = lax.broadcasted_iota(jnp.int32, (c2, LANE), 1)
    row2 = lax.broadcasted_iota(jnp.int32, (c2, LANE), 0)
    own = (lane2 < RW_HEAD_DIM) == (row2 < ch)

    def stack(x):
        return jnp.where(own, jnp.concatenate([x, x], axis=0), 0.0)

    bf = lambda x: x.astype(BF16)
    pairs = [slice(pr * LANE, (pr + 1) * LANE) for pr in range(PAIRS)]

    def chunk(c, _):
        ts = pl.ds(pl.multiple_of(c * ch, ch), ch)
        css = [_doth(tri, ld_s[ts, ls]) for ls in pairs]
        lhss, amats, tails = [], [], []
        for ls, cs in zip(pairs, css):
            ld = ld_s[ts, ls]
            e_inv = jnp.exp(-cs)
            lhs = bf(jnp.concatenate([stack(kk_s[ts, ls] * jnp.exp(cs - ld)), stack(r_s[ts, ls] * jnp.exp(cs))], axis=0))
            rhs = bf(jnp.concatenate([stack(k_s[ts, ls] * e_inv), stack(b_s[ts, ls] * e_inv)], axis=0))
            lhss.append(lhs)
            amats.append(_dot_nt(lhs, rhs))
        sts = [so_ref[0, pr] for pr in range(PAIRS)]
        gmats = [_dot_nt(lhs, bf(st)) for lhs, st in zip(lhss, sts)]
        vss = [bf(stack(v_s[ts, ls])) for ls in pairs]
        akbs = [jnp.where(strict, amat[:c2, c2:], 0.0) for amat in amats]
        ykk = [_dot(bf(jnp.where(strict, amat[:c2, :c2], 0.0)), vs) for amat, vs in zip(amats, vss)]
        yrk = [_dot(bf(jnp.where(incl, amat[c2:, :c2], 0.0)), vs) for amat, vs in zip(amats, vss)]
        invs = [eye - jnp.where((ri // 2 == ci // 2), a_kb, 0.0) for a_kb in akbs]
        m = 2
        while m < ch:
            lvl = (ri // (2 * m) == ci // (2 * m)) & ((ri // m) % 2 == 1) & ((ci // m) % 2 == 0)
            tmps = [_dot(bf(inv), bf(jnp.where(lvl, a_kb, 0.0))) for inv, a_kb in zip(invs, akbs)]
            invs = [inv - _dot(bf(tmp), bf(inv)) for inv, tmp in zip(invs, tmps)]
            m *= 2
        us = [_dot(bf(inv), bf(gmat[:c2] + y)) for inv, gmat, y in zip(invs, gmats, ykk)]
        o2s = [gmat[c2:] + y - _dot(bf(jnp.where(incl, amat[c2:, c2:], 0.0)), bf(u))
               for gmat, y, amat, u in zip(gmats, yrk, amats, us)]
        for pr, (ls, cs, st, vs, u, o2) in enumerate(zip(pairs, css, sts, vss, us, o2s)):
            tot = cs[ch - 1:ch, :]
            e_tail = jnp.exp(tot - cs)
            out_s[ts, ls] = o2[:ch] + o2[ch:]
            so_ref[0, pr] = (st * jnp.exp(tot) + _dot_tn(vs, bf(stack(k_s[ts, ls] * e_tail)))
                             - _dot_tn(bf(u), bf(stack(b_s[ts, ls] * e_tail))))
        return 0

    lax.fori_loop(0, tb // ch, chunk, 0)
    _rwkv_epilogue(out_s[...], gate, bonus, lg_ref, lb_ref, o_ref)


def _rwkv(p, s0, mu, w0, wa2, a0, g2, k_k, k_a, r_k, lnx_g, lnx_b, *, tb, ch, row_lo, row_hi):
    b, t, _ = p.shape
    const = lambda bi, i: (0, 0)
    vec = lambda a: pl.BlockSpec(a.shape, const)
    scr = lambda: pltpu.VMEM((tb, RW_WIDTH), F32)
    return pl.pallas_call(
        functools.partial(_rwkv_kernel, tb=tb, ch=ch, row_lo=row_lo, row_hi=row_hi), grid=(b, t // tb),
        in_specs=[pl.BlockSpec((1, tb, RW_PROJ), lambda bi, i: (bi, i, 0)),
                  pl.BlockSpec((1, PAIRS, LANE, LANE), lambda bi, i: (bi, 0, 0, 0)),
                  vec(mu), vec(w0), vec(wa2), vec(a0), vec(g2), vec(k_k), vec(k_a), vec(r_k), vec(lnx_g), vec(lnx_b)],
        out_specs=[pl.BlockSpec((1, tb, RW_WIDTH), lambda bi, i: (bi, i, 0)),
                   pl.BlockSpec((1, PAIRS, LANE, LANE), lambda bi, i: (bi, 0, 0, 0))],
        out_shape=[jax.ShapeDtypeStruct((b, t, RW_WIDTH), BF16), jax.ShapeDtypeStruct((b, PAIRS, LANE, LANE), F32)],
        scratch_shapes=[pltpu.VMEM((SUBLANE, RW_PROJ), F32)] + [scr() for _ in range(7)],
        compiler_params=_cparams("parallel", "arbitrary"), name="rwkv")(
            p, s0, mu, w0, wa2, a0, g2, k_k, k_a, r_k, lnx_g, lnx_b)


QUAD = 4
QCH = RW_HEAD_DIM
QW = QUAD * RW_HEAD_DIM
QLEVELS = (2, 4, 8, 16, 32)
PREP_CHUNKS = 4


def _tile4(x):
    return jnp.concatenate([x] * QUAD, axis=0)


def _rwkv_quad_kernel(p_ref, s0_ref, mu_ref, w0_ref, wa2_ref, a0_ref, g2_ref, kk_ref, ka_ref, rk_ref, lg_ref, lb_ref,
                      tri_ref, bdm_ref, own_ref, sbm_ref,
                      o_ref, so_ref, carry_s, r_s, k_s, v_s, kk_s, b_s, ld_s, out_s,
                      lhs_c, arb_c, y_c, kt_c, dec_c, *, tb):
    gate, bonus = _rwkv_prologue(p_ref, s0_ref, so_ref, (mu_ref, w0_ref, wa2_ref, a0_ref, g2_ref, kk_ref, ka_ref, rk_ref),
                                 (carry_s, r_s, k_s, v_s, kk_s, b_s, ld_s), tb=tb, row_lo=0, row_hi=tb)
    ch = QCH
    nquad = RW_HEADS // QUAD
    bf = lambda x: x.astype(BF16)

    def bd(x):
        return _tile4(bf(x)) * bdm_ref[0]

    quads = [slice(q * QW, (q + 1) * QW) for q in range(nquad)]

    def prepare(chunks):
        inst = [(c, q, pl.ds(pl.multiple_of(c * ch, ch), ch), quads[q]) for c in chunks for q in range(nquad)]
        strict, incl, eye, lvl1 = sbm_ref[0] > 0, sbm_ref[1] > 0, sbm_ref[2], sbm_ref[3]
        css = []
        for c, q, ts, ls in inst:
            ld = ld_s[ts, ls]
            l1 = bf(ld)
            r1 = ld - l1.astype(F32)
            l2 = bf(r1)
            l3 = bf(r1 - l2.astype(F32))
            css.append(_dot(tri_ref[...], jnp.concatenate([l1, l2, l3], axis=0)))
        a1s, a2s = [], []
        for (c, q, ts, ls), cs in zip(inst, css):
            ld = ld_s[ts, ls]
            tot = cs[ch - 1:ch, :]
            e_inv = jnp.exp(-cs)
            kc, bc = k_s[ts, ls], b_s[ts, ls]
            lhs = bf(jnp.concatenate([kk_s[ts, ls] * jnp.exp(cs - ld), r_s[ts, ls] * jnp.exp(cs)], axis=0))
            lhs_c[c, q] = lhs
            e_tail = jnp.exp(tot - cs)
            kt_c[c, q] = bf(jnp.concatenate([kc * e_tail, bc * e_tail], axis=0))
            dec_c[c, q] = jnp.broadcast_to(jnp.exp(tot), (SUBLANE, QW))
            a1s.append(_dot_nt(lhs, bd(kc * e_inv)))
            a2s.append(_dot_nt(lhs, bd(bc * e_inv)))
        ys, akb4s, invs = [], [], []
        for (c, q, ts, ls), a1, a2 in zip(inst, a1s, a2s):
            a_kk = jnp.where(strict, a1[:ch], 0.0)
            a_rk = jnp.where(incl, a1[ch:], 0.0)
            a_kb = jnp.where(strict, a2[:ch], 0.0)
            arb_c[c, q] = bf(jnp.where(incl, a2[ch:], 0.0))
            ys.append(_dot(bf(jnp.concatenate([a_kk, a_rk], axis=0)), bd(v_s[ts, ls])))
            akb4s.append(_tile4(bf(a_kb)))
            invs.append(eye - a_kb * lvl1)
        for li in range(len(QLEVELS)):
            tmps = [_dot(bf(inv), akb4 * bdm_ref[1 + li]) for inv, akb4 in zip(invs, akb4s)]
            invs = [inv - _dot(bf(tmp), bd(inv)) for inv, tmp in zip(invs, tmps)]
        ws = [_dot(bf(inv), _tile4(lhs_c[c, q][:ch]) * bdm_ref[0]) for (c, q, ts, ls), inv in zip(inst, invs)]
        yks = [_dot(bf(inv), bd(y[:ch])) for inv, y in zip(invs, ys)]
        for (c, q, ts, ls), w, yk, y in zip(inst, ws, yks, ys):
            lhs_c[c, q, 0:ch, :] = bf(w)
            y_c[c, q] = jnp.concatenate([yk, y[ch:]], axis=0)

    nc = tb // ch
    if nc % PREP_CHUNKS == 0:
        def prepare_loop(cc, _):
            prepare([PREP_CHUNKS * cc + d for d in range(PREP_CHUNKS)])
            return 0
        lax.fori_loop(0, nc // PREP_CHUNKS, prepare_loop, 0)
    else:
        lax.fori_loop(0, nc, lambda c, _: (prepare([c]), 0)[1], 0)

    def recur(c, _):
        ts = pl.ds(pl.multiple_of(c * ch, ch), ch)
        sts = [so_ref[0, q] for q in range(nquad)]
        gys = [_dot_nt(lhs_c[c, q], bf(sts[q])) + y_c[c, q] for q in range(nquad)]
        us = [gys[q][:ch] for q in range(nquad)]
        outs = [gys[q][ch:] - _dot(arb_c[c, q], bd(us[q])) for q in range(nquad)]
        upds = [_dot_tn(bf(jnp.concatenate([v_s[ts, quads[q]], -us[q]], axis=0)), kt_c[c, q]) for q in range(nquad)]
        for q in range(nquad):
            out_s[ts, quads[q]] = outs[q]
            so_ref[0, q] = sts[q] * dec_c[c, q][0:1, :] + upds[q] * own_ref[...]
        return 0

    lax.fori_loop(0, nc, recur, 0)
    _rwkv_epilogue(out_s[...], gate, bonus, lg_ref, lb_ref, o_ref)


def _quad_masks():
    ch = QCH
    r = np.arange(QUAD * ch)[:, None]
    c = np.arange(QW)[None, :]
    own = (r // ch) == (c // RW_HEAD_DIM)
    sp, s = r % ch, c % ch
    bdm = [own]
    for m in QLEVELS:
        bdm.append(own & (sp // (2 * m) == s // (2 * m)) & ((sp // m) % 2 == 1) & ((s // m) % 2 == 0))
    t = np.arange(ch)[:, None]
    sbm = [s < t, s <= t, s == t, (t // 2 == s // 2) & (t % 2 == 1) & (s % 2 == 0)]
    tri = np.tile(np.arange(ch)[None, :] <= np.arange(ch)[:, None], (1, 3))
    return (jnp.asarray(tri, BF16), jnp.asarray(np.stack(bdm), BF16), jnp.asarray(own, F32),
            jnp.asarray(np.stack([np.broadcast_to(m, (ch, QW)) for m in sbm]), F32))


def _rwkv_quad(p, s0, mu, w0, wa2, a0, g2, k_k, k_a, r_k, lnx_g, lnx_b, *, tb):
    b, t, _ = p.shape
    nq = RW_HEADS // QUAD
    nc = tb // QCH
    masks = _quad_masks()
    vec = lambda a: pl.BlockSpec(a.shape, lambda bi, i: (0,) * a.ndim)
    scr = lambda: pltpu.VMEM((tb, RW_WIDTH), F32)
    vecs = (mu, w0, wa2, a0, g2, k_k, k_a, r_k, lnx_g, lnx_b) + masks
    return pl.pallas_call(
        functools.partial(_rwkv_quad_kernel, tb=tb), grid=(b, t // tb),
        in_specs=[pl.BlockSpec((1, tb, RW_PROJ), lambda bi, i: (bi, i, 0)),
                  pl.BlockSpec((1, nq, QW, QW), lambda bi, i: (bi, 0, 0, 0))] + [vec(a) for a in vecs],
        out_specs=[pl.BlockSpec((1, tb, RW_WIDTH), lambda bi, i: (bi, i, 0)),
                   pl.BlockSpec((1, nq, QW, QW), lambda bi, i: (bi, 0, 0, 0))],
        out_shape=[jax.ShapeDtypeStruct((b, t, RW_WIDTH), BF16), jax.ShapeDtypeStruct((b, nq, QW, QW), F32)],
        scratch_shapes=[pltpu.VMEM((SUBLANE, RW_PROJ), F32)] + [scr() for _ in range(7)] + [
            pltpu.VMEM((nc, nq, 2 * QCH, QW), BF16),
            pltpu.VMEM((nc, nq, QCH, QW), BF16), pltpu.VMEM((nc, nq, 2 * QCH, QW), F32),
            pltpu.VMEM((nc, nq, 2 * QCH, QW), BF16), pltpu.VMEM((nc, nq, SUBLANE, QW), F32)],
        compiler_params=_cparams("parallel", "arbitrary"), name="rwkv_quad")(p, s0, *vecs)


def _quads_to_state(sq):
    b = sq.shape[0]
    s6 = sq.reshape(b, RW_HEADS // QUAD, QUAD, RW_HEAD_DIM, QUAD, RW_HEAD_DIM)
    return jnp.stack([s6[:, :, h, :, h, :] for h in range(QUAD)], axis=2).reshape(b, RW_HEADS, RW_HEAD_DIM, RW_HEAD_DIM)


def _mix_kernel(x_ref, orw_ref, omla_ref, g_ref, wa_ref, wb_ref, o_ref):
    om = _rms(omla_ref[...], g_ref[...]).astype(BF16)
    o_ref[...] = x_ref[...] + _dot(orw_ref[...], wa_ref[...]) + _dot(om, wb_ref[...])


def _mix(x, orw, omla, g, wa, wb, *, tm):
    n, d = x.shape
    row = lambda i: (i, 0)
    const = lambda i: (0, 0)
    return pl.pallas_call(
        _mix_kernel, grid=(n // tm,),
        in_specs=[pl.BlockSpec((tm, d), row), pl.BlockSpec((tm, orw.shape[1]), row), pl.BlockSpec((tm, omla.shape[1]), row),
                  pl.BlockSpec(g.shape, const), pl.BlockSpec(wa.shape, const), pl.BlockSpec(wb.shape, const)],
        out_specs=pl.BlockSpec((tm, d), row), out_shape=jax.ShapeDtypeStruct((n, d), F32),
        compiler_params=_cparams("parallel"), name="mix_out")(x, orw, omla, g, wa, wb)


def _norm_mm_kernel(x_ref, g_ref, w_ref, o_ref):
    o_ref[...] = _dot(_rms(x_ref[...], g_ref[...]).astype(BF16), w_ref[...]).astype(o_ref.dtype)


def _norm_mm(x, g, w, out_dtype, *, tm):
    n, d = x.shape
    m = w.shape[1]
    return pl.pallas_call(
        _norm_mm_kernel, grid=(n // tm,),
        in_specs=[pl.BlockSpec((tm, d), lambda i: (i, 0)), pl.BlockSpec(g.shape, lambda i: (0, 0)),
                  pl.BlockSpec(w.shape, lambda i: (0, 0))],
        out_specs=pl.BlockSpec((tm, m), lambda i: (i, 0)), out_shape=jax.ShapeDtypeStruct((n, m), out_dtype),
        compiler_params=_cparams("parallel"), name="norm_mm")(x, g, w)


def _mm_res_kernel(a_ref, w_ref, x_ref, o_ref):
    o_ref[...] = x_ref[...] + _dot(a_ref[...], w_ref[...])


def _mm_res(a, w, x, *, tm):
    n, d = x.shape
    return pl.pallas_call(
        _mm_res_kernel, grid=(n // tm,),
        in_specs=[pl.BlockSpec((tm, a.shape[1]), lambda i: (i, 0)), pl.BlockSpec(w.shape, lambda i: (0, 0)),
                  pl.BlockSpec((tm, d), lambda i: (i, 0))],
        out_specs=pl.BlockSpec((tm, d), lambda i: (i, 0)), out_shape=jax.ShapeDtypeStruct((n, d), F32),
        compiler_params=_cparams("parallel"), name="mm_res")(a, w, x)


def _memattn_kernel(q_ref, k_ref, v_ref, o_ref):
    dh = q_ref.shape[2] // MEM_HEADS
    scale = dh ** -0.5
    for h in range(MEM_HEADS):
        cs = slice(h * dh, (h + 1) * dh)
        s = _dot_nt(q_ref[0, :, cs], k_ref[0, :, cs].astype(BF16)) * scale
        p = jnp.exp(s - jnp.max(s, axis=-1, keepdims=True))
        l = jnp.sum(p, axis=-1, keepdims=True)
        o_ref[0, :, cs] = (_dot(p.astype(BF16), v_ref[0, :, cs].astype(BF16)) / l).astype(o_ref.dtype)


def _memattn_tiled_kernel(q_ref, k_ref, v_ref, o_ref, *, nm):
    dh = q_ref.shape[2] // MEM_HEADS
    nj = dh // LANE
    scale = dh ** -0.5
    for h in range(MEM_HEADS):
        rows = [pl.ds(j * MEM_HEADS + h, nm, stride=nj * MEM_HEADS) for j in range(nj)]
        s = sum(_dot_nt(q_ref[0, :, h * dh + j * LANE:h * dh + (j + 1) * LANE], k_ref[0, rows[j], :].astype(BF16))
                for j in range(nj)) * scale
        p = jnp.exp(s - jnp.max(s, axis=-1, keepdims=True))
        l = jnp.sum(p, axis=-1, keepdims=True)
        pb = p.astype(BF16)
        for j in range(nj):
            o_ref[0, :, h * dh + j * LANE:h * dh + (j + 1) * LANE] = (
                _dot(pb, v_ref[0, rows[j], :].astype(BF16)) / l).astype(o_ref.dtype)


def _memattn_tiled(q, mk, mv, *, nm):
    b, t, d = q.shape
    mem_spec = pl.BlockSpec((1,) + mk.shape[1:], lambda bi: (bi, 0, 0))
    return pl.pallas_call(
        functools.partial(_memattn_tiled_kernel, nm=nm), grid=(b,),
        in_specs=[pl.BlockSpec((1, t, d), lambda bi: (bi, 0, 0)), mem_spec, mem_spec],
        out_specs=pl.BlockSpec((1, t, d), lambda bi: (bi, 0, 0)), out_shape=jax.ShapeDtypeStruct((b, t, d), BF16),
        compiler_params=_cparams("parallel"), name="mem_attn_tiled")(q, mk, mv)


def _memattn(q, mk, mv, *, tm):
    b, t, d = q.shape
    mem_spec = pl.BlockSpec((1,) + mk.shape[1:], lambda bi, i: (bi,) + (0,) * (mk.ndim - 1))
    return pl.pallas_call(
        _memattn_kernel, grid=(b, t // tm),
        in_specs=[pl.BlockSpec((1, tm, d), lambda bi, i: (bi, i, 0)), mem_spec, mem_spec],
        out_specs=pl.BlockSpec((1, tm, d), lambda bi, i: (bi, i, 0)), out_shape=jax.ShapeDtypeStruct((b, t, d), BF16),
        compiler_params=_cparams("parallel", "parallel"), name="mem_attn")(q, mk, mv)


def _memblock_kernel(x_ref, orw_ref, omla_ref, gmo_ref, wa_ref, wb_ref, gq_ref, wq_ref, k_ref, v_ref, wo_ref, o_ref):
    om = _rms(omla_ref[0], gmo_ref[...]).astype(BF16)
    x1 = x_ref[0] + _dot(orw_ref[0], wa_ref[...]) + _dot(om, wb_ref[...])
    q = _dot(_rms(x1, gq_ref[...]).astype(BF16), wq_ref[...]).astype(BF16)
    dh = q.shape[1] // MEM_HEADS
    scale = dh ** -0.5
    outs = []
    for h in range(MEM_HEADS):
        cs = slice(h * dh, (h + 1) * dh)
        s = _dot_nt(q[:, cs], k_ref[0, :, cs].astype(BF16)) * scale
        p = jnp.exp(s - jnp.max(s, axis=-1, keepdims=True))
        l = jnp.sum(p, axis=-1, keepdims=True)
        outs.append((_dot(p.astype(BF16), v_ref[0, :, cs].astype(BF16)) / l).astype(BF16))
    o_ref[0] = x1 + _dot(jnp.concatenate(outs, axis=1), wo_ref[...])


def _memblock(x, orw, omla, gmo, wa, wb, gq, wq, mk, mv, wo, *, tm):
    b, t, d = x.shape
    row = lambda bi, i: (bi, i, 0)
    seq = lambda bi, i: (bi, 0, 0)
    const = lambda bi, i: (0, 0)
    cs = lambda a: pl.BlockSpec(a.shape, const)
    return pl.pallas_call(
        _memblock_kernel, grid=(b, t // tm),
        in_specs=[pl.BlockSpec((1, tm, d), row), pl.BlockSpec((1, tm, orw.shape[2]), row), pl.BlockSpec((1, tm, omla.shape[2]), row),
                  cs(gmo), cs(wa), cs(wb), cs(gq), cs(wq), pl.BlockSpec((1,) + mk.shape[1:], seq),
                  pl.BlockSpec((1,) + mv.shape[1:], seq), cs(wo)],
        out_specs=pl.BlockSpec((1, tm, d), row), out_shape=jax.ShapeDtypeStruct((b, t, d), F32),
        compiler_params=_cparams("parallel", "parallel"), name="mem_block")(x, orw, omla, gmo, wa, wb, gq, wq, mk, mv, wo)


FFN_CHUNK = 256
FFN_GROUP = 2


def _ffn_kernel(x_ref, prev_ref, g_ref, wup_ref, cw_ref, cb_ref, wdn_ref, gf_ref, y_ref, u_ref, act_s, ua_s, ub_s,
                *, tm, nch, prev_rows):
    i = pl.program_id(1)
    x = x_ref[0]
    h = _rms(x, g_ref[...]).astype(BF16)
    rows = lax.broadcasted_iota(jnp.int32, (tm, 1), 0)
    if prev_rows is None:
        @pl.when(i == 0)
        def _():
            u_ref[0] = prev_ref[0]
    else:
        hist = (rows % SROWS) < prev_rows

    def up(c):
        return _dot(h, wup_ref[c]), _dot(h, wup_ref[c + nch])

    def conv(c, u):
        if prev_rows is None:
            tail = u_ref[0, c]
            u_ref[0, c] = u[tm - SUBLANE:tm, :]
            p1, p2 = tail[SUBLANE - 1:SUBLANE, :], tail[SUBLANE - 2:SUBLANE - 1, :]
            u1, u2 = pltpu.roll(u, 1, axis=0), pltpu.roll(u, 2, axis=0)
            r8 = rows[:SUBLANE]
            u1 = jnp.concatenate([jnp.where(r8 == 0, p1, u1[:SUBLANE]), u1[SUBLANE:]], axis=0)
            u2 = jnp.concatenate([jnp.where(r8 == 0, p2, jnp.where(r8 == 1, p1, u2[:SUBLANE])), u2[SUBLANE:]], axis=0)
        else:
            u = jnp.where(hist, prev_ref[c], u)
            u_ref[c] = u
            u1 = pltpu.roll(u, 1, axis=0)
            u2 = pltpu.roll(u, 2, axis=0)
        cw = cw_ref[c]
        return cb_ref[c] + cw[0:1, :] * u2 + cw[1:2, :] * u1 + cw[2:3, :] * u

    def gated(c, ug, uv):
        gate = conv(c, ug)
        val = conv(c + nch, uv)
        act_s[c] = (gate * jax.nn.sigmoid(gate) * val).astype(BF16)

    groups = [list(range(s, min(s + FFN_GROUP, nch))) for s in range(0, nch, FFN_GROUP)]
    bufs = (ua_s, ub_s)

    def issue(gi):
        for d, c in enumerate(groups[gi]):
            bufs[gi % 2][2 * d], bufs[gi % 2][2 * d + 1] = up(c)

    def consume(gi):
        for d, c in enumerate(groups[gi]):
            gated(c, bufs[gi % 2][2 * d], bufs[gi % 2][2 * d + 1])

    issue(0)
    for gi in range(len(groups)):
        if gi + 1 < len(groups):
            issue(gi + 1)
        consume(gi)
    f = _dot(jnp.concatenate([act_s[c] for c in range(nch)], axis=1), wdn_ref[...])
    y_ref[0] = _rms(x + f, gf_ref[...])


def _ffn(x, prev, g, wup, cw, cb, wdn, gf, *, tm, prev_rows):
    b, t, d = x.shape
    nch = wup.shape[0] // 2
    c3 = lambda bi, i: (0, 0, 0)
    if prev_rows is None:
        prev_spec = pl.BlockSpec((1, 2 * nch, SUBLANE, FFN_CHUNK), lambda bi, i: (bi, 0, 0, 0))
        u_spec = pl.BlockSpec((1, 2 * nch, SUBLANE, FFN_CHUNK), lambda bi, i: (bi, 0, 0, 0))
        u_shape = jax.ShapeDtypeStruct((b, 2 * nch, SUBLANE, FFN_CHUNK), F32)
    else:
        prev_spec = pl.BlockSpec((2 * nch, tm, FFN_CHUNK), lambda bi, i: (0, bi * (t // tm) + i, 0))
        u_spec = pl.BlockSpec((2 * nch, tm, FFN_CHUNK), lambda bi, i: (0, bi * (t // tm) + i, 0))
        u_shape = jax.ShapeDtypeStruct((2 * nch, b * t, FFN_CHUNK), F32)
    return pl.pallas_call(
        functools.partial(_ffn_kernel, tm=tm, nch=nch, prev_rows=prev_rows), grid=(b, t // tm),
        in_specs=[pl.BlockSpec((1, tm, d), lambda bi, i: (bi, i, 0)), prev_spec, pl.BlockSpec(g.shape, lambda bi, i: (0, 0)),
                  pl.BlockSpec(wup.shape, c3), pl.BlockSpec(cw.shape, c3), pl.BlockSpec(cb.shape, c3),
                  pl.BlockSpec(wdn.shape, lambda bi, i: (0, 0)), pl.BlockSpec(gf.shape, lambda bi, i: (0, 0))],
        out_specs=[pl.BlockSpec((1, tm, d), lambda bi, i: (bi, i, 0)), u_spec],
        out_shape=[jax.ShapeDtypeStruct((b, t, d), F32), u_shape],
        scratch_shapes=[pltpu.VMEM((nch, tm, FFN_CHUNK), BF16), pltpu.VMEM((2 * FFN_GROUP, tm, FFN_CHUNK), F32),
                        pltpu.VMEM((2 * FFN_GROUP, tm, FFN_CHUNK), F32)],
        compiler_params=_cparams("parallel", "arbitrary"), name="conv_ffn")(x, prev, g, wup, cw, cb, wdn, gf)


def _rope_tables(pos, scale):
    half = QK_ROPE // 2
    inv = 1.0 / (ROPE_THETA ** (np.arange(half, dtype=np.float64) / half))
    ang = np.asarray(pos, np.float64)[:, None] * inv[None, :]
    cos, sin = np.cos(ang), np.sin(ang)
    n = len(pos)
    c = np.zeros((n, HEAD_PAD))
    s = np.zeros((n, HEAD_PAD))
    c[:, :QK_NOPE] = 1.0
    c[:, ROPE_LO:ROPE_LO + half] = cos
    c[:, ROPE_LO + half:ROPE_LO + QK_ROPE] = cos
    s[:, ROPE_LO:ROPE_LO + half] = -sin
    s[:, ROPE_LO + half:ROPE_LO + QK_ROPE] = sin
    return (jnp.asarray(c * scale, F32), jnp.asarray(s * scale, F32), jnp.asarray(c, F32), jnp.asarray(s, F32))


def _swap_halves(w):
    half = w.shape[-1] // 2
    return jnp.concatenate([w[..., half:], w[..., :half]], axis=-1)


def _prep_weights(w_in, w_uq, w_ukv, rw_w2, rw_a2, w_up, conv_w, conv_b, w_down):
    d = w_in.shape[0]
    z = lambda *s: jnp.zeros(s, F32)
    w_kr = w_in[:, Q_LORA + KV_LORA:MLA_PROJ]
    pad_head = lambda w: jnp.concatenate([z(d, ROPE_LO), w, z(d, HEAD_PAD - ROPE_LO - QK_ROPE)], axis=1)
    w1 = jnp.concatenate([w_in[:, :Q_LORA + KV_LORA], pad_head(w_kr), pad_head(_swap_halves(w_kr)), w_in[:, MLA_PROJ:]],
                         axis=1).astype(BF16)
    wq3 = w_uq.reshape(Q_LORA, MLA_HEADS, QK_NOPE + QK_ROPE)
    zq = lambda n: z(Q_LORA, MLA_HEADS, n)
    q_plain = jnp.concatenate([wq3, zq(HEAD_PAD - QK_NOPE - QK_ROPE)], axis=2)
    q_swap = jnp.concatenate([zq(QK_NOPE), _swap_halves(wq3[..., QK_NOPE:]), zq(HEAD_PAD - QK_NOPE - QK_ROPE)], axis=2)
    wq = jnp.concatenate([q_plain.reshape(Q_LORA, -1), q_swap.reshape(Q_LORA, -1)], axis=1).astype(BF16)
    wkv3 = w_ukv.reshape(KV_LORA, MLA_HEADS, QK_NOPE + V_DIM)
    zk = z(KV_LORA, MLA_HEADS, HEAD_PAD - QK_NOPE)
    wk = jnp.concatenate([wkv3[..., :QK_NOPE], zk], axis=2).reshape(KV_LORA, -1).astype(BF16)
    wv = wkv3[..., QK_NOPE:].reshape(KV_LORA, -1).T.astype(BF16)
    w_uk = jnp.transpose(wkv3[..., :QK_NOPE], (1, 2, 0))
    w_uk = jnp.concatenate([w_uk, z(MLA_HEADS, HEAD_PAD - QK_NOPE, KV_LORA)], axis=1).astype(BF16)
    w_uv = jnp.transpose(wkv3[..., QK_NOPE:], (1, 0, 2)).astype(BF16)
    wa2 = jnp.concatenate([jnp.concatenate([rw_w2, z(A_LORA, RW_WIDTH)], axis=0),
                           jnp.concatenate([z(DECAY_LORA, RW_WIDTH), rw_a2], axis=0)], axis=1).astype(BF16)
    f2 = w_up.shape[1]
    nch2 = f2 // FFN_CHUNK
    wup = jnp.transpose(w_up.reshape(d, nch2, FFN_CHUNK), (1, 0, 2)).astype(BF16)
    cw = jnp.transpose(conv_w.reshape(CONV_W, nch2, FFN_CHUNK), (1, 0, 2))
    cw = jnp.concatenate([cw, z(nch2, SUBLANE - CONV_W, FFN_CHUNK)], axis=1)
    cb = conv_b.reshape(nch2, 1, FFN_CHUNK)
    wdn = w_down.astype(BF16)
    return w1, wq, wk, wv, w_uk, w_uv, wa2, wup, cw, cb, wdn


def _state_to_pairs(s):
    b = s.shape[0]
    s = s.reshape(b, PAIRS, 2, RW_HEAD_DIM, RW_HEAD_DIM)
    zz = jnp.zeros_like(s[:, :, 0])
    top = jnp.concatenate([s[:, :, 0], zz], axis=-1)
    bot = jnp.concatenate([zz, s[:, :, 1]], axis=-1)
    return jnp.concatenate([top, bot], axis=-2)


def _pairs_to_state(s):
    b = s.shape[0]
    h0 = s[:, :, :RW_HEAD_DIM, :RW_HEAD_DIM]
    h1 = s[:, :, RW_HEAD_DIM:, RW_HEAD_DIM:]
    return jnp.stack([h0, h1], axis=2).reshape(b, RW_HEADS, RW_HEAD_DIM, RW_HEAD_DIM)


def _pick(n, pref):
    for t in pref:
        if n % t == 0:
            return t
    return n


def kernel(x_prompt, x_sample, cache_mla_latent, cache_mla_krope, cache_mem_k, cache_mem_v, state_rwkv, state_rwkv_shift, state_ffn_conv, page_table, mem_prompt, g_mix, w_in, q_norm_g, kv_norm_g, w_uq, w_ukv, g_mla_out, rw_mu, rw_w0, rw_w2, rw_a0, rw_a2, rw_g2, rw_k_k, rw_k_a, rw_r_k, rw_lnx_g, rw_lnx_b, w_o, g_mem_q, g_mem_kv, w_mq, w_mk, w_mv, w_mo, g_ffn, w_up, conv_w, conv_b, w_down, g_final):
    depth = w_in.shape[0]
    assert depth == 1, "single-layer step"
    bp, tp, d = x_prompt.shape
    bs, ts, _ = x_sample.shape
    npages = page_table.shape[1]
    past_len = npages * PAGE_SIZE
    row2 = lambda a: a.reshape(1, -1)
    l = 0
    w1, wq, wk, wv, w_uk, w_uv, wa2, wup, cw, cb, wdn = _prep_weights(
        w_in[l], w_uq[l], w_ukv[l], rw_w2[l], rw_a2[l], w_up[l], conv_w[l], conv_b[l], w_down[l])
    wo_a, wo_b = w_o[l, :RW_WIDTH].astype(BF16), w_o[l, RW_WIDTH:].astype(BF16)
    wmq, wmo = w_mq[l].astype(BF16), w_mo[l].astype(BF16)
    wmkv = jnp.concatenate([w_mk[l], w_mv[l]], axis=1).astype(BF16)
    g2 = rw_g2[l].astype(BF16)
    nch2 = wup.shape[0]
    n_mem_s = cache_mem_k.shape[2]
    rw_vecs = (row2(rw_mu[l]), row2(rw_w0[l]), wa2, row2(rw_a0[l]), g2, row2(rw_k_k[l]), row2(rw_k_a[l]),
               row2(rw_r_k[l]), row2(rw_lnx_g[l]), row2(rw_lnx_b[l]))

    def after_attention(x2, orw, omla, mem_k, mem_v, conv_in, b, t, tm, tm_mem, tm_ffn, prev_rows):
        if prev_rows is None:
            xm = _memblock(x2.reshape(b, t, d), orw.reshape(b, t, -1), omla.reshape(b, t, -1), row2(g_mla_out[l]),
                           wo_a, wo_b, row2(g_mem_q[l]), wmq, mem_k, mem_v, wmo, tm=tm)
        else:
            x1 = _mix(x2, orw, omla, row2(g_mla_out[l]), wo_a, wo_b, tm=tm)
            qm = _norm_mm(x1, row2(g_mem_q[l]), wmq, BF16, tm=tm)
            om = _memattn_tiled(qm.reshape(b, t, d), mem_k, mem_v, nm=n_mem_s)
            xm = _mm_res(om.reshape(b * t, d), wmo, x1, tm=tm)
        fb, ft = (b, t) if prev_rows is None else (1, b * t)
        return _ffn(xm.reshape(fb, ft, d), conv_in, row2(g_ffn[l]), wup, cw, cb, wdn, row2(g_final), tm=tm_ffn,
                    prev_rows=prev_rows)

    n_p = bp * tp
    tm_p = _pick(tp, (512, 256, 128, 64, 32, 16, 8))
    tabs_p = _rope_tables(np.arange(tp), MLA_SCALE * LOG2E)
    xp2 = x_prompt.reshape(n_p, d)
    tq = _pick(tm_p, (FLASH_T, 256, 128))
    q_p, lat_p, kr_p, prw_p, k_p, v_p = _inproj(
        xp2, row2(g_mix[l]), w1, row2(q_norm_g[l]), row2(kv_norm_g[l]), wq, tabs_p, wk, wv, tm=tm_p, with_kv=True, seq=tp, vblock=tq)
    omla_p = _flash(q_p, k_p, v_p, batch=bp, seq=tp, tq=tq)
    tb_p = _pick(tp, (512, 256, 128, 64, 32, 16, 8))
    if tb_p % QCH == 0:
        zeros_state = jnp.zeros((bp, RW_HEADS // QUAD, QW, QW), F32)
        orw_p, st_p = _rwkv_quad(prw_p.reshape(bp, tp, RW_PROJ), zeros_state, *rw_vecs, tb=tb_p)
        st_p = _quads_to_state(st_p)
    else:
        zeros_state = jnp.zeros((bp, PAIRS, LANE, LANE), F32)
        orw_p, st_p = _rwkv(prw_p.reshape(bp, tp, RW_PROJ), zeros_state, *rw_vecs, tb=tb_p,
                            ch=_pick(tb_p, (32, 16, 8)), row_lo=0, row_hi=tb_p)
        st_p = _pairs_to_state(st_p)
    mkv = _norm_mm(mem_prompt.reshape(-1, d), row2(g_mem_kv[l]), wmkv, F32, tm=_pick(mem_prompt.shape[0] * mem_prompt.shape[1], (512, 256, 128, 8)))
    n_mem = mem_prompt.shape[1]
    mk_p = mkv[:, :d].reshape(bp, n_mem, d)
    mv_p = mkv[:, d:].reshape(bp, n_mem, d)
    conv0_p = jnp.zeros((bp, nch2, SUBLANE, FFN_CHUNK), F32)
    y_p, u_p = after_attention(xp2, orw_p.reshape(n_p, RW_WIDTH), omla_p, mk_p, mv_p, conv0_p, bp, tp, tm_p, tm_p, tm_p, None)
    conv_p = jnp.transpose(u_p[:, :, SUBLANE - (CONV_W - 1):, :], (0, 2, 1, 3)).reshape(bp, CONV_W - 1, nch2 * FFN_CHUNK)

    n_s = bs * SROWS
    xs3 = jnp.pad(x_sample, ((0, 0), (SLO, SROWS - SLO - ts), (0, 0)))
    pos_s = np.tile(np.concatenate([np.zeros(SLO), past_len + np.arange(ts), np.zeros(SROWS - SLO - ts)]), bs)
    tm_s = _pick(n_s, (1024, 512, 256, 128, 64, 32, 16, 8))
    tabs_s = _rope_tables(pos_s[:tm_s], MLA_SCALE)
    q_s, lat_s, kr_s, prw_s = _inproj(
        xs3.reshape(n_s, d), row2(g_mix[l]), w1, row2(q_norm_g[l]), row2(kv_norm_g[l]), wq, tabs_s, None, None,
        tm=tm_s, with_kv=False)
    qabs = _bmm_cols(q_s, w_uk, BF16)
    qabs = qabs.reshape(MLA_HEADS, bs, SROWS, KV_LORA)[:, :, SLO:SLO + ts]
    qabs = jnp.transpose(qabs, (1, 0, 2, 3)).reshape(bs, MLA_HEADS * ts, KV_LORA)
    qrope = q_s.reshape(bs, SROWS, MLA_HEADS, HEAD_PAD)[:, SLO:SLO + ts, :, ROPE_LO:ROPE_LO + QK_ROPE]
    qrope = jnp.transpose(qrope, (0, 2, 1, 3)).reshape(bs, MLA_HEADS * ts, QK_ROPE)
    olat = _paged(page_table, qabs, qrope, lat_s.reshape(bs, SROWS, KV_LORA), kr_s.reshape(bs, SROWS, QK_ROPE),
                  cache_mla_latent[l], jnp.swapaxes(cache_mla_krope[l], 1, 2), n_new=ts)
    olat = jnp.transpose(olat.reshape(bs, MLA_HEADS, ts, KV_LORA), (1, 0, 2, 3))
    olat = jnp.pad(olat, ((0, 0), (0, 0), (SLO, SROWS - SLO - ts), (0, 0))).reshape(MLA_HEADS, n_s, KV_LORA)
    omla_s = _bmm(olat, w_uv, F32)
    omla_s = jnp.transpose(omla_s, (1, 0, 2)).reshape(n_s, MLA_HEADS * V_DIM)
    prw_s3 = prw_s.reshape(bs, SROWS, RW_PROJ).at[:, SLO - 1, :].set(state_rwkv_shift[l])
    orw_s, st_s = _rwkv(prw_s3, _state_to_pairs(state_rwkv[l]), *rw_vecs, tb=SROWS, ch=SROWS, row_lo=SLO, row_hi=SLO + ts)
    hist = jnp.transpose(state_ffn_conv[l].reshape(bs, CONV_W - 1, nch2, FFN_CHUNK), (2, 0, 1, 3))
    hist = jnp.pad(hist, ((0, 0), (0, 0), (SLO - (CONV_W - 1), SROWS - SLO), (0, 0))).reshape(nch2, n_s, FFN_CHUNK)
    def mem_rows(c):
        _, nm, hh, dh = c.shape
        return jnp.transpose(c.reshape(bs, nm, hh, dh // LANE, LANE), (0, 1, 3, 2, 4)).reshape(bs, nm * hh * (dh // LANE), LANE)
    mk_s, mv_s = mem_rows(cache_mem_k[l]), mem_rows(cache_mem_v[l])
    y_s, u_s = after_attention(xs3.reshape(n_s, d), orw_s.reshape(n_s, RW_WIDTH), omla_s, mk_s, mv_s, hist,
                               bs, SROWS, tm_s, SROWS, _pick(n_s, (256, 128, 64, 32, 16, 8)), SLO)
    u_s = u_s.reshape(nch2, bs, SROWS, FFN_CHUNK)[:, :, SLO + ts - (CONV_W - 1):SLO + ts]
    conv_s = jnp.transpose(u_s, (1, 2, 0, 3)).reshape(bs, CONV_W - 1, nch2 * FFN_CHUNK)

    real = lambda a, w: a.reshape(bs, SROWS, w)[:, SLO:SLO + ts]
    mem5 = lambda a: a.reshape(1, bp, n_mem, MEM_HEADS, d // MEM_HEADS)
    return (y_p, real(y_s, d),
            lat_p.reshape(1, bp, tp, KV_LORA), kr_p.reshape(1, bp, tp, QK_ROPE), mem5(mk_p), mem5(mv_p),
            st_p[None], prw_p.reshape(bp, tp, RW_PROJ)[:, -1][None], conv_p[None],
            real(lat_s, KV_LORA)[None], real(kr_s, QK_ROPE)[None], _pairs_to_state(st_s)[None],
            prw_s.reshape(bs, SROWS, RW_PROJ)[:, SLO + ts - 1][None], conv_s[None])
```

```python
import functools

import numpy as np
import jax
import jax.numpy as jnp
from jax import lax
from jax.experimental import pallas as pl
from jax.experimental.pallas import tpu as pltpu

F32 = jnp.float32
BF16 = jnp.bfloat16

RW_HEADS = 8
RW_HEAD_DIM = 64
RW_WIDTH = 512
DECAY_LORA = 64
A_LORA = 64
GATE_LORA = 128
RW_PROJ = 3 * RW_WIDTH + DECAY_LORA + A_LORA + GATE_LORA
LNX_EPS = 64e-5
MLA_HEADS = 8
QK_NOPE = 64
QK_ROPE = 32
V_DIM = 64
Q_LORA = 384
KV_LORA = 256
MLA_PROJ = Q_LORA + KV_LORA + QK_ROPE
MLA_SCALE = (QK_NOPE + QK_ROPE) ** -0.5
ROPE_THETA = 10000.0
MEM_HEADS = 4
CONV_W = 3
NORM_EPS = 1e-6
PAGE_SIZE = 128

LANE = 128
SUBLANE = 8
VMEM_LIMIT = 56 * 1024 * 1024
HEAD_PAD = LANE
ROPE_LO = QK_NOPE
SROWS = 8
SLO = 2
NEG = -1e30
LOG2E = float(np.log2(np.e))
HI = lax.Precision.HIGHEST


def _cparams(*sem):
    return pltpu.CompilerParams(dimension_semantics=sem, vmem_limit_bytes=VMEM_LIMIT)


def _rms(x, g, eps=NORM_EPS):
    return x * lax.rsqrt(jnp.mean(x * x, axis=-1, keepdims=True) + eps) * g


def _dot(a, b):
    return jnp.dot(a, b, preferred_element_type=F32)


def _dot_nt(a, b, precision=None):
    return lax.dot_general(a, b, (((1,), (1,)), ((), ())), preferred_element_type=F32, precision=precision)


def _dot_tn(a, b, precision=None):
    return lax.dot_general(a, b, (((0,), (0,)), ((), ())), preferred_element_type=F32, precision=precision)


def _doth(a, b):
    return jnp.dot(a, b, preferred_element_type=F32, precision=HI)


C_Q0, C_KV0, C_KR0, C_KRS0, C_RW0 = 0, Q_LORA, Q_LORA + KV_LORA, Q_LORA + KV_LORA + LANE, Q_LORA + KV_LORA + 2 * LANE
W1_COLS = C_RW0 + RW_PROJ


def _inproj_kernel(x_ref, g_ref, w1_ref, qg_ref, kvg_ref, wq_ref, cq_ref, sq_ref, ck_ref, sk_ref, *rest, with_kv):
    if with_kv:
        wk_ref, wv_ref, q_out, lat_out, kr_out, prw_out, k_out, v_out = rest
    else:
        q_out, lat_out, kr_out, prw_out = rest
    h = _rms(x_ref[...], g_ref[...]).astype(BF16)
    prw_out[...] = _dot(h, w1_ref[:, C_RW0:W1_COLS])
    pm = _dot(h, w1_ref[:, 0:C_RW0])
    cqn = _rms(pm[:, C_Q0:C_KV0], qg_ref[...]).astype(BF16)
    lat = _rms(pm[:, C_KV0:C_KR0], kvg_ref[...])
    lat_out[...] = lat
    krope = pm[:, C_KR0:C_KRS0] * ck_ref[...] + pm[:, C_KRS0:C_RW0] * sk_ref[...]
    kr_out[...] = krope[:, ROPE_LO:ROPE_LO + QK_ROPE]
    q12 = _dot(cqn, wq_ref[...])
    nq = MLA_HEADS * HEAD_PAD
    cq, sq = cq_ref[...], sq_ref[...]
    for hd in range(MLA_HEADS):
        a, b = hd * HEAD_PAD, (hd + 1) * HEAD_PAD
        q_out[:, a:b] = (q12[:, a:b] * cq + q12[:, nq + a:nq + b] * sq).astype(BF16)
    if with_kv:
        latb = lat.astype(BF16)
        kn = _dot(latb, wk_ref[...])
        for hd in range(MLA_HEADS):
            a, b = hd * HEAD_PAD, (hd + 1) * HEAD_PAD
            k_out[:, a:b] = (kn[:, a:b] + krope).astype(BF16)
        vt = _dot_nt(wv_ref[...], latb).astype(BF16)
        for jb in range(v_out.shape[1]):
            v_out[0, jb] = vt[:, jb * v_out.shape[3]:(jb + 1) * v_out.shape[3]]


def _inproj(x, g, w1, qg, kvg, wq, tabs, wk, wv, *, tm, with_kv, seq=None, vblock=None):
    n, d = x.shape
    ttab = tabs[0].shape[0]
    nt = ttab // tm
    row = lambda i: (i, 0)
    const = lambda i: (0, 0)
    tab = lambda i: (i % nt, 0)
    in_specs = [pl.BlockSpec((tm, d), row), pl.BlockSpec(g.shape, const), pl.BlockSpec(w1.shape, const),
                pl.BlockSpec(qg.shape, const), pl.BlockSpec(kvg.shape, const), pl.BlockSpec(wq.shape, const)]
    in_specs += [pl.BlockSpec((tm, LANE), tab)] * 4
    args = [x, g, w1, qg, kvg, wq, *tabs]
    nq = MLA_HEADS * HEAD_PAD
    out_shape = [jax.ShapeDtypeStruct((n, nq), BF16), jax.ShapeDtypeStruct((n, KV_LORA), F32),
                 jax.ShapeDtypeStruct((n, QK_ROPE), F32), jax.ShapeDtypeStruct((n, RW_PROJ), F32)]
    out_specs = [pl.BlockSpec((tm, nq), row), pl.BlockSpec((tm, KV_LORA), row),
                 pl.BlockSpec((tm, QK_ROPE), row), pl.BlockSpec((tm, RW_PROJ), row)]
    if with_kv:
        in_specs += [pl.BlockSpec(wk.shape, const), pl.BlockSpec(wv.shape, const)]
        args += [wk, wv]
        nv = MLA_HEADS * V_DIM
        npb = seq // tm
        out_shape += [jax.ShapeDtypeStruct((n, nq), BF16), jax.ShapeDtypeStruct((n // seq, seq // vblock, nv, vblock), BF16)]
        out_specs += [pl.BlockSpec((tm, nq), row),
                      pl.BlockSpec((1, tm // vblock, nv, vblock), lambda i: (i // npb, i % npb, 0, 0))]
    return pl.pallas_call(
        functools.partial(_inproj_kernel, with_kv=with_kv), grid=(n // tm,), in_specs=in_specs,
        out_specs=out_specs, out_shape=out_shape, compiler_params=_cparams("parallel"),
        name="inproj_kv" if with_kv else "inproj")(*args)


FLASH_HPB = 4
FLASH_T = 512
FLASH_LROWS = 16


def _flash_kernel(q_ref, k_ref, vt_ref, o_ref, st_a, st_b, m_s, acc_s, *, tq):
    tk = tq
    qi = pl.program_id(2)
    assert tq == tk
    nfull = qi
    qs = [q_ref[:, h * HEAD_PAD:(h + 1) * HEAD_PAD] for h in range(FLASH_HPB)]
    kidx = lax.broadcasted_iota(jnp.int32, (tk, tq), 0)
    qidx = lax.broadcasted_iota(jnp.int32, (tk, tq), 1)

    ones = jnp.ones((FLASH_LROWS, tk), BF16)

    def scores(slot, j):
        start = pl.multiple_of(j * tk, tk)
        for h in range(FLASH_HPB):
            slot[h] = _dot_nt(k_ref[pl.ds(start, tk), h * HEAD_PAD:(h + 1) * HEAD_PAD], qs[h])

    def step(slot, j, diagonal):
        for h in range(FLASH_HPB):
            m = m_s[h]
            st = slot[h]
            if diagonal:
                st = jnp.where(kidx <= qidx, st, NEG)
            m_new = jnp.maximum(m, jnp.max(st, axis=0, keepdims=True))
            alpha = jnp.exp2(m - m_new)
            p = jnp.exp2(st - m_new).astype(BF16)
            vt1 = jnp.concatenate([vt_ref[0, j, h * V_DIM:(h + 1) * V_DIM, :], ones], axis=0)
            acc_s[h] = alpha * acc_s[h] + _dot(vt1, p)
            m_s[h] = m_new

    m_s[...] = jnp.full(m_s.shape, NEG, F32)
    acc_s[...] = jnp.zeros(acc_s.shape, F32)

    scores(st_a, 0)

    def pair(jj, _):
        scores(st_b, 2 * jj + 1)
        step(st_a, 2 * jj, False)
        scores(st_a, 2 * jj + 2)
        step(st_b, 2 * jj + 1, False)
        return 0

    npair = nfull // 2
    lax.fori_loop(0, npair, pair, 0)

    @pl.when(nfull % 2 == 1)
    def _():
        scores(st_b, nfull)
        step(st_a, nfull - 1, False)
        step(st_b, nfull, True)

    @pl.when(nfull % 2 == 0)
    def _():
        step(st_a, nfull, True)

    ot = jnp.concatenate([acc_s[h, :V_DIM] / acc_s[h, V_DIM:V_DIM + 1] for h in range(FLASH_HPB)], axis=0)
    o_ref[...] = ot.T


def _flash(q, k, vt, *, batch, seq, tq):
    nqb = seq // tq
    grid = (batch, MLA_HEADS // FLASH_HPB, nqb)
    w = FLASH_HPB * HEAD_PAD
    return pl.pallas_call(
        functools.partial(_flash_kernel, tq=tq), grid=grid,
        in_specs=[pl.BlockSpec((tq, w), lambda b, h, i: (b * nqb + i, h)),
                  pl.BlockSpec((seq, w), lambda b, h, i: (b, h)),
                  pl.BlockSpec((1, seq // tq, FLASH_HPB * V_DIM, tq), lambda b, h, i: (b, 0, h, 0))],
        out_specs=pl.BlockSpec((tq, FLASH_HPB * V_DIM), lambda b, h, i: (b * nqb + i, h)),
        out_shape=jax.ShapeDtypeStruct((batch * seq, MLA_HEADS * V_DIM), F32),
        scratch_shapes=[pltpu.VMEM((FLASH_HPB, tq, tq), F32), pltpu.VMEM((FLASH_HPB, tq, tq), F32),
                        pltpu.VMEM((FLASH_HPB, 1, tq), F32), pltpu.VMEM((FLASH_HPB, V_DIM + FLASH_LROWS, tq), F32)],
        compiler_params=_cparams("parallel", "parallel", "arbitrary"), name="mla_flash")(q, k, vt)


PAGES_PER_STEP = 64
PAGE_GROUP = 4
PAGE_STREAMS = 4


def _paged_kernel(pt_ref, qa_ref, qr_ref, latn_ref, krn_ref, lat_hbm, kr_hbm, o_ref,
                  lat_buf, kr_buf, sem, m_s, l_s, acc_s, *, n_new, g, group, streams):
    seq, step = pl.program_id(0), pl.program_id(1)
    nsteps = pl.num_programs(1)
    lin = seq * nsteps + step
    slot = lin % 2

    def page_copy(page, sl, j):
        return (pltpu.make_async_copy(lat_hbm.at[page], lat_buf.at[sl, j], sem.at[sl, 0]),
                pltpu.make_async_copy(kr_hbm.at[page], kr_buf.at[sl, j], sem.at[sl, 1]))

    def start_step(sq, st, sl):
        for j in range(g):
            for c in page_copy(pt_ref[sq, st * g + j], sl, j):
                c.start(priority=j % 2)

    @pl.when(lin == 0)
    def _():
        start_step(0, 0, 0)

    @pl.when(lin + 1 < pl.num_programs(0) * nsteps)
    def _():
        wrap = step + 1 == nsteps
        start_step(jnp.where(wrap, seq + 1, seq), jnp.where(wrap, 0, step + 1), 1 - slot)

    for j in range(g):
        for c in page_copy(0, slot, j):
            c.wait()

    lat_refs = [lat_buf.at[slot, j] for j in range(g)]
    kr_refs = [kr_buf.at[slot, j] for j in range(g)]

    @pl.when(step == 0)
    def _():
        m_s[...] = jnp.full(m_s.shape, NEG, F32)
        l_s[...] = jnp.zeros(l_s.shape, F32)
        acc_s[...] = jnp.zeros(acc_s.shape, F32)

    qa = qa_ref[0]
    qr = qr_ref[0]

    def update(carry, s, vb):
        m, l, acc = carry
        m_new = jnp.maximum(m, jnp.max(s, axis=-1, keepdims=True))
        alpha = jnp.exp(m - m_new)
        p = jnp.exp(s - m_new)
        l = alpha * l + jnp.sum(p, axis=-1, keepdims=True)
        acc = alpha * acc + _dot(p.astype(BF16), vb)
        return m_new, l, acc

    def scores(grp):
        pages = range(grp * group, (grp + 1) * group)
        latb = jnp.concatenate([lat_refs[j][...].astype(BF16) for j in pages], axis=0)
        krt = jnp.concatenate([kr_refs[j][...].astype(BF16) for j in pages], axis=1)
        return _dot_nt(qa, latb) + _dot(qr, krt), latb

    carries = [(m_s[t], l_s[t], acc_s[t]) for t in range(streams)]
    ngrp = g // group
    ahead = streams
    pend = [scores(i) for i in range(min(ahead, ngrp))]
    for grp in range(ngrp):
        if grp + ahead < ngrp:
            pend.append(scores(grp + ahead))
        t = grp % streams
        carries[t] = update(carries[t], *pend[grp])
        pend[grp] = None
    for t in range(streams):
        m_s[t], l_s[t], acc_s[t] = carries[t]

    @pl.when(step == pl.num_programs(1) - 1)
    def _():
        latn = latn_ref[0].astype(BF16)
        krn = krn_ref[0].astype(BF16)
        s = _dot_nt(qa, latn) + _dot_nt(qr, krn)
        qt = lax.broadcasted_iota(jnp.int32, s.shape, 0) % n_new
        kt = lax.broadcasted_iota(jnp.int32, s.shape, 1) - SLO
        s = jnp.where((kt >= 0) & (kt <= qt), s, NEG)
        m, l, acc = update((m_s[0], l_s[0], acc_s[0]), s, latn)
        for t in range(1, streams):
            mt = m_s[t]
            m_new = jnp.maximum(m, mt)
            a, bt = jnp.exp(m - m_new), jnp.exp(mt - m_new)
            l = a * l + bt * l_s[t]
            acc = a * acc + bt * acc_s[t]
            m = m_new
        o_ref[0] = acc / l


def _paged(page_table, qa, qr, lat_new, kr_new, cache_lat, cache_kr, *, n_new):
    b, npages = page_table.shape
    g = max(d for d in range(1, PAGES_PER_STEP + 1) if npages % d == 0)
    group = max(d for d in range(1, PAGE_GROUP + 1) if g % d == 0)
    streams = min(PAGE_STREAMS, g // group)
    rows = qa.shape[1]
    seq3 = lambda i, s, pt: (i, 0, 0)
    grid_spec = pltpu.PrefetchScalarGridSpec(
        num_scalar_prefetch=1, grid=(b, npages // g),
        in_specs=[pl.BlockSpec((1, rows, KV_LORA), seq3), pl.BlockSpec((1, rows, QK_ROPE), seq3),
                  pl.BlockSpec((1, SROWS, KV_LORA), seq3), pl.BlockSpec((1, SROWS, QK_ROPE), seq3),
                  pl.BlockSpec(memory_space=pl.ANY), pl.BlockSpec(memory_space=pl.ANY)],
        out_specs=pl.BlockSpec((1, rows, KV_LORA), seq3),
        scratch_shapes=[pltpu.VMEM((2, g, PAGE_SIZE, KV_LORA), F32), pltpu.VMEM((2, g, QK_ROPE, PAGE_SIZE), F32),
                        pltpu.SemaphoreType.DMA((2, 2)),
                        pltpu.VMEM((streams, rows, 1), F32), pltpu.VMEM((streams, rows, 1), F32),
                        pltpu.VMEM((streams, rows, KV_LORA), F32)])
    return pl.pallas_call(
        functools.partial(_paged_kernel, n_new=n_new, g=g, group=group, streams=streams), grid_spec=grid_spec,
        out_shape=jax.ShapeDtypeStruct((b, rows, KV_LORA), F32),
        compiler_params=_cparams("arbitrary", "arbitrary"), name="mla_paged")(
            page_table, qa, qr, lat_new, kr_new, cache_lat, cache_kr)


def _bmm_kernel(a_ref, w_ref, o_ref):
    o_ref[0] = _dot(a_ref[...].astype(BF16), w_ref[0]).astype(o_ref.dtype)


def _bmm_cols(a, w, out_dtype):
    n = a.shape[0]
    hh, k, m = w.shape
    return pl.pallas_call(
        _bmm_kernel, grid=(hh,),
        in_specs=[pl.BlockSpec((n, k), lambda h: (0, h)), pl.BlockSpec((1, k, m), lambda h: (h, 0, 0))],
        out_specs=pl.BlockSpec((1, n, m), lambda h: (h, 0, 0)),
        out_shape=jax.ShapeDtypeStruct((hh, n, m), out_dtype), compiler_params=_cparams("parallel"),
        name="bmm_cols")(a, w)


def _bmm_kernel3(a_ref, w_ref, o_ref):
    o_ref[0] = _dot(a_ref[0].astype(BF16), w_ref[0]).astype(o_ref.dtype)


def _bmm(a, w, out_dtype):
    hh, n, k = a.shape
    m = w.shape[2]
    return pl.pallas_call(
        _bmm_kernel3, grid=(hh,),
        in_specs=[pl.BlockSpec((1, n, k), lambda h: (h, 0, 0)), pl.BlockSpec((1, k, m), lambda h: (h, 0, 0))],
        out_specs=pl.BlockSpec((1, n, m), lambda h: (h, 0, 0)),
        out_shape=jax.ShapeDtypeStruct((hh, n, m), out_dtype), compiler_params=_cparams("parallel"),
        name="bmm")(a, w)


PAIRS = RW_HEADS // 2
C_R, C_K, C_V, C_WA, C_G = 0, RW_WIDTH, 2 * RW_WIDTH, 3 * RW_WIDTH, 3 * RW_WIDTH + DECAY_LORA + A_LORA


def _seg_sum(x):
    lane = lax.broadcasted_iota(jnp.int32, (x.shape[0], LANE), 1)
    low = lane < RW_HEAD_DIM
    outs = []
    for t in range(x.shape[1] // LANE):
        xt = x[:, t * LANE:(t + 1) * LANE]
        s0 = jnp.sum(jnp.where(low, xt, 0.0), axis=-1, keepdims=True)
        s1 = jnp.sum(jnp.where(low, 0.0, xt), axis=-1, keepdims=True)
        outs.append(jnp.where(low, s0, s1))
    return outs[0] if len(outs) == 1 else jnp.concatenate(outs, axis=-1)


def _rwkv_prologue(p_ref, s0_ref, so_ref, vec_refs, scratch, *, tb, row_lo, row_hi):
    mu_ref, w0_ref, wa2_ref, a0_ref, g2_ref, kk_ref, ka_ref, rk_ref = vec_refs
    carry_s, r_s, k_s, v_s, kk_s, b_s, ld_s = scratch
    i = pl.program_id(1)

    @pl.when(i == 0)
    def _():
        carry_s[...] = jnp.zeros(carry_s.shape, F32)
        so_ref[0] = s0_ref[0]

    p = p_ref[0]
    rows1 = lax.broadcasted_iota(jnp.int32, (tb, 1), 0)
    prev = jnp.where(rows1 == 0, carry_s[SUBLANE - 1:SUBLANE, :], pltpu.roll(p, 1, axis=0))
    carry_s[...] = p[tb - SUBLANE:tb, :]
    s = p + (prev - p) * mu_ref[...]
    r = s[:, C_R:C_K]
    k = s[:, C_K:C_V]
    v = s[:, C_V:C_WA]
    wa = s[:, C_WA:C_G]
    lane = lax.broadcasted_iota(jnp.int32, wa.shape, 1)
    z = jnp.where(lane < DECAY_LORA, jnp.tanh(wa), wa).astype(BF16)
    lin = _dot(z, wa2_ref[...])
    xw = -(w0_ref[...] + lin[:, :RW_WIDTH])
    w = -(jnp.maximum(xw, 0.0) + jnp.log(1.0 + jnp.exp(-jnp.abs(xw)))) - 0.5
    logd = -jnp.exp(w)
    a = jax.nn.sigmoid(a0_ref[...] + lin[:, RW_WIDTH:])
    gate = _dot(jax.nn.sigmoid(s[:, C_G:RW_PROJ]).astype(BF16), g2_ref[...])
    kk = k * kk_ref[...]
    kk = kk / jnp.maximum(jnp.sqrt(_seg_sum(kk * kk)), 1e-12)
    k = k * (1.0 + (a - 1.0) * ka_ref[...])
    bonus = _seg_sum(r * k * rk_ref[...]) * v
    if row_lo > 0 or row_hi < tb:
        live = (rows1 >= row_lo) & (rows1 < row_hi)
        logd = jnp.where(live, logd, 0.0)
        kk = jnp.where(live, kk, 0.0)
        k = jnp.where(live, k, 0.0)
        v = jnp.where(live, v, 0.0)
    r_s[...] = r
    k_s[...] = k
    v_s[...] = v
    kk_s[...] = kk
    b_s[...] = kk * a
    ld_s[...] = logd
    return gate, bonus


def _rwkv_epilogue(out, gate, bonus, lg_ref, lb_ref, o_ref):
    mean = _seg_sum(out) * (1.0 / RW_HEAD_DIM)
    cen = out - mean
    var = _seg_sum(cen * cen) * (1.0 / RW_HEAD_DIM)
    y = cen * lax.rsqrt(var + LNX_EPS) * lg_ref[...] + lb_ref[...]
    o_ref[0] = ((y + bonus) * gate).astype(o_ref.dtype)


def _rwkv_kernel(p_ref, s0_ref, mu_ref, w0_ref, wa2_ref, a0_ref, g2_ref, kk_ref, ka_ref, rk_ref, lg_ref, lb_ref,
                 o_ref, so_ref, carry_s, r_s, k_s, v_s, kk_s, b_s, ld_s, out_s, *, tb, ch, row_lo, row_hi):
    gate, bonus = _rwkv_prologue(p_ref, s0_ref, so_ref, (mu_ref, w0_ref, wa2_ref, a0_ref, g2_ref, kk_ref, ka_ref, rk_ref),
                                 (carry_s, r_s, k_s, v_s, kk_s, b_s, ld_s), tb=tb, row_lo=row_lo, row_hi=row_hi)
    c2 = 2 * ch
    ri = lax.broadcasted_iota(jnp.int32, (c2, c2), 0)
    ci = lax.broadcasted_iota(jnp.int32, (c2, c2), 1)
    strict = (ci % ch) < (ri % ch)
    incl = (ci % ch) <= (ri % ch)
    eye = (ri == ci).astype(F32)
    tri = (lax.broadcasted_iota(jnp.int32, (ch, ch), 1) <= lax.broadcasted_iota(jnp.int32, (ch, ch), 0)).astype(F32)
    lane2 = lax.broadcasted_iota(jnp.int32, (c2, LANE), 1)
    row2 = lax.broadcasted_iota(jnp.int32, (c2, LANE), 0)
    own = (lane2 < RW_HEAD_DIM) == (row2 < ch)

    def stack(x):
        return jnp.where(own, jnp.concatenate([x, x], axis=0), 0.0)

    bf = lambda x: x.astype(BF16)
    pairs = [slice(pr * LANE, (pr + 1) * LANE) for pr in range(PAIRS)]

    def chunk(c, _):
        ts = pl.ds(pl.multiple_of(c * ch, ch), ch)
        css = [_doth(tri, ld_s[ts, ls]) for ls in pairs]
        lhss, amats, tails = [], [], []
        for ls, cs in zip(pairs, css):
            ld = ld_s[ts, ls]
            e_inv = jnp.exp(-cs)
            lhs = bf(jnp.concatenate([stack(kk_s[ts, ls] * jnp.exp(cs - ld)), stack(r_s[ts, ls] * jnp.exp(cs))], axis=0))
            rhs = bf(jnp.concatenate([stack(k_s[ts, ls] * e_inv), stack(b_s[ts, ls] * e_inv)], axis=0))
            lhss.append(lhs)
            amats.append(_dot_nt(lhs, rhs))
        sts = [so_ref[0, pr] for pr in range(PAIRS)]
        gmats = [_dot_nt(lhs, bf(st)) for lhs, st in zip(lhss, sts)]
        vss = [bf(stack(v_s[ts, ls])) for ls in pairs]
        akbs = [jnp.where(strict, amat[:c2, c2:], 0.0) for amat in amats]
        ykk = [_dot(bf(jnp.where(strict, amat[:c2, :c2], 0.0)), vs) for amat, vs in zip(amats, vss)]
        yrk = [_dot(bf(jnp.where(incl, amat[c2:, :c2], 0.0)), vs) for amat, vs in zip(amats, vss)]
        invs = [eye - jnp.where((ri // 2 == ci // 2), a_kb, 0.0) for a_kb in akbs]
        m = 2
        while m < ch:
            lvl = (ri // (2 * m) == ci // (2 * m)) & ((ri // m) % 2 == 1) & ((ci // m) % 2 == 0)
            tmps = [_dot(bf(inv), bf(jnp.where(lvl, a_kb, 0.0))) for inv, a_kb in zip(invs, akbs)]
            invs = [inv - _dot(bf(tmp), bf(inv)) for inv, tmp in zip(invs, tmps)]
            m *= 2
        us = [_dot(bf(inv), bf(gmat[:c2] + y)) for inv, gmat, y in zip(invs, gmats, ykk)]
        o2s = [gmat[c2:] + y - _dot(bf(jnp.where(incl, amat[c2:, c2:], 0.0)), bf(u))
               for gmat, y, amat, u in zip(gmats, yrk, amats, us)]
        for pr, (ls, cs, st, vs, u, o2) in enumerate(zip(pairs, css, sts, vss, us, o2s)):
            tot = cs[ch - 1:ch, :]
            e_tail = jnp.exp(tot - cs)
            out_s[ts, ls] = o2[:ch] + o2[ch:]
            so_ref[0, pr] = (st * jnp.exp(tot) + _dot_tn(vs, bf(stack(k_s[ts, ls] * e_tail)))
                             - _dot_tn(bf(u), bf(stack(b_s[ts, ls] * e_tail))))
        return 0

    lax.fori_loop(0, tb // ch, chunk, 0)
    _rwkv_epilogue(out_s[...], gate, bonus, lg_ref, lb_ref, o_ref)


def _rwkv(p, s0, mu, w0, wa2, a0, g2, k_k, k_a, r_k, lnx_g, lnx_b, *, tb, ch, row_lo, row_hi):
    b, t, _ = p.shape
    const = lambda bi, i: (0, 0)
    vec = lambda a: pl.BlockSpec(a.shape, const)
    scr = lambda: pltpu.VMEM((tb, RW_WIDTH), F32)
    return pl.pallas_call(
        functools.partial(_rwkv_kernel, tb=tb, ch=ch, row_lo=row_lo, row_hi=row_hi), grid=(b, t // tb),
        in_specs=[pl.BlockSpec((1, tb, RW_PROJ), lambda bi, i: (bi, i, 0)),
                  pl.BlockSpec((1, PAIRS, LANE, LANE), lambda bi, i: (bi, 0, 0, 0)),
                  vec(mu), vec(w0), vec(wa2), vec(a0), vec(g2), vec(k_k), vec(k_a), vec(r_k), vec(lnx_g), vec(lnx_b)],
        out_specs=[pl.BlockSpec((1, tb, RW_WIDTH), lambda bi, i: (bi, i, 0)),
                   pl.BlockSpec((1, PAIRS, LANE, LANE), lambda bi, i: (bi, 0, 0, 0))],
        out_shape=[jax.ShapeDtypeStruct((b, t, RW_WIDTH), BF16), jax.ShapeDtypeStruct((b, PAIRS, LANE, LANE), F32)],
        scratch_shapes=[pltpu.VMEM((SUBLANE, RW_PROJ), F32)] + [scr() for _ in range(7)],
        compiler_params=_cparams("parallel", "arbitrary"), name="rwkv")(
            p, s0, mu, w0, wa2, a0, g2, k_k, k_a, r_k, lnx_g, lnx_b)


QUAD = 4
QCH = RW_HEAD_DIM
QW = QUAD * RW_HEAD_DIM
QLEVELS = (2, 4, 8, 16, 32)
PREP_CHUNKS = 4


def _tile4(x):
    return jnp.concatenate([x] * QUAD, axis=0)


def _rwkv_quad_kernel(p_ref, s0_ref, mu_ref, w0_ref, wa2_ref, a0_ref, g2_ref, kk_ref, ka_ref, rk_ref, lg_ref, lb_ref,
                      tri_ref, bdm_ref, own_ref, sbm_ref,
                      o_ref, so_ref, carry_s, r_s, k_s, v_s, kk_s, b_s, ld_s, out_s,
                      lhs_c, arb_c, y_c, kt_c, dec_c, *, tb):
    gate, bonus = _rwkv_prologue(p_ref, s0_ref, so_ref, (mu_ref, w0_ref, wa2_ref, a0_ref, g2_ref, kk_ref, ka_ref, rk_ref),
                                 (carry_s, r_s, k_s, v_s, kk_s, b_s, ld_s), tb=tb, row_lo=0, row_hi=tb)
    ch = QCH
    nquad = RW_HEADS // QUAD
    bf = lambda x: x.astype(BF16)

    def bd(x):
        return _tile4(bf(x)) * bdm_ref[0]

    quads = [slice(q * QW, (q + 1) * QW) for q in range(nquad)]

    def prepare(chunks):
        inst = [(c, q, pl.ds(pl.multiple_of(c * ch, ch), ch), quads[q]) for c in chunks for q in range(nquad)]
        strict, incl, eye, lvl1 = sbm_ref[0] > 0, sbm_ref[1] > 0, sbm_ref[2], sbm_ref[3]
        css = []
        for c, q, ts, ls in inst:
            ld = ld_s[ts, ls]
            l1 = bf(ld)
            r1 = ld - l1.astype(F32)
            l2 = bf(r1)
            l3 = bf(r1 - l2.astype(F32))
            css.append(_dot(tri_ref[...], jnp.concatenate([l1, l2, l3], axis=0)))
        a1s, a2s = [], []
        for (c, q, ts, ls), cs in zip(inst, css):
            ld = ld_s[ts, ls]
            tot = cs[ch - 1:ch, :]
            e_inv = jnp.exp(-cs)
            kc, bc = k_s[ts, ls], b_s[ts, ls]
            lhs = bf(jnp.concatenate([kk_s[ts, ls] * jnp.exp(cs - ld), r_s[ts, ls] * jnp.exp(cs)], axis=0))
            lhs_c[c, q] = lhs
            e_tail = jnp.exp(tot - cs)
            kt_c[c, q] = bf(jnp.concatenate([kc * e_tail, bc * e_tail], axis=0))
            dec_c[c, q] = jnp.broadcast_to(jnp.exp(tot), (SUBLANE, QW))
            a1s.append(_dot_nt(lhs, bd(kc * e_inv)))
            a2s.append(_dot_nt(lhs, bd(bc * e_inv)))
        ys, akb4s, invs = [], [], []
        for (c, q, ts, ls), a1, a2 in zip(inst, a1s, a2s):
            a_kk = jnp.where(strict, a1[:ch], 0.0)
            a_rk = jnp.where(incl, a1[ch:], 0.0)
            a_kb = jnp.where(strict, a2[:ch], 0.0)
            arb_c[c, q] = bf(jnp.where(incl, a2[ch:], 0.0))
            ys.append(_dot(bf(jnp.concatenate([a_kk, a_rk], axis=0)), bd(v_s[ts, ls])))
            akb4s.append(_tile4(bf(a_kb)))
            invs.append(eye - a_kb * lvl1)
        for li in range(len(QLEVELS)):
            tmps = [_dot(bf(inv), akb4 * bdm_ref[1 + li]) for inv, akb4 in zip(invs, akb4s)]
            invs = [inv - _dot(bf(tmp), bd(inv)) for inv, tmp in zip(invs, tmps)]
        ws = [_dot(bf(inv), _tile4(lhs_c[c, q][:ch]) * bdm_ref[0]) for (c, q, ts, ls), inv in zip(inst, invs)]
        yks = [_dot(bf(inv), bd(y[:ch])) for inv, y in zip(invs, ys)]
        for (c, q, ts, ls), w, yk, y in zip(inst, ws, yks, ys):
            lhs_c[c, q, 0:ch, :] = bf(w)
            y_c[c, q] = jnp.concatenate([yk, y[ch:]], axis=0)

    nc = tb // ch
    if nc % PREP_CHUNKS == 0:
        def prepare_loop(cc, _):
            prepare([PREP_CHUNKS * cc + d for d in range(PREP_CHUNKS)])
            return 0
        lax.fori_loop(0, nc // PREP_CHUNKS, prepare_loop, 0)
    else:
        lax.fori_loop(0, nc, lambda c, _: (prepare([c]), 0)[1], 0)

    def recur(c, _):
        ts = pl.ds(pl.multiple_of(c * ch, ch), ch)
        sts = [so_ref[0, q] for q in range(nquad)]
        gys = [_dot_nt(lhs_c[c, q], bf(sts[q])) + y_c[c, q] for q in range(nquad)]
        us = [gys[q][:ch] for q in range(nquad)]
        outs = [gys[q][ch:] - _dot(arb_c[c, q], bd(us[q])) for q in range(nquad)]
        upds = [_dot_tn(bf(jnp.concatenate([v_s[ts, quads[q]], -us[q]], axis=0)), kt_c[c, q]) for q in range(nquad)]
        for q in range(nquad):
            out_s[ts, quads[q]] = outs[q]
            so_ref[0, q] = sts[q] * dec_c[c, q][0:1, :] + upds[q] * own_ref[...]
        return 0

    lax.fori_loop(0, nc, recur, 0)
    _rwkv_epilogue(out_s[...], gate, bonus, lg_ref, lb_ref, o_ref)


def _quad_masks():
    ch = QCH
    r = np.arange(QUAD * ch)[:, None]
    c = np.arange(QW)[None, :]
    own = (r // ch) == (c // RW_HEAD_DIM)
    sp, s = r % ch, c % ch
    bdm = [own]
    for m in QLEVELS:
        bdm.append(own & (sp // (2 * m) == s // (2 * m)) & ((sp // m) % 2 == 1) & ((s // m) % 2 == 0))
    t = np.arange(ch)[:, None]
    sbm = [s < t, s <= t, s == t, (t // 2 == s // 2) & (t % 2 == 1) & (s % 2 == 0)]
    tri = np.tile(np.arange(ch)[None, :] <= np.arange(ch)[:, None], (1, 3))
    return (jnp.asarray(tri, BF16), jnp.asarray(np.stack(bdm), BF16), jnp.asarray(own, F32),
            jnp.asarray(np.stack([np.broadcast_to(m, (ch, QW)) for m in sbm]), F32))


def _rwkv_quad(p, s0, mu, w0, wa2, a0, g2, k_k, k_a, r_k, lnx_g, lnx_b, *, tb):
    b, t, _ = p.shape
    nq = RW_HEADS // QUAD
    nc = tb // QCH
    masks = _quad_masks()
    vec = lambda a: pl.BlockSpec(a.shape, lambda bi, i: (0,) * a.ndim)
    scr = lambda: pltpu.VMEM((tb, RW_WIDTH), F32)
    vecs = (mu, w0, wa2, a0, g2, k_k, k_a, r_k, lnx_g, lnx_b) + masks
    return pl.pallas_call(
        functools.partial(_rwkv_quad_kernel, tb=tb), grid=(b, t // tb),
        in_specs=[pl.BlockSpec((1, tb, RW_PROJ), lambda bi, i: (bi, i, 0)),
                  pl.BlockSpec((1, nq, QW, QW), lambda bi, i: (bi, 0, 0, 0))] + [vec(a) for a in vecs],
        out_specs=[pl.BlockSpec((1, tb, RW_WIDTH), lambda bi, i: (bi, i, 0)),
                   pl.BlockSpec((1, nq, QW, QW), lambda bi, i: (bi, 0, 0, 0))],
        out_shape=[jax.ShapeDtypeStruct((b, t, RW_WIDTH), BF16), jax.ShapeDtypeStruct((b, nq, QW, QW), F32)],
        scratch_shapes=[pltpu.VMEM((SUBLANE, RW_PROJ), F32)] + [scr() for _ in range(7)] + [
            pltpu.VMEM((nc, nq, 2 * QCH, QW), BF16),
            pltpu.VMEM((nc, nq, QCH, QW), BF16), pltpu.VMEM((nc, nq, 2 * QCH, QW), F32),
            pltpu.VMEM((nc, nq, 2 * QCH, QW), BF16), pltpu.VMEM((nc, nq, SUBLANE, QW), F32)],
        compiler_params=_cparams("parallel", "arbitrary"), name="rwkv_quad")(p, s0, *vecs)


def _quads_to_state(sq):
    b = sq.shape[0]
    s6 = sq.reshape(b, RW_HEADS // QUAD, QUAD, RW_HEAD_DIM, QUAD, RW_HEAD_DIM)
    return jnp.stack([s6[:, :, h, :, h, :] for h in range(QUAD)], axis=2).reshape(b, RW_HEADS, RW_HEAD_DIM, RW_HEAD_DIM)


def _mix_kernel(x_ref, orw_ref, omla_ref, g_ref, wa_ref, wb_ref, o_ref):
    om = _rms(omla_ref[...], g_ref[...]).astype(BF16)
    o_ref[...] = x_ref[...] + _dot(orw_ref[...], wa_ref[...]) + _dot(om, wb_ref[...])


def _mix(x, orw, omla, g, wa, wb, *, tm):
    n, d = x.shape
    row = lambda i: (i, 0)
    const = lambda i: (0, 0)
    return pl.pallas_call(
        _mix_kernel, grid=(n // tm,),
        in_specs=[pl.BlockSpec((tm, d), row), pl.BlockSpec((tm, orw.shape[1]), row), pl.BlockSpec((tm, omla.shape[1]), row),
                  pl.BlockSpec(g.shape, const), pl.BlockSpec(wa.shape, const), pl.BlockSpec(wb.shape, const)],
        out_specs=pl.BlockSpec((tm, d), row), out_shape=jax.ShapeDtypeStruct((n, d), F32),
        compiler_params=_cparams("parallel"), name="mix_out")(x, orw, omla, g, wa, wb)


def _norm_mm_kernel(x_ref, g_ref, w_ref, o_ref):
    o_ref[...] = _dot(_rms(x_ref[...], g_ref[...]).astype(BF16), w_ref[...]).astype(o_ref.dtype)


def _norm_mm(x, g, w, out_dtype, *, tm):
    n, d = x.shape
    m = w.shape[1]
    return pl.pallas_call(
        _norm_mm_kernel, grid=(n // tm,),
        in_specs=[pl.BlockSpec((tm, d), lambda i: (i, 0)), pl.BlockSpec(g.shape, lambda i: (0, 0)),
                  pl.BlockSpec(w.shape, lambda i: (0, 0))],
        out_specs=pl.BlockSpec((tm, m), lambda i: (i, 0)), out_shape=jax.ShapeDtypeStruct((n, m), out_dtype),
        compiler_params=_cparams("parallel"), name="norm_mm")(x, g, w)


def _mm_res_kernel(a_ref, w_ref, x_ref, o_ref):
    o_ref[...] = x_ref[...] + _dot(a_ref[...], w_ref[...])


def _mm_res(a, w, x, *, tm):
    n, d = x.shape
    return pl.pallas_call(
        _mm_res_kernel, grid=(n // tm,),
        in_specs=[pl.BlockSpec((tm, a.shape[1]), lambda i: (i, 0)), pl.BlockSpec(w.shape, lambda i: (0, 0)),
                  pl.BlockSpec((tm, d), lambda i: (i, 0))],
        out_specs=pl.BlockSpec((tm, d), lambda i: (i, 0)), out_shape=jax.ShapeDtypeStruct((n, d), F32),
        compiler_params=_cparams("parallel"), name="mm_res")(a, w, x)


def _memattn_kernel(q_ref, k_ref, v_ref, o_ref):
    dh = q_ref.shape[2] // MEM_HEADS
    scale = dh ** -0.5
    for h in range(MEM_HEADS):
        cs = slice(h * dh, (h + 1) * dh)
        s = _dot_nt(q_ref[0, :, cs], k_ref[0, :, cs].astype(BF16)) * scale
        p = jnp.exp(s - jnp.max(s, axis=-1, keepdims=True))
        l = jnp.sum(p, axis=-1, keepdims=True)
        o_ref[0, :, cs] = (_dot(p.astype(BF16), v_ref[0, :, cs].astype(BF16)) / l).astype(o_ref.dtype)


def _memattn_tiled_kernel(q_ref, k_ref, v_ref, o_ref, *, nm):
    dh = q_ref.shape[2] // MEM_HEADS
    nj = dh // LANE
    scale = dh ** -0.5
    for h in range(MEM_HEADS):
        rows = [pl.ds(j * MEM_HEADS + h, nm, stride=nj * MEM_HEADS) for j in range(nj)]
        s = sum(_dot_nt(q_ref[0, :, h * dh + j * LANE:h * dh + (j + 1) * LANE], k_ref[0, rows[j], :].astype(BF16))
                for j in range(nj)) * scale
        p = jnp.exp(s - jnp.max(s, axis=-1, keepdims=True))
        l = jnp.sum(p, axis=-1, keepdims=True)
        pb = p.astype(BF16)
        for j in range(nj):
            o_ref[0, :, h * dh + j * LANE:h * dh + (j + 1) * LANE] = (
                _dot(pb, v_ref[0, rows[j], :].astype(BF16)) / l).astype(o_ref.dtype)


def _memattn_tiled(q, mk, mv, *, nm):
    b, t, d = q.shape
    mem_spec = pl.BlockSpec((1,) + mk.shape[1:], lambda bi: (bi, 0, 0))
    return pl.pallas_call(
        functools.partial(_memattn_tiled_kernel, nm=nm), grid=(b,),
        in_specs=[pl.BlockSpec((1, t, d), lambda bi: (bi, 0, 0)), mem_spec, mem_spec],
        out_specs=pl.BlockSpec((1, t, d), lambda bi: (bi, 0, 0)), out_shape=jax.ShapeDtypeStruct((b, t, d), BF16),
        compiler_params=_cparams("parallel"), name="mem_attn_tiled")(q, mk, mv)


def _memattn(q, mk, mv, *, tm):
    b, t, d = q.shape
    mem_spec = pl.BlockSpec((1,) + mk.shape[1:], lambda bi, i: (bi,) + (0,) * (mk.ndim - 1))
    return pl.pallas_call(
        _memattn_kernel, grid=(b, t // tm),
        in_specs=[pl.BlockSpec((1, tm, d), lambda bi, i: (bi, i, 0)), mem_spec, mem_spec],
        out_specs=pl.BlockSpec((1, tm, d), lambda bi, i: (bi, i, 0)), out_shape=jax.ShapeDtypeStruct((b, t, d), BF16),
        compiler_params=_cparams("parallel", "parallel"), name="mem_attn")(q, mk, mv)


def _memblock_kernel(x_ref, orw_ref, omla_ref, gmo_ref, wa_ref, wb_ref, gq_ref, wq_ref, k_ref, v_ref, wo_ref, o_ref):
    om = _rms(omla_ref[0], gmo_ref[...]).astype(BF16)
    x1 = x_ref[0] + _dot(orw_ref[0], wa_ref[...]) + _dot(om, wb_ref[...])
    q = _dot(_rms(x1, gq_ref[...]).astype(BF16), wq_ref[...]).astype(BF16)
    dh = q.shape[1] // MEM_HEADS
    scale = dh ** -0.5
    outs = []
    for h in range(MEM_HEADS):
        cs = slice(h * dh, (h + 1) * dh)
        s = _dot_nt(q[:, cs], k_ref[0, :, cs].astype(BF16)) * scale
        p = jnp.exp(s - jnp.max(s, axis=-1, keepdims=True))
        l = jnp.sum(p, axis=-1, keepdims=True)
        outs.append((_dot(p.astype(BF16), v_ref[0, :, cs].astype(BF16)) / l).astype(BF16))
    o_ref[0] = x1 + _dot(jnp.concatenate(outs, axis=1), wo_ref[...])


def _memblock(x, orw, omla, gmo, wa, wb, gq, wq, mk, mv, wo, *, tm):
    b, t, d = x.shape
    row = lambda bi, i: (bi, i, 0)
    seq = lambda bi, i: (bi, 0, 0)
    const = lambda bi, i: (0, 0)
    cs = lambda a: pl.BlockSpec(a.shape, const)
    return pl.pallas_call(
        _memblock_kernel, grid=(b, t // tm),
        in_specs=[pl.BlockSpec((1, tm, d), row), pl.BlockSpec((1, tm, orw.shape[2]), row), pl.BlockSpec((1, tm, omla.shape[2]), row),
                  cs(gmo), cs(wa), cs(wb), cs(gq), cs(wq), pl.BlockSpec((1,) + mk.shape[1:], seq),
                  pl.BlockSpec((1,) + mv.shape[1:], seq), cs(wo)],
        out_specs=pl.BlockSpec((1, tm, d), row), out_shape=jax.ShapeDtypeStruct((b, t, d), F32),
        compiler_params=_cparams("parallel", "parallel"), name="mem_block")(x, orw, omla, gmo, wa, wb, gq, wq, mk, mv, wo)


FFN_CHUNK = 256
FFN_GROUP = 2


def _ffn_kernel(x_ref, prev_ref, g_ref, wup_ref, cw_ref, cb_ref, wdn_ref, gf_ref, y_ref, u_ref, act_s, ua_s, ub_s,
                *, tm, nch, prev_rows):
    i = pl.program_id(1)
    x = x_ref[0]
    h = _rms(x, g_ref[...]).astype(BF16)
    rows = lax.broadcasted_iota(jnp.int32, (tm, 1), 0)
    if prev_rows is None:
        @pl.when(i == 0)
        def _():
            u_ref[0] = prev_ref[0]
    else:
        hist = (rows % SROWS) < prev_rows

    def up(c):
        return _dot(h, wup_ref[c]), _dot(h, wup_ref[c + nch])

    def conv(c, u):
        if prev_rows is None:
            tail = u_ref[0, c]
            u_ref[0, c] = u[tm - SUBLANE:tm, :]
            p1, p2 = tail[SUBLANE - 1:SUBLANE, :], tail[SUBLANE - 2:SUBLANE - 1, :]
            u1, u2 = pltpu.roll(u, 1, axis=0), pltpu.roll(u, 2, axis=0)
            r8 = rows[:SUBLANE]
            u1 = jnp.concatenate([jnp.where(r8 == 0, p1, u1[:SUBLANE]), u1[SUBLANE:]], axis=0)
            u2 = jnp.concatenate([jnp.where(r8 == 0, p2, jnp.where(r8 == 1, p1, u2[:SUBLANE])), u2[SUBLANE:]], axis=0)
        else:
            u = jnp.where(hist, prev_ref[c], u)
            u_ref[c] = u
            u1 = pltpu.roll(u, 1, axis=0)
            u2 = pltpu.roll(u, 2, axis=0)
        cw = cw_ref[c]
        return cb_ref[c] + cw[0:1, :] * u2 + cw[1:2, :] * u1 + cw[2:3, :] * u

    def gated(c, ug, uv):
        gate = conv(c, ug)
        val = conv(c + nch, uv)
        act_s[c] = (gate * jax.nn.sigmoid(gate) * val).astype(BF16)

    groups = [list(range(s, min(s + FFN_GROUP, nch))) for s in range(0, nch, FFN_GROUP)]
    bufs = (ua_s, ub_s)

    def issue(gi):
        for d, c in enumerate(groups[gi]):
            bufs[gi % 2][2 * d], bufs[gi % 2][2 * d + 1] = up(c)

    def consume(gi):
        for d, c in enumerate(groups[gi]):
            gated(c, bufs[gi % 2][2 * d], bufs[gi % 2][2 * d + 1])

    issue(0)
    for gi in range(len(groups)):
        if gi + 1 < len(groups):
            issue(gi + 1)
        consume(gi)
    f = _dot(jnp.concatenate([act_s[c] for c in range(nch)], axis=1), wdn_ref[...])
    y_ref[0] = _rms(x + f, gf_ref[...])


def _ffn(x, prev, g, wup, cw, cb, wdn, gf, *, tm, prev_rows):
    b, t, d = x.shape
    nch = wup.shape[0] // 2
    c3 = lambda bi, i: (0, 0, 0)
    if prev_rows is None:
        prev_spec = pl.BlockSpec((1, 2 * nch, SUBLANE, FFN_CHUNK), lambda bi, i: (bi, 0, 0, 0))
        u_spec = pl.BlockSpec((1, 2 * nch, SUBLANE, FFN_CHUNK), lambda bi, i: (bi, 0, 0, 0))
        u_shape = jax.ShapeDtypeStruct((b, 2 * nch, SUBLANE, FFN_CHUNK), F32)
    else:
        prev_spec = pl.BlockSpec((2 * nch, tm, FFN_CHUNK), lambda bi, i: (0, bi * (t // tm) + i, 0))
        u_spec = pl.BlockSpec((2 * nch, tm, FFN_CHUNK), lambda bi, i: (0, bi * (t // tm) + i, 0))
        u_shape = jax.ShapeDtypeStruct((2 * nch, b * t, FFN_CHUNK), F32)
    return pl.pallas_call(
        functools.partial(_ffn_kernel, tm=tm, nch=nch, prev_rows=prev_rows), grid=(b, t // tm),
        in_specs=[pl.BlockSpec((1, tm, d), lambda bi, i: (bi, i, 0)), prev_spec, pl.BlockSpec(g.shape, lambda bi, i: (0, 0)),
                  pl.BlockSpec(wup.shape, c3), pl.BlockSpec(cw.shape, c3), pl.BlockSpec(cb.shape, c3),
                  pl.BlockSpec(wdn.shape, lambda bi, i: (0, 0)), pl.BlockSpec(gf.shape, lambda bi, i: (0, 0))],
        out_specs=[pl.BlockSpec((1, tm, d), lambda bi, i: (bi, i, 0)), u_spec],
        out_shape=[jax.ShapeDtypeStruct((b, t, d), F32), u_shape],
        scratch_shapes=[pltpu.VMEM((nch, tm, FFN_CHUNK), BF16), pltpu.VMEM((2 * FFN_GROUP, tm, FFN_CHUNK), F32),
                        pltpu.VMEM((2 * FFN_GROUP, tm, FFN_CHUNK), F32)],
        compiler_params=_cparams("parallel", "arbitrary"), name="conv_ffn")(x, prev, g, wup, cw, cb, wdn, gf)


def _rope_tables(pos, scale):
    half = QK_ROPE // 2
    inv = 1.0 / (ROPE_THETA ** (np.arange(half, dtype=np.float64) / half))
    ang = np.asarray(pos, np.float64)[:, None] * inv[None, :]
    cos, sin = np.cos(ang), np.sin(ang)
    n = len(pos)
    c = np.zeros((n, HEAD_PAD))
    s = np.zeros((n, HEAD_PAD))
    c[:, :QK_NOPE] = 1.0
    c[:, ROPE_LO:ROPE_LO + half] = cos
    c[:, ROPE_LO + half:ROPE_LO + QK_ROPE] = cos
    s[:, ROPE_LO:ROPE_LO + half] = -sin
    s[:, ROPE_LO + half:ROPE_LO + QK_ROPE] = sin
    return (jnp.asarray(c * scale, F32), jnp.asarray(s * scale, F32), jnp.asarray(c, F32), jnp.asarray(s, F32))


def _swap_halves(w):
    half = w.shape[-1] // 2
    return jnp.concatenate([w[..., half:], w[..., :half]], axis=-1)


def _prep_weights(w_in, w_uq, w_ukv, rw_w2, rw_a2, w_up, conv_w, conv_b, w_down):
    d = w_in.shape[0]
    z = lambda *s: jnp.zeros(s, F32)
    w_kr = w_in[:, Q_LORA + KV_LORA:MLA_PROJ]
    pad_head = lambda w: jnp.concatenate([z(d, ROPE_LO), w, z(d, HEAD_PAD - ROPE_LO - QK_ROPE)], axis=1)
    w1 = jnp.concatenate([w_in[:, :Q_LORA + KV_LORA], pad_head(w_kr), pad_head(_swap_halves(w_kr)), w_in[:, MLA_PROJ:]],
                         axis=1).astype(BF16)
    wq3 = w_uq.reshape(Q_LORA, MLA_HEADS, QK_NOPE + QK_ROPE)
    zq = lambda n: z(Q_LORA, MLA_HEADS, n)
    q_plain = jnp.concatenate([wq3, zq(HEAD_PAD - QK_NOPE - QK_ROPE)], axis=2)
    q_swap = jnp.concatenate([zq(QK_NOPE), _swap_halves(wq3[..., QK_NOPE:]), zq(HEAD_PAD - QK_NOPE - QK_ROPE)], axis=2)
    wq = jnp.concatenate([q_plain.reshape(Q_LORA, -1), q_swap.reshape(Q_LORA, -1)], axis=1).astype(BF16)
    wkv3 = w_ukv.reshape(KV_LORA, MLA_HEADS, QK_NOPE + V_DIM)
    zk = z(KV_LORA, MLA_HEADS, HEAD_PAD - QK_NOPE)
    wk = jnp.concatenate([wkv3[..., :QK_NOPE], zk], axis=2).reshape(KV_LORA, -1).astype(BF16)
    wv = wkv3[..., QK_NOPE:].reshape(KV_LORA, -1).T.astype(BF16)
    w_uk = jnp.transpose(wkv3[..., :QK_NOPE], (1, 2, 0))
    w_uk = jnp.concatenate([w_uk, z(MLA_HEADS, HEAD_PAD - QK_NOPE, KV_LORA)], axis=1).astype(BF16)
    w_uv = jnp.transpose(wkv3[..., QK_NOPE:], (1, 0, 2)).astype(BF16)
    wa2 = jnp.concatenate([jnp.concatenate([rw_w2, z(A_LORA, RW_WIDTH)], axis=0),
                           jnp.concatenate([z(DECAY_LORA, RW_WIDTH), rw_a2], axis=0)], axis=1).astype(BF16)
    f2 = w_up.shape[1]
    nch2 = f2 // FFN_CHUNK
    wup = jnp.transpose(w_up.reshape(d, nch2, FFN_CHUNK), (1, 0, 2)).astype(BF16)
    cw = jnp.transpose(conv_w.reshape(CONV_W, nch2, FFN_CHUNK), (1, 0, 2))
    cw = jnp.concatenate([cw, z(nch2, SUBLANE - CONV_W, FFN_CHUNK)], axis=1)
    cb = conv_b.reshape(nch2, 1, FFN_CHUNK)
    wdn = w_down.astype(BF16)
    return w1, wq, wk, wv, w_uk, w_uv, wa2, wup, cw, cb, wdn


def _state_to_pairs(s):
    b = s.shape[0]
    s = s.reshape(b, PAIRS, 2, RW_HEAD_DIM, RW_HEAD_DIM)
    zz = jnp.zeros_like(s[:, :, 0])
    top = jnp.concatenate([s[:, :, 0], zz], axis=-1)
    bot = jnp.concatenate([zz, s[:, :, 1]], axis=-1)
    return jnp.concatenate([top, bot], axis=-2)


def _pairs_to_state(s):
    b = s.shape[0]
    h0 = s[:, :, :RW_HEAD_DIM, :RW_HEAD_DIM]
    h1 = s[:, :, RW_HEAD_DIM:, RW_HEAD_DIM:]
    return jnp.stack([h0, h1], axis=2).reshape(b, RW_HEADS, RW_HEAD_DIM, RW_HEAD_DIM)


def _pick(n, pref):
    for t in pref:
        if n % t == 0:
            return t
    return n


def kernel(x_prompt, x_sample, cache_mla_latent, cache_mla_krope, cache_mem_k, cache_mem_v, state_rwkv, state_rwkv_shift, state_ffn_conv, page_table, mem_prompt, g_mix, w_in, q_norm_g, kv_norm_g, w_uq, w_ukv, g_mla_out, rw_mu, rw_w0, rw_w2, rw_a0, rw_a2, rw_g2, rw_k_k, rw_k_a, rw_r_k, rw_lnx_g, rw_lnx_b, w_o, g_mem_q, g_mem_kv, w_mq, w_mk, w_mv, w_mo, g_ffn, w_up, conv_w, conv_b, w_down, g_final):
    depth = w_in.shape[0]
    assert depth == 1, "single-layer step"
    bp, tp, d = x_prompt.shape
    bs, ts, _ = x_sample.shape
    npages = page_table.shape[1]
    past_len = npages * PAGE_SIZE
    row2 = lambda a: a.reshape(1, -1)
    l = 0
    w1, wq, wk, wv, w_uk, w_uv, wa2, wup, cw, cb, wdn = _prep_weights(
        w_in[l], w_uq[l], w_ukv[l], rw_w2[l], rw_a2[l], w_up[l], conv_w[l], conv_b[l], w_down[l])
    wo_a, wo_b = w_o[l, :RW_WIDTH].astype(BF16), w_o[l, RW_WIDTH:].astype(BF16)
    wmq, wmo = w_mq[l].astype(BF16), w_mo[l].astype(BF16)
    wmkv = jnp.concatenate([w_mk[l], w_mv[l]], axis=1).astype(BF16)
    g2 = rw_g2[l].astype(BF16)
    nch2 = wup.shape[0]
    n_mem_s = cache_mem_k.shape[2]
    rw_vecs = (row2(rw_mu[l]), row2(rw_w0[l]), wa2, row2(rw_a0[l]), g2, row2(rw_k_k[l]), row2(rw_k_a[l]),
               row2(rw_r_k[l]), row2(rw_lnx_g[l]), row2(rw_lnx_b[l]))

    def after_attention(x2, orw, omla, mem_k, mem_v, conv_in, b, t, tm, tm_mem, tm_ffn, prev_rows):
        if prev_rows is None:
            xm = _memblock(x2.reshape(b, t, d), orw.reshape(b, t, -1), omla.reshape(b, t, -1), row2(g_mla_out[l]),
                           wo_a, wo_b, row2(g_mem_q[l]), wmq, mem_k, mem_v, wmo, tm=tm)
        else:
            x1 = _mix(x2, orw, omla, row2(g_mla_out[l]), wo_a, wo_b, tm=tm)
            qm = _norm_mm(x1, row2(g_mem_q[l]), wmq, BF16, tm=tm)
            om = _memattn_tiled(qm.reshape(b, t, d), mem_k, mem_v, nm=n_mem_s)
            xm = _mm_res(om.reshape(b * t, d), wmo, x1, tm=tm)
        fb, ft = (b, t) if prev_rows is None else (1, b * t)
        return _ffn(xm.reshape(fb, ft, d), conv_in, row2(g_ffn[l]), wup, cw, cb, wdn, row2(g_final), tm=tm_ffn,
                    prev_rows=prev_rows)

    n_p = bp * tp
    tm_p = _pick(tp, (512, 256, 128, 64, 32, 16, 8))
    tabs_p = _rope_tables(np.arange(tp), MLA_SCALE * LOG2E)
    xp2 = x_prompt.reshape(n_p, d)
    tq = _pick(tm_p, (FLASH_T, 256, 128))
    q_p, lat_p, kr_p, prw_p, k_p, v_p = _inproj(
        xp2, row2(g_mix[l]), w1, row2(q_norm_g[l]), row2(kv_norm_g[l]), wq, tabs_p, wk, wv, tm=tm_p, with_kv=True, seq=tp, vblock=tq)
    omla_p = _flash(q_p, k_p, v_p, batch=bp, seq=tp, tq=tq)
    tb_p = _pick(tp, (512, 256, 128, 64, 32, 16, 8))
    if tb_p % QCH == 0:
        zeros_state = jnp.zeros((bp, RW_HEADS // QUAD, QW, QW), F32)
        orw_p, st_p = _rwkv_quad(prw_p.reshape(bp, tp, RW_PROJ), zeros_state, *rw_vecs, tb=tb_p)
        st_p = _quads_to_state(st_p)
    else:
        zeros_state = jnp.zeros((bp, PAIRS, LANE, LANE), F32)
        orw_p, st_p = _rwkv(prw_p.reshape(bp, tp, RW_PROJ), zeros_state, *rw_vecs, tb=tb_p,
                            ch=_pick(tb_p, (32, 16, 8)), row_lo=0, row_hi=tb_p)
        st_p = _pairs_to_state(st_p)
    mkv = _norm_mm(mem_prompt.reshape(-1, d), row2(g_mem_kv[l]), wmkv, F32, tm=_pick(mem_prompt.shape[0] * mem_prompt.shape[1], (512, 256, 128, 8)))
    n_mem = mem_prompt.shape[1]
    mk_p = mkv[:, :d].reshape(bp, n_mem, d)
    mv_p = mkv[:, d:].reshape(bp, n_mem, d)
    conv0_p = jnp.zeros((bp, nch2, SUBLANE, FFN_CHUNK), F32)
    y_p, u_p = after_attention(xp2, orw_p.reshape(n_p, RW_WIDTH), omla_p, mk_p, mv_p, conv0_p, bp, tp, tm_p, tm_p, tm_p, None)
    conv_p = jnp.transpose(u_p[:, :, SUBLANE - (CONV_W - 1):, :], (0, 2, 1, 3)).reshape(bp, CONV_W - 1, nch2 * FFN_CHUNK)

    n_s = bs * SROWS
    xs3 = jnp.pad(x_sample, ((0, 0), (SLO, SROWS - SLO - ts), (0, 0)))
    pos_s = np.tile(np.concatenate([np.zeros(SLO), past_len + np.arange(ts), np.zeros(SROWS - SLO - ts)]), bs)
    tm_s = _pick(n_s, (1024, 512, 256, 128, 64, 32, 16, 8))
    tabs_s = _rope_tables(pos_s[:tm_s], MLA_SCALE)
    q_s, lat_s, kr_s, prw_s = _inproj(
        xs3.reshape(n_s, d), row2(g_mix[l]), w1, row2(q_norm_g[l]), row2(kv_norm_g[l]), wq, tabs_s, None, None,
        tm=tm_s, with_kv=False)
    qabs = _bmm_cols(q_s, w_uk, BF16)
    qabs = qabs.reshape(MLA_HEADS, bs, SROWS, KV_LORA)[:, :, SLO:SLO + ts]
    qabs = jnp.transpose(qabs, (1, 0, 2, 3)).reshape(bs, MLA_HEADS * ts, KV_LORA)
    qrope = q_s.reshape(bs, SROWS, MLA_HEADS, HEAD_PAD)[:, SLO:SLO + ts, :, ROPE_LO:ROPE_LO + QK_ROPE]
    qrope = jnp.transpose(qrope, (0, 2, 1, 3)).reshape(bs, MLA_HEADS * ts, QK_ROPE)
    olat = _paged(page_table, qabs, qrope, lat_s.reshape(bs, SROWS, KV_LORA), kr_s.reshape(bs, SROWS, QK_ROPE),
                  cache_mla_latent[l], jnp.swapaxes(cache_mla_krope[l], 1, 2), n_new=ts)
    olat = jnp.transpose(olat.reshape(bs, MLA_HEADS, ts, KV_LORA), (1, 0, 2, 3))
    olat = jnp.pad(olat, ((0, 0), (0, 0), (SLO, SROWS - SLO - ts), (0, 0))).reshape(MLA_HEADS, n_s, KV_LORA)
    omla_s = _bmm(olat, w_uv, F32)
    omla_s = jnp.transpose(omla_s, (1, 0, 2)).reshape(n_s, MLA_HEADS * V_DIM)
    prw_s3 = prw_s.reshape(bs, SROWS, RW_PROJ).at[:, SLO - 1, :].set(state_rwkv_shift[l])
    orw_s, st_s = _rwkv(prw_s3, _state_to_pairs(state_rwkv[l]), *rw_vecs, tb=SROWS, ch=SROWS, row_lo=SLO, row_hi=SLO + ts)
    hist = jnp.transpose(state_ffn_conv[l].reshape(bs, CONV_W - 1, nch2, FFN_CHUNK), (2, 0, 1, 3))
    hist = jnp.pad(hist, ((0, 0), (0, 0), (SLO - (CONV_W - 1), SROWS - SLO), (0, 0))).reshape(nch2, n_s, FFN_CHUNK)
    def mem_rows(c):
        _, nm, hh, dh = c.shape
        return jnp.transpose(c.reshape(bs, nm, hh, dh // LANE, LANE), (0, 1, 3, 2, 4)).reshape(bs, nm * hh * (dh // LANE), LANE)
    mk_s, mv_s = mem_rows(cache_mem_k[l]), mem_rows(cache_mem_v[l])
    y_s, u_s = after_attention(xs3.reshape(n_s, d), orw_s.reshape(n_s, RW_WIDTH), omla_s, mk_s, mv_s, hist,
                               bs, SROWS, tm_s, SROWS, _pick(n_s, (256, 128, 64, 32, 16, 8)), SLO)
    u_s = u_s.reshape(nch2, bs, SROWS, FFN_CHUNK)[:, :, SLO + ts - (CONV_W - 1):SLO + ts]
    conv_s = jnp.transpose(u_s, (1, 2, 0, 3)).reshape(bs, CONV_W - 1, nch2 * FFN_CHUNK)

    real = lambda a, w: a.reshape(bs, SROWS, w)[:, SLO:SLO + ts]
    mem5 = lambda a: a.reshape(1, bp, n_mem, MEM_HEADS, d // MEM_HEADS)
    return (y_p, real(y_s, d),
            lat_p.reshape(1, bp, tp, KV_LORA), kr_p.reshape(1, bp, tp, QK_ROPE), mem5(mk_p), mem5(mv_p),
            st_p[None], prw_p.reshape(bp, tp, RW_PROJ)[:, -1][None], conv_p[None],
            real(lat_s, KV_LORA)[None], real(kr_s, QK_ROPE)[None], _pairs_to_state(st_s)[None],
            prw_s.reshape(bs, SROWS, RW_PROJ)[:, SLO + ts - 1][None], conv_s[None])
```
